```python
import jax, jax.numpy as jnp
from jax import lax
import numpy as np

D_MODEL = 1024
BATCH = 2
SEQ = 8192
DEPTH = 1

HEAD_DIM = 64
DSA_GROUPS = ((128, 1), (512, 4), (2048, 16))
DSA_HEADS_PER_GROUP = 4
DSA_HEADS = DSA_HEADS_PER_GROUP * len(DSA_GROUPS)
MOBA_HEADS = 8
MOBA_BLOCK = 256
MOBA_TOPK = 3
MOBA_Q_CHUNK = 64
D_FF = ((8 * D_MODEL // 3 + 127) // 128) * 128
DSA_WIDTH = DSA_HEADS * HEAD_DIM
MOBA_WIDTH = MOBA_HEADS * HEAD_DIM
DSA_OUT = DSA_HEADS_PER_GROUP * HEAD_DIM
IN_WIDTH = 3 * DSA_WIDTH + 3 * MOBA_WIDTH + 2 * D_MODEL
EPS = 1e-6
NEG = -1e30

kernel_name = "hybrid_dilated_moba_macaron"

f32 = jnp.float32


def rmsnorm(x, g):
    xf = x.astype(f32)
    y = xf * lax.rsqrt(jnp.mean(xf * xf, axis=-1, keepdims=True) + EPS)
    return (y * g.astype(f32)).astype(x.dtype)


def swiglu(x, w_gate, w_up, w_down):
    return (jax.nn.silu(x @ w_gate) * (x @ w_up)) @ w_down


def alibi_slopes(n):
    return 2.0 ** (-8.0 * jnp.arange(1, n + 1, dtype=f32) / n)


def to_heads(a, n):
    b, s, _ = a.shape
    return a.reshape(b, s, n, HEAD_DIM).transpose(0, 2, 1, 3)


def dilated_window_attention(q, k, v, slopes, window, dilation):
    b, h, s, hd = q.shape
    blk = window // dilation
    L = s // dilation
    Lp = -(-L // blk) * blk
    nb = Lp // blk

    def to_residue(a):
        a = a.reshape(b, h, L, dilation, hd).transpose(0, 1, 3, 2, 4)
        return jnp.pad(a, ((0, 0), (0, 0), (0, 0), (0, Lp - L), (0, 0)))

    qr, kr, vr = to_residue(q), to_residue(k), to_residue(v)
    qb = qr.reshape(b, h, dilation, nb, blk, hd)

    def band(a):
        ap = jnp.pad(a, ((0, 0), (0, 0), (0, 0), (blk, 0), (0, 0)))
        prev = ap[:, :, :, :Lp].reshape(b, h, dilation, nb, blk, hd)
        cur = a.reshape(b, h, dilation, nb, blk, hd)
        return jnp.concatenate([prev, cur], axis=4)

    kb, vb = band(kr), band(vr)
    scores = jnp.einsum('bhrnqd,bhrnkd->bhrnqk', qb, kb).astype(f32)
    qi = jnp.arange(blk)[:, None]
    kj = jnp.arange(2 * blk)[None, :]
    delta = qi + blk - kj
    kpos = jnp.arange(nb)[:, None, None] * blk - blk + kj[None]
    valid = (delta >= 0)[None] & (delta <= blk)[None] & (kpos >= 0)
    bias = -slopes[:, None, None] * (delta * dilation).astype(f32)[None]
    scores = jnp.where(valid[None, None, None], scores + bias[None, :, None, None], NEG)
    lse = jax.nn.logsumexp(scores, axis=-1)
    p = jnp.exp(scores - lse[..., None]).astype(v.dtype)
    out = jnp.einsum('bhrnqk,bhrnkd->bhrnqd', p, vb)
    out = out.reshape(b, h, dilation, Lp, hd)[:, :, :, :L]
    out = out.transpose(0, 1, 3, 2, 4).reshape(b, h, s, hd)
    lse = lse.reshape(b, h, dilation, Lp)[..., :L].transpose(0, 1, 3, 2).reshape(b, h, s)
    return out, lse


def moba_attention(q, k, v, slopes):
    b, h, s, hd = q.shape
    nblk = -(-s // MOBA_BLOCK)
    sp = nblk * MOBA_BLOCK
    n_sel = min(MOBA_TOPK, nblk)
    padw = ((0, 0), (0, 0), (0, sp - s), (0, 0))
    q, k, v = jnp.pad(q, padw), jnp.pad(k, padw), jnp.pad(v, padw)
    bh = b * h
    kblk = k.reshape(bh, nblk, MOBA_BLOCK, hd)
    vblk = v.reshape(bh, nblk, MOBA_BLOCK, hd)
    qf = q.reshape(bh, sp, hd)
    kmean = jnp.mean(kblk.astype(f32), axis=2)
    gate = jnp.einsum('xtd,xnd->xtn', qf.astype(f32), kmean)
    own = jnp.arange(sp) // MOBA_BLOCK
    past = jnp.arange(nblk)[None, :] < own[:, None]
    gate = jnp.where(past[None], gate, NEG)
    _, sel = lax.top_k(gate, n_sel)
    slope_x = jnp.tile(slopes, b)
    nq = sp // MOBA_Q_CHUNK
    q_chunks = qf.reshape(bh, nq, MOBA_Q_CHUNK, hd).transpose(1, 0, 2, 3)
    sel_chunks = sel.reshape(bh, nq, MOBA_Q_CHUNK, n_sel).transpose(1, 0, 2, 3)
    take = jax.vmap(lambda blocks, ix: blocks[ix])
    blk_off = jnp.arange(MOBA_BLOCK)

    def chunk(args):
        qc, ix, c = args
        t = c * MOBA_Q_CHUNK + jnp.arange(MOBA_Q_CHUNK)
        own_blk = (c * MOBA_Q_CHUNK) // MOBA_BLOCK
        k_own = lax.dynamic_index_in_dim(kblk, own_blk, axis=1, keepdims=False)
        v_own = lax.dynamic_index_in_dim(vblk, own_blk, axis=1, keepdims=False)
        d_own = t[:, None] - (own_blk * MOBA_BLOCK + blk_off)[None, :]
        s_own = jnp.einsum('xqd,xkd->xqk', qc, k_own).astype(f32)
        s_own = jnp.where(d_own[None] >= 0,
                          s_own - slope_x[:, None, None] * d_own.astype(f32)[None], NEG)
        k_sel = take(kblk, ix)
        v_sel = take(vblk, ix)
        s_sel = jnp.einsum('xqd,xqjkd->xqjk', qc, k_sel).astype(f32)
        d_sel = t[None, :, None, None] - (ix[..., None] * MOBA_BLOCK + blk_off)
        ok = (ix < own_blk)[..., None]
        s_sel = jnp.where(ok, s_sel - slope_x[:, None, None, None] * d_sel.astype(f32), NEG)
        n_k = n_sel * MOBA_BLOCK
        scores = jnp.concatenate([s_sel.reshape(bh, MOBA_Q_CHUNK, n_k), s_own], axis=-1)
        p = jax.nn.softmax(scores, axis=-1).astype(v.dtype)
        p_sel = p[..., :n_k].reshape(bh, MOBA_Q_CHUNK, n_sel, MOBA_BLOCK)
        p_own = p[..., n_k:]
        return (jnp.einsum('xqjk,xqjkd->xqd', p_sel, v_sel)
                + jnp.einsum('xqk,xkd->xqd', p_own, v_own))

    out = lax.map(chunk, (q_chunks, sel_chunks, jnp.arange(nq)))
    return out.transpose(1, 0, 2, 3).reshape(b, h, sp, hd)[:, :, :s]


def hybrid_layer(x, norm_ffn1, ffn1_gate, ffn1_up, ffn1_down, norm_mix, w_in,
                 w_up_a, w_up_b, w_out, norm_ffn2, ffn2_gate, ffn2_up, ffn2_down):
    b, s, _ = x.shape
    x = x + 0.5 * swiglu(rmsnorm(x, norm_ffn1), ffn1_gate, ffn1_up, ffn1_down)
    h = rmsnorm(x, norm_mix)
    proj = h @ w_in
    o = 0
    parts = []
    for width in (DSA_WIDTH, DSA_WIDTH, DSA_WIDTH, MOBA_WIDTH, MOBA_WIDTH, MOBA_WIDTH,
                  D_MODEL, D_MODEL):
        parts.append(proj[..., o:o + width])
        o += width
    qa, ka, va, qb, kb, vb, g_a, g_b = parts
    scale = HEAD_DIM ** -0.5
    qa, ka, va = to_heads(qa, DSA_HEADS) * scale, to_heads(ka, DSA_HEADS), to_heads(va, DSA_HEADS)
    qb, kb, vb = to_heads(qb, MOBA_HEADS) * scale, to_heads(kb, MOBA_HEADS), to_heads(vb, MOBA_HEADS)

    slopes_a = alibi_slopes(DSA_HEADS)
    outs, lses = [], []
    for gi, (window, dilation) in enumerate(DSA_GROUPS):
        hs = slice(gi * DSA_HEADS_PER_GROUP, (gi + 1) * DSA_HEADS_PER_GROUP)
        og, lg = dilated_window_attention(qa[:, hs], ka[:, hs], va[:, hs], slopes_a[hs],
                                          window, dilation)
        outs.append(og)
        lses.append(lg)
    w_grp = jax.nn.softmax(jnp.stack(lses, 0), axis=0)
    y_a = jnp.sum(w_grp[..., None] * jnp.stack(outs, 0).astype(f32), axis=0).astype(x.dtype)
    y_a = y_a.transpose(0, 2, 1, 3).reshape(b, s, DSA_OUT)

    y_b = moba_attention(qb, kb, vb, alibi_slopes(MOBA_HEADS))
    y_b = y_b.transpose(0, 2, 1, 3).reshape(b, s, MOBA_WIDTH)

    merged = jax.nn.sigmoid(g_a) * (y_a @ w_up_a) + jax.nn.sigmoid(g_b) * (y_b @ w_up_b)
    x = x + merged @ w_out
    x = x + 0.5 * swiglu(rmsnorm(x, norm_ffn2), ffn2_gate, ffn2_up, ffn2_down)
    return x


def setup_inputs(seed: int = 0) -> dict:
    key = jax.random.key(seed)
    ks = jax.random.split(key, 16)

    def dense(k, fan_in, fan_out):
        return jax.random.normal(k, (DEPTH, fan_in, fan_out), f32) * fan_in ** -0.5

    def gain(k):
        return 1.0 + 0.02 * jax.random.normal(k, (DEPTH, D_MODEL), f32)

    return {
        "x": jax.random.normal(ks[0], (BATCH, SEQ, D_MODEL), f32),
        "norm_ffn1": gain(ks[1]),
        "ffn1_gate": dense(ks[2], D_MODEL, D_FF),
        "ffn1_up": dense(ks[3], D_MODEL, D_FF),
        "ffn1_down": dense(ks[4], D_FF, D_MODEL),
        "norm_mix": gain(ks[5]),
        "w_in": dense(ks[6], D_MODEL, IN_WIDTH),
        "w_up_a": dense(ks[7], DSA_OUT, D_MODEL),
        "w_up_b": dense(ks[8], MOBA_WIDTH, D_MODEL),
        "w_out": dense(ks[9], D_MODEL, D_MODEL),
        "norm_ffn2": gain(ks[10]),
        "ffn2_gate": dense(ks[11], D_MODEL, D_FF),
        "ffn2_up": dense(ks[12], D_MODEL, D_FF),
        "ffn2_down": dense(ks[13], D_FF, D_MODEL),
        "norm_final": 1.0 + 0.02 * jax.random.normal(ks[14], (D_MODEL,), f32),
    }


def reference(x, norm_ffn1, ffn1_gate, ffn1_up, ffn1_down, norm_mix, w_in, w_up_a,
              w_up_b, w_out, norm_ffn2, ffn2_gate, ffn2_up, ffn2_down, norm_final):
    for layer in range(DEPTH):
        x = hybrid_layer(x, norm_ffn1[layer], ffn1_gate[layer], ffn1_up[layer],
                         ffn1_down[layer], norm_mix[layer], w_in[layer], w_up_a[layer],
                         w_up_b[layer], w_out[layer], norm_ffn2[layer], ffn2_gate[layer],
                         ffn2_up[layer], ffn2_down[layer])
    return rmsnorm(x, norm_final)
```

```python
import functools

import numpy as np
import jax
import jax.numpy as jnp
from jax import lax
from jax.experimental import pallas as pl
from jax.experimental.pallas import tpu as pltpu

D_MODEL = 1024
HEAD_DIM = 64
DSA_GROUPS = ((128, 1), (512, 4), (2048, 16))
DSA_HEADS_PER_GROUP = 4
DSA_HEADS = DSA_HEADS_PER_GROUP * len(DSA_GROUPS)
MOBA_HEADS = 8
MOBA_BLOCK = 256
MOBA_TOPK = 3
D_FF = ((8 * D_MODEL // 3 + 127) // 128) * 128
DSA_WIDTH = DSA_HEADS * HEAD_DIM
MOBA_WIDTH = MOBA_HEADS * HEAD_DIM
DSA_OUT = DSA_HEADS_PER_GROUP * HEAD_DIM
EPS = 1e-6
NEG = -1e30

LANES = 128
HEAD_PAIR = 2 * HEAD_DIM
DSA_BLK = 128
V_ROWS = 80
K_FEAT = 128
VMEM_LIMIT = 56 * 1024 * 1024

f32 = jnp.float32
bf16 = jnp.bfloat16


def _rmsnorm(x, g):
    return x * lax.rsqrt(jnp.mean(x * x, axis=-1, keepdims=True) + EPS) * g


def _dot(a, b):
    return jnp.dot(a, b, preferred_element_type=f32)


def _dot_nt(a, b):
    return lax.dot_general(a, b, (((1,), (1,)), ((), ())), preferred_element_type=f32)


def _dot_tn(a, b):
    return lax.dot_general(a, b, (((0,), (0,)), ((), ())), preferred_element_type=f32)


def _resident(shape):
    nd = len(shape)
    return pl.BlockSpec(shape, lambda *_: (0,) * nd, pipeline_mode=pl.Buffered(1))


def _ffn_kernel(x_ref, g_ref, wg_ref, wu_ref, wd_ref, gf_ref, o_ref, *, final_norm):
    x = x_ref[...]
    h = _rmsnorm(x, g_ref[...]).astype(bf16)
    gate = _dot(h, wg_ref[...])
    up = _dot(h, wu_ref[...])
    act = (gate * jax.nn.sigmoid(gate) * up).astype(bf16)
    y = x + 0.5 * _dot(act, wd_ref[...])
    if final_norm:
        y = _rmsnorm(y, gf_ref[...])
    o_ref[...] = y


def _ffn(x2d, gain, w_gate, w_up, w_down, gain_final, *, final_norm, tm):
    n = x2d.shape[0]
    return pl.pallas_call(
        functools.partial(_ffn_kernel, final_norm=final_norm),
        out_shape=jax.ShapeDtypeStruct((n, D_MODEL), f32),
        grid=(n // tm,),
        in_specs=[
            pl.BlockSpec((tm, D_MODEL), lambda i: (i, 0)),
            _resident((1, D_MODEL)),
            _resident((D_MODEL, D_FF)),
            _resident((D_MODEL, D_FF)),
            _resident((D_FF, D_MODEL)),
            _resident((1, D_MODEL)),
        ],
        out_specs=pl.BlockSpec((tm, D_MODEL), lambda i: (i, 0)),
        compiler_params=pltpu.CompilerParams(
            dimension_semantics=("parallel",), vmem_limit_bytes=VMEM_LIMIT),
        name="ffn_final" if final_norm else "ffn",
    )(x2d, gain, w_gate, w_up, w_down, gain_final)


def _proj_kernel(x_ref, g_ref, wa_ref, wkb_ref, wqvt_ref, wgate_ref, kfeat_ref, vpad_ref,
                 qkva_ref, kaug_ref, qt_ref, vaug_ref, ksum_ref, gates_ref, *, tm):
    h = _rmsnorm(x_ref[0], g_ref[...]).astype(bf16)
    qkva_ref[0] = _dot(h, wa_ref[...]).astype(bf16)
    kb = _dot(h, wkb_ref[...])
    ksum_ref[...] = jnp.sum(kb.reshape(tm // MOBA_BLOCK, MOBA_BLOCK, MOBA_WIDTH), axis=1)[:, None, :]
    kb = kb.astype(bf16)
    for p in range(MOBA_HEADS // 2):
        kaug_ref[0, p, :, 0:HEAD_PAIR] = kb[:, p * HEAD_PAIR:(p + 1) * HEAD_PAIR]
        kaug_ref[0, p, :, HEAD_PAIR:HEAD_PAIR + K_FEAT] = kfeat_ref[...]
    qv = _dot_nt(wqvt_ref[...], h)
    qt_ref[0] = qv[0:MOBA_WIDTH].astype(bf16)
    for hd in range(MOBA_HEADS):
        lo = MOBA_WIDTH + hd * HEAD_DIM
        vaug_ref[0, hd, 0:HEAD_DIM, :] = qv[lo:lo + HEAD_DIM].astype(bf16)
        vaug_ref[0, hd, HEAD_DIM:V_ROWS, :] = vpad_ref[...]
    gates_ref[0] = jax.nn.sigmoid(_dot(h, wgate_ref[...])).astype(bf16)


def _proj(x3d, gain, wa, wkb, wqvt, wgate, kfeat, vpad, *, tm):
    b, s, _ = x3d.shape
    nblk = s // MOBA_BLOCK
    out_shape = (
        jax.ShapeDtypeStruct((b, s, 3 * DSA_WIDTH), bf16),
        jax.ShapeDtypeStruct((b, MOBA_HEADS // 2, s, HEAD_PAIR + K_FEAT), bf16),
        jax.ShapeDtypeStruct((b, MOBA_WIDTH, s), bf16),
        jax.ShapeDtypeStruct((b, MOBA_HEADS, V_ROWS, s), bf16),
        jax.ShapeDtypeStruct((b * nblk, 1, MOBA_WIDTH), f32),
        jax.ShapeDtypeStruct((b, s, 2 * D_MODEL), bf16),
    )
    per = tm // MOBA_BLOCK
    return pl.pallas_call(
        functools.partial(_proj_kernel, tm=tm),
        out_shape=out_shape,
        grid=(b, s // tm),
        in_specs=[
            pl.BlockSpec((1, tm, D_MODEL), lambda bi, i: (bi, i, 0)),
            _resident((1, D_MODEL)),
            _resident((D_MODEL, 3 * DSA_WIDTH)),
            _resident((D_MODEL, MOBA_WIDTH)),
            _resident((2 * MOBA_WIDTH, D_MODEL)),
            _resident((D_MODEL, 2 * D_MODEL)),
            pl.BlockSpec((tm, K_FEAT), lambda bi, i: (i, 0)),
            _resident((V_ROWS - HEAD_DIM, tm)),
        ],
        out_specs=(
            pl.BlockSpec((1, tm, 3 * DSA_WIDTH), lambda bi, i: (bi, i, 0)),
            pl.BlockSpec((1, MOBA_HEADS // 2, tm, HEAD_PAIR + K_FEAT), lambda bi, i: (bi, 0, i, 0)),
            pl.BlockSpec((1, MOBA_WIDTH, tm), lambda bi, i: (bi, 0, i)),
            pl.BlockSpec((1, MOBA_HEADS, V_ROWS, tm), lambda bi, i: (bi, 0, 0, i)),
            pl.BlockSpec((per, 1, MOBA_WIDTH), lambda bi, i: (bi * (s // tm) + i, 0, 0)),
            pl.BlockSpec((1, tm, 2 * D_MODEL), lambda bi, i: (bi, i, 0)),
        ),
        compiler_params=pltpu.CompilerParams(
            dimension_semantics=("parallel", "parallel"), vmem_limit_bytes=VMEM_LIMIT),
        name="proj",
    )(x3d, gain, wa, wkb, wqvt, wgate, kfeat, vpad)


def _dsa_kernel(q_ref, k_ref, v_ref, bias_ref, o_ref, *, nblk):
    blk = DSA_BLK
    lane = lax.broadcasted_iota(jnp.int32, (blk, LANES), 1)
    qi = lax.broadcasted_iota(jnp.int32, (blk, 2 * blk), 0)
    kj = lax.broadcasted_iota(jnp.int32, (blk, 2 * blk), 1)
    delta = qi + blk - kj
    valid = (delta >= 0) & (delta <= blk)

    def attend(row0, q, kband, vband, ok, bias_of):
        outs, lses = [], []
        for h in range(2):
            mine = (lane >= HEAD_DIM) if h else (lane < HEAD_DIM)
            qm = jnp.where(mine, q, jnp.zeros_like(q))
            s = _dot_nt(qm, kband)
            s = jnp.where(ok, s + bias_of(h), NEG)
            m = jnp.max(s, axis=-1, keepdims=True)
            p = jnp.exp(s - m)
            l = jnp.sum(p, axis=-1, keepdims=True)
            o = _dot(p.astype(bf16), vband)
            outs.append(o / l)
            lses.append(jnp.broadcast_to(m + jnp.log(l), (blk, LANES)))
        first = lane < HEAD_DIM
        o_ref[0, pl.ds(row0, blk), 0:LANES] = jnp.where(first, outs[0], outs[1])
        o_ref[0, pl.ds(row0, blk), LANES:2 * LANES] = jnp.where(first, lses[0], lses[1])

    attend(0, q_ref[0, 0:blk, :], k_ref[0, 0:blk, :], v_ref[0, 0:blk, :], valid[:, blk:],
           lambda h: bias_ref[0, h, :, blk:])

    def body(i, carry):
        row0 = pl.multiple_of(i * blk, blk)
        band0 = pl.multiple_of((i - 1) * blk, blk)
        attend(row0, q_ref[0, pl.ds(row0, blk), :], k_ref[0, pl.ds(band0, 2 * blk), :],
               v_ref[0, pl.ds(band0, 2 * blk), :], valid, lambda h: bias_ref[0, h])
        return carry

    lax.fori_loop(1, nblk, body, 0)


def _dsa_group(qkva, bias, *, group, dilation):
    b, s, _ = qkva.shape
    length = s // dilation
    width = 3 * DSA_WIDTH
    view = qkva.reshape(b, length, dilation * width)
    cols = width // LANES
    qcol = group * 2
    kcol = DSA_WIDTH // LANES + group * 2
    vcol = 2 * DSA_WIDTH // LANES + group * 2
    out = pl.pallas_call(
        functools.partial(_dsa_kernel, nblk=length // DSA_BLK),
        out_shape=jax.ShapeDtypeStruct((b, length, dilation * 4 * LANES), f32),
        grid=(b, dilation, 2),
        in_specs=[
            pl.BlockSpec((1, length, LANES), lambda bi, r, hp: (bi, 0, r * cols + qcol + hp)),
            pl.BlockSpec((1, length, LANES), lambda bi, r, hp: (bi, 0, r * cols + kcol + hp)),
            pl.BlockSpec((1, length, LANES), lambda bi, r, hp: (bi, 0, r * cols + vcol + hp)),
            pl.BlockSpec((1, 2, DSA_BLK, 2 * DSA_BLK), lambda bi, r, hp: (hp, 0, 0, 0)),
        ],
        out_specs=pl.BlockSpec((1, length, 2 * LANES), lambda bi, r, hp: (bi, 0, r * 2 + hp)),
        compiler_params=pltpu.CompilerParams(
            dimension_semantics=("parallel", "parallel", "parallel"), vmem_limit_bytes=VMEM_LIMIT),
        name=f"dsa_g{group}",
    )(view, view, view, bias)
    return out.reshape(b, s, 4 * LANES)


def _moba_kernel(slope_ref, qt_ref, kaug_ref, vaug_ref, ksum_ref, arow_ref, o_ref,
                 rhs_ref, acc_ref, m_ref, *, nblk):
    hp = pl.program_id(1)
    n = pl.program_id(2)
    blk = MOBA_BLOCK
    qpair = qt_ref[0]
    rowi = lax.broadcasted_iota(jnp.int32, (HEAD_PAIR, blk), 0)
    bidx = lax.broadcasted_iota(jnp.int32, (nblk, blk), 0)
    kmean = ksum_ref[0] * (1.0 / blk)
    kmean_hi = kmean.astype(bf16)
    kmean_lo = (kmean - kmean_hi.astype(f32)).astype(bf16)
    krow = lax.broadcasted_iota(jnp.int32, (blk, blk), 0)
    qcol = lax.broadcasted_iota(jnp.int32, (blk, blk), 1)
    causal = krow <= qcol
    own0 = pl.multiple_of(n * blk, blk)
    k_own = kaug_ref[0, 0, pl.ds(own0, blk), :]

    for h in range(2):
        mine = (rowi >= HEAD_DIM) if h else (rowi < HEAD_DIM)
        qpad = jnp.where(mine, qpair, jnp.zeros_like(qpair))
        gate = _dot(kmean_hi, qpad) + _dot(kmean_lo, qpad)
        past = bidx < n
        g = jnp.where(past, gate, NEG)
        sel = bidx == n
        for _ in range(MOBA_TOPK):
            mx = jnp.max(g, axis=0, keepdims=True)
            idx = jnp.min(jnp.where(g == mx, bidx, nblk), axis=0, keepdims=True)
            pick = bidx == idx
            sel = sel | (pick & past)
            g = jnp.where(pick, -jnp.inf, g)
        maskbias = jnp.where(sel, 0.0, NEG).astype(bf16)
        rhs_ref[h, 0:HEAD_PAIR, :] = qpad
        rhs_ref[h, HEAD_PAIR:HEAD_PAIR + nblk, :] = maskbias
        rhs_ref[h, HEAD_PAIR + nblk:HEAD_PAIR + K_FEAT, :] = arow_ref[0, h]

        s = _dot(k_own, rhs_ref[h])
        s = jnp.where(causal, s, NEG)
        m = jnp.max(s, axis=0, keepdims=True)
        p = jnp.exp(s - m).astype(bf16)
        acc_ref[h] = _dot(vaug_ref[0, h, :, pl.ds(own0, blk)], p)
        m_ref[h] = m

    def body(j, carry):
        k0 = pl.multiple_of(j * blk, blk)
        kt = kaug_ref[0, 0, pl.ds(k0, blk), :]
        for h in range(2):
            shift = slope_ref[2 * hp + h] * (blk * (j - n)).astype(f32)
            s = _dot(kt, rhs_ref[h])
            m_old = m_ref[h]
            m_new = jnp.maximum(m_old, jnp.max(s, axis=0, keepdims=True) + shift)
            p = jnp.exp(s - (m_new - shift)).astype(bf16)
            acc_ref[h] = jnp.exp(m_old - m_new) * acc_ref[h] + _dot(vaug_ref[0, h, :, pl.ds(k0, blk)], p)
            m_ref[h] = m_new
        return carry

    lax.fori_loop(0, n, body, 0)

    for h in range(2):
        acc = acc_ref[h]
        o_ref[0, h * HEAD_DIM:(h + 1) * HEAD_DIM, :] = (
            acc[0:HEAD_DIM] / acc[HEAD_DIM:HEAD_DIM + 1]).astype(bf16)


def _moba(slopes, qt, kaug, vaug, ksum, arows):
    b, _, s = qt.shape
    nblk = s // MOBA_BLOCK
    assert nblk + arows.shape[2] == K_FEAT
    pairs = MOBA_HEADS // 2
    return pl.pallas_call(
        functools.partial(_moba_kernel, nblk=nblk),
        out_shape=jax.ShapeDtypeStruct((b, MOBA_WIDTH, s), bf16),
        grid=(b, pairs, nblk),
        in_specs=[
            pl.BlockSpec(memory_space=pltpu.SMEM),
            pl.BlockSpec((1, HEAD_PAIR, MOBA_BLOCK), lambda bi, hp, n: (bi, hp, n)),
            pl.BlockSpec((1, 1, s, HEAD_PAIR + K_FEAT), lambda bi, hp, n: (bi, hp, 0, 0)),
            pl.BlockSpec((1, 2, V_ROWS, s), lambda bi, hp, n: (bi, hp, 0, 0)),
            pl.BlockSpec((1, nblk, HEAD_PAIR), lambda bi, hp, n: (bi, 0, hp)),
            pl.BlockSpec((1, 2, K_FEAT - nblk, MOBA_BLOCK), lambda bi, hp, n: (hp, 0, 0, 0)),
        ],
        out_specs=pl.BlockSpec((1, HEAD_PAIR, MOBA_BLOCK), lambda bi, hp, n: (bi, hp, n)),
        scratch_shapes=[
            pltpu.VMEM((2, HEAD_PAIR + K_FEAT, MOBA_BLOCK), bf16),
            pltpu.VMEM((2, V_ROWS, MOBA_BLOCK), f32),
            pltpu.VMEM((2, 1, MOBA_BLOCK), f32),
        ],
        compiler_params=pltpu.CompilerParams(
            dimension_semantics=("parallel", "parallel", "arbitrary"), vmem_limit_bytes=VMEM_LIMIT),
        name="moba",
    )(slopes, qt, kaug, vaug, ksum.reshape(b, nblk, MOBA_WIDTH), arows)


def _merge_kernel(x_ref, gates_ref, d0_ref, d1_ref, d2_ref, ybt_ref, wua_ref, wub_ref, wo_ref, o_ref):
    groups = (d0_ref[0], d1_ref[0], d2_ref[0])
    pairs = []
    for hp in range(2):
        outs = [d[:, hp * 2 * LANES:hp * 2 * LANES + LANES] for d in groups]
        lses = [d[:, hp * 2 * LANES + LANES:(hp + 1) * 2 * LANES] for d in groups]
        mx = jnp.maximum(jnp.maximum(lses[0], lses[1]), lses[2])
        es = [jnp.exp(l - mx) for l in lses]
        den = es[0] + es[1] + es[2]
        pairs.append((es[0] * outs[0] + es[1] * outs[1] + es[2] * outs[2]) / den)
    ya = jnp.concatenate(pairs, axis=1).astype(bf16)
    ta = _dot(ya, wua_ref[...])
    tb = _dot_tn(ybt_ref[0], wub_ref[...])
    gates = gates_ref[0]
    merged = gates[:, 0:D_MODEL].astype(f32) * ta + gates[:, D_MODEL:].astype(f32) * tb
    o_ref[0] = x_ref[0] + _dot(merged.astype(bf16), wo_ref[...])


def _merge(x3d, gates, dsa_outs, ybt, wua, wub, wo, *, tm):
    b, s, _ = x3d.shape
    tok = lambda w: pl.BlockSpec((1, tm, w), lambda bi, i: (bi, i, 0))
    return pl.pallas_call(
        _merge_kernel,
        out_shape=jax.ShapeDtypeStruct((b, s, D_MODEL), f32),
        grid=(b, s // tm),
        in_specs=[
            tok(D_MODEL), tok(2 * D_MODEL), tok(4 * LANES), tok(4 * LANES), tok(4 * LANES),
            pl.BlockSpec((1, MOBA_WIDTH, tm), lambda bi, i: (bi, 0, i)),
            _resident((DSA_OUT, D_MODEL)),
            _resident((MOBA_WIDTH, D_MODEL)),
            _resident((D_MODEL, D_MODEL)),
        ],
        out_specs=tok(D_MODEL),
        compiler_params=pltpu.CompilerParams(
            dimension_semantics=("parallel", "parallel"), vmem_limit_bytes=VMEM_LIMIT),
        name="merge",
    )(x3d, gates, *dsa_outs, ybt, wua, wub, wo)


def _alibi_slopes(n):
    return 2.0 ** (-8.0 * np.arange(1, n + 1, dtype=np.float64) / n)


def _split3(x):
    hi = x.astype(bf16)
    r1 = x - hi.astype(f32)
    mid = r1.astype(bf16)
    lo = (r1 - mid.astype(f32)).astype(bf16)
    return hi, mid, lo


def _moba_constants(s):
    nblk = s // MOBA_BLOCK
    n_arow = K_FEAT - nblk
    pos = np.arange(s)
    kfeat = np.zeros((s, K_FEAT), np.float32)
    kfeat[pos, pos // MOBA_BLOCK] = 1.0
    kfeat[:, nblk:nblk + 3] = 1.0
    kfeat[:, nblk + 3:nblk + 6] = (pos % MOBA_BLOCK)[:, None]
    slopes = jnp.asarray(_alibi_slopes(MOBA_HEADS), f32)
    off = jnp.arange(MOBA_BLOCK, dtype=f32)
    qterm = _split3(-slopes[:, None] * off[None, :])
    sterm = _split3(jnp.broadcast_to(slopes[:, None], (MOBA_HEADS, MOBA_BLOCK)))
    rows = jnp.stack(list(qterm) + list(sterm), axis=1)
    rows = jnp.concatenate(
        [rows, jnp.zeros((MOBA_HEADS, n_arow - 6, MOBA_BLOCK), bf16)], axis=1)
    arows = rows.reshape(MOBA_HEADS // 2, 2, n_arow, MOBA_BLOCK)
    return jnp.asarray(kfeat, bf16), arows, slopes


def _dsa_bias(group, dilation):
    slopes = _alibi_slopes(DSA_HEADS)[group * DSA_HEADS_PER_GROUP:(group + 1) * DSA_HEADS_PER_GROUP]
    qi = np.arange(DSA_BLK)[:, None]
    kj = np.arange(2 * DSA_BLK)[None, :]
    delta = (qi + DSA_BLK - kj) * dilation
    bias = -slopes[:, None, None].astype(np.float32) * delta.astype(np.float32)[None]
    return jnp.asarray(bias.reshape(2, 2, DSA_BLK, 2 * DSA_BLK), f32)


def _layer(x, norm_ffn1, ffn1_gate, ffn1_up, ffn1_down, norm_mix, w_in, w_up_a, w_up_b, w_out,
           norm_ffn2, ffn2_gate, ffn2_up, ffn2_down, gain_final, *, final_norm):
    b, s, _ = x.shape
    tm = 256
    row = lambda g: g.reshape(1, D_MODEL).astype(f32)
    x1 = _ffn(x.reshape(b * s, D_MODEL), row(norm_ffn1), ffn1_gate.astype(bf16), ffn1_up.astype(bf16),
              ffn1_down.astype(bf16), row(gain_final), final_norm=False, tm=tm).reshape(b, s, D_MODEL)

    scale = HEAD_DIM ** -0.5
    o = 0
    parts = []
    for width in (DSA_WIDTH, DSA_WIDTH, DSA_WIDTH, MOBA_WIDTH, MOBA_WIDTH, MOBA_WIDTH, 2 * D_MODEL):
        parts.append(w_in[:, o:o + width])
        o += width
    wqa, wka, wva, wqb, wkb, wvb, wgate = parts
    wa = jnp.concatenate([wqa * scale, wka, wva], axis=1).astype(bf16)
    wqvt = jnp.concatenate([wqb * scale, wvb], axis=1).T.astype(bf16)
    kfeat, arows, slopes_b = _moba_constants(s)
    vpad = jnp.zeros((V_ROWS - HEAD_DIM, tm), bf16).at[0].set(1.0)
    qkva, kaug, qt, vaug, ksum, gates = _proj(
        x1, row(norm_mix), wa, wkb.astype(bf16), wqvt, wgate.astype(bf16), kfeat, vpad, tm=tm)

    dsa_outs = [
        _dsa_group(qkva, _dsa_bias(gi, dil), group=gi, dilation=dil)
        for gi, (_, dil) in enumerate(DSA_GROUPS)
    ]
    ybt = _moba(slopes_b, qt, kaug, vaug, ksum, arows)
    x2 = _merge(x1, gates, dsa_outs, ybt, w_up_a.astype(bf16), w_up_b.astype(bf16),
                w_out.astype(bf16), tm=tm)
    x3 = _ffn(x2.reshape(b * s, D_MODEL), row(norm_ffn2), ffn2_gate.astype(bf16), ffn2_up.astype(bf16),
              ffn2_down.astype(bf16), row(gain_final), final_norm=final_norm, tm=tm)
    return x3.reshape(b, s, D_MODEL)


def kernel(x, norm_ffn1, ffn1_gate, ffn1_up, ffn1_down, norm_mix, w_in, w_up_a, w_up_b, w_out,
           norm_ffn2, ffn2_gate, ffn2_up, ffn2_down, norm_final):
    depth = norm_ffn1.shape[0]
    for layer in range(depth):
        last = layer == depth - 1
        x = _layer(x, norm_ffn1[layer], ffn1_gate[layer], ffn1_up[layer], ffn1_down[layer],
                   norm_mix[layer], w_in[layer], w_up_a[layer], w_up_b[layer], w_out[layer],
                   norm_ffn2[layer], ffn2_gate[layer], ffn2_up[layer], ffn2_down[layer],
                   norm_final, final_norm=last)
    return x
```

```python
import functools

import numpy as np
import jax
import jax.numpy as jnp
from jax import lax
from jax.experimental import pallas as pl
from jax.experimental.pallas import tpu as pltpu

D_MODEL = 1024
HEAD_DIM = 64
DSA_GROUPS = ((128, 1), (512, 4), (2048, 16))
DSA_HEADS_PER_GROUP = 4
DSA_HEADS = DSA_HEADS_PER_GROUP * len(DSA_GROUPS)
MOBA_HEADS = 8
MOBA_BLOCK = 256
MOBA_TOPK = 3
D_FF = ((8 * D_MODEL // 3 + 127) // 128) * 128
DSA_WIDTH = DSA_HEADS * HEAD_DIM
MOBA_WIDTH = MOBA_HEADS * HEAD_DIM
DSA_OUT = DSA_HEADS_PER_GROUP * HEAD_DIM
EPS = 1e-6
NEG = -1e30

LANES = 128
HEAD_PAIR = 2 * HEAD_DIM
DSA_BLK = 128
MOBA_TILE = 2 * MOBA_BLOCK
MAX_SLABS = 8
LOG2E = 1.4426950408889634
M_INIT = -3.0e38
V_ROWS = 80
K_FEAT = 128
VMEM_LIMIT = 56 * 1024 * 1024

f32 = jnp.float32
bf16 = jnp.bfloat16


def _rmsnorm(x, g):
    return x * lax.rsqrt(jnp.mean(x * x, axis=-1, keepdims=True) + EPS) * g


def _dot(a, b):
    return jnp.dot(a, b, preferred_element_type=f32)


def _dot_nt(a, b):
    return lax.dot_general(a, b, (((1,), (1,)), ((), ())), preferred_element_type=f32)


def _dot_tn(a, b):
    return lax.dot_general(a, b, (((0,), (0,)), ((), ())), preferred_element_type=f32)


def _resident(shape):
    nd = len(shape)
    return pl.BlockSpec(shape, lambda *_: (0,) * nd, pipeline_mode=pl.Buffered(1))


def _ffn_kernel(x_ref, g_ref, wg_ref, wu_ref, wd_ref, gf_ref, o_ref, *, final_norm):
    x = x_ref[...]
    h = _rmsnorm(x, g_ref[...]).astype(bf16)
    gate = _dot(h, wg_ref[...])
    up = _dot(h, wu_ref[...])
    act = (gate * jax.nn.sigmoid(gate) * up).astype(bf16)
    y = x + 0.5 * _dot(act, wd_ref[...])
    if final_norm:
        y = _rmsnorm(y, gf_ref[...])
    o_ref[...] = y


def _ffn(x2d, gain, w_gate, w_up, w_down, gain_final, *, final_norm, tm):
    n = x2d.shape[0]
    return pl.pallas_call(
        functools.partial(_ffn_kernel, final_norm=final_norm),
        out_shape=jax.ShapeDtypeStruct((n, D_MODEL), f32),
        grid=(n // tm,),
        in_specs=[
            pl.BlockSpec((tm, D_MODEL), lambda i: (i, 0)),
            _resident((1, D_MODEL)),
            _resident((D_MODEL, D_FF)),
            _resident((D_MODEL, D_FF)),
            _resident((D_FF, D_MODEL)),
            _resident((1, D_MODEL)),
        ],
        out_specs=pl.BlockSpec((tm, D_MODEL), lambda i: (i, 0)),
        compiler_params=pltpu.CompilerParams(
            dimension_semantics=("parallel",), vmem_limit_bytes=VMEM_LIMIT),
        name="ffn_final" if final_norm else "ffn",
    )(x2d, gain, w_gate, w_up, w_down, gain_final)


def _proj_kernel(x_ref, g_ref, wa_ref, wkb_ref, wqvt_ref, wgate_ref, kfeat_ref, vpad_ref,
                 qkva_ref, kaug_ref, qt_ref, vaug_ref, ksum_ref, gates_ref, *, tm):
    h = _rmsnorm(x_ref[0], g_ref[...]).astype(bf16)
    qkva_ref[0] = _dot(h, wa_ref[...]).astype(bf16)
    kb = _dot(h, wkb_ref[...])
    ksum_ref[...] = jnp.sum(kb.reshape(tm // MOBA_BLOCK, MOBA_BLOCK, MOBA_WIDTH), axis=1)[:, None, :]
    kb = kb.astype(bf16)
    for p in range(MOBA_HEADS // 2):
        kaug_ref[0, p, :, 0:HEAD_PAIR] = kb[:, p * HEAD_PAIR:(p + 1) * HEAD_PAIR]
        kaug_ref[0, p, :, HEAD_PAIR:HEAD_PAIR + K_FEAT] = kfeat_ref[...]
    qv = _dot_nt(wqvt_ref[...], h)
    qt_ref[0] = qv[0:MOBA_WIDTH].astype(bf16)
    for hd in range(MOBA_HEADS):
        lo = MOBA_WIDTH + hd * HEAD_DIM
        vaug_ref[0, hd, 0:HEAD_DIM, :] = qv[lo:lo + HEAD_DIM].astype(bf16)
        vaug_ref[0, hd, HEAD_DIM:V_ROWS, :] = vpad_ref[...]
    gates_ref[0] = jax.nn.sigmoid(_dot(h, wgate_ref[...])).astype(bf16)


def _proj(x3d, gain, wa, wkb, wqvt, wgate, kfeat, vpad, *, tm):
    b, s, _ = x3d.shape
    nblk = s // MOBA_BLOCK
    out_shape = (
        jax.ShapeDtypeStruct((b, s, 3 * DSA_WIDTH), bf16),
        jax.ShapeDtypeStruct((b, MOBA_HEADS // 2, s, HEAD_PAIR + K_FEAT), bf16),
        jax.ShapeDtypeStruct((b, MOBA_WIDTH, s), bf16),
        jax.ShapeDtypeStruct((b, MOBA_HEADS, V_ROWS, s), bf16),
        jax.ShapeDtypeStruct((b * nblk, 1, MOBA_WIDTH), f32),
        jax.ShapeDtypeStruct((b, s, 2 * D_MODEL), bf16),
    )
    per = tm // MOBA_BLOCK
    return pl.pallas_call(
        functools.partial(_proj_kernel, tm=tm),
        out_shape=out_shape,
        grid=(b, s // tm),
        in_specs=[
            pl.BlockSpec((1, tm, D_MODEL), lambda bi, i: (bi, i, 0)),
            _resident((1, D_MODEL)),
            _resident((D_MODEL, 3 * DSA_WIDTH)),
            _resident((D_MODEL, MOBA_WIDTH)),
            _resident((2 * MOBA_WIDTH, D_MODEL)),
            _resident((D_MODEL, 2 * D_MODEL)),
            pl.BlockSpec((tm, K_FEAT), lambda bi, i: (i, 0)),
            _resident((V_ROWS - HEAD_DIM, tm)),
        ],
        out_specs=(
            pl.BlockSpec((1, tm, 3 * DSA_WIDTH), lambda bi, i: (bi, i, 0)),
            pl.BlockSpec((1, MOBA_HEADS // 2, tm, HEAD_PAIR + K_FEAT), lambda bi, i: (bi, 0, i, 0)),
            pl.BlockSpec((1, MOBA_WIDTH, tm), lambda bi, i: (bi, 0, i)),
            pl.BlockSpec((1, MOBA_HEADS, V_ROWS, tm), lambda bi, i: (bi, 0, 0, i)),
            pl.BlockSpec((per, 1, MOBA_WIDTH), lambda bi, i: (bi * (s // tm) + i, 0, 0)),
            pl.BlockSpec((1, tm, 2 * D_MODEL), lambda bi, i: (bi, i, 0)),
        ),
        compiler_params=pltpu.CompilerParams(
            dimension_semantics=("parallel", "parallel"), vmem_limit_bytes=VMEM_LIMIT),
        name="proj",
    )(x3d, gain, wa, wkb, wqvt, wgate, kfeat, vpad)


def _dsa_kernel(q_ref, k_ref, v_ref, bias_ref, o_ref, *, nblk):
    blk = DSA_BLK
    lane = lax.broadcasted_iota(jnp.int32, (blk, LANES), 1)
    qi = lax.broadcasted_iota(jnp.int32, (blk, 2 * blk), 0)
    kj = lax.broadcasted_iota(jnp.int32, (blk, 2 * blk), 1)
    delta = qi + blk - kj
    valid = (delta >= 0) & (delta <= blk)

    def attend(row0, q, kband, vband, ok, bias_of):
        outs, lses = [], []
        for h in range(2):
            mine = (lane >= HEAD_DIM) if h else (lane < HEAD_DIM)
            qm = jnp.where(mine, q, jnp.zeros_like(q))
            s = _dot_nt(qm, kband)
            s = jnp.where(ok, s + bias_of(h), NEG)
            m = jnp.max(s, axis=-1, keepdims=True)
            p = jnp.exp(s - m)
            l = jnp.sum(p, axis=-1, keepdims=True)
            o = _dot(p.astype(bf16), vband)
            outs.append(o / l)
            lses.append(jnp.broadcast_to(m + jnp.log(l), (blk, LANES)))
        first = lane < HEAD_DIM
        o_ref[0, pl.ds(row0, blk), 0:LANES] = jnp.where(first, outs[0], outs[1])
        o_ref[0, pl.ds(row0, blk), LANES:2 * LANES] = jnp.where(first, lses[0], lses[1])

    attend(0, q_ref[0, 0:blk, :], k_ref[0, 0:blk, :], v_ref[0, 0:blk, :], valid[:, blk:],
           lambda h: bias_ref[0, h, :, blk:])

    def body(i, carry):
        row0 = pl.multiple_of(i * blk, blk)
        band0 = pl.multiple_of((i - 1) * blk, blk)
        attend(row0, q_ref[0, pl.ds(row0, blk), :], k_ref[0, pl.ds(band0, 2 * blk), :],
               v_ref[0, pl.ds(band0, 2 * blk), :], valid, lambda h: bias_ref[0, h])
        return carry

    lax.fori_loop(1, nblk, body, 0)


def _dsa_group(qkva, bias, *, group, dilation):
    b, s, _ = qkva.shape
    length = s // dilation
    width = 3 * DSA_WIDTH
    view = qkva.reshape(b, length, dilation * width)
    cols = width // LANES
    qcol = group * 2
    kcol = DSA_WIDTH // LANES + group * 2
    vcol = 2 * DSA_WIDTH // LANES + group * 2
    out = pl.pallas_call(
        functools.partial(_dsa_kernel, nblk=length // DSA_BLK),
        out_shape=jax.ShapeDtypeStruct((b, length, dilation * 4 * LANES), f32),
        grid=(b, dilation, 2),
        in_specs=[
            pl.BlockSpec((1, length, LANES), lambda bi, r, hp: (bi, 0, r * cols + qcol + hp)),
            pl.BlockSpec((1, length, LANES), lambda bi, r, hp: (bi, 0, r * cols + kcol + hp)),
            pl.BlockSpec((1, length, LANES), lambda bi, r, hp: (bi, 0, r * cols + vcol + hp)),
            pl.BlockSpec((1, 2, DSA_BLK, 2 * DSA_BLK), lambda bi, r, hp: (hp, 0, 0, 0)),
        ],
        out_specs=pl.BlockSpec((1, length, 2 * LANES), lambda bi, r, hp: (bi, 0, r * 2 + hp)),
        compiler_params=pltpu.CompilerParams(
            dimension_semantics=("parallel", "parallel", "parallel"), vmem_limit_bytes=VMEM_LIMIT),
        name=f"dsa_g{group}",
    )(view, view, view, bias)
    return out.reshape(b, s, 4 * LANES)


def _moba_kernel(slope_ref, qt_ref, kaug_ref, vaug_ref, ksum_ref, arow_ref, o_ref,
                 rhs_ref, sa_ref, sb_ref, acc_ref, m_ref, *, nblk):
    hp = pl.program_id(1)
    n = pl.program_id(2)
    blk = MOBA_BLOCK
    tile = MOBA_TILE
    ntile = (n + 2) // 2
    qpair = qt_ref[0]
    rowi = lax.broadcasted_iota(jnp.int32, (HEAD_PAIR, blk), 0)
    bidx = lax.broadcasted_iota(jnp.int32, (nblk, blk), 0)
    kmean = ksum_ref[0] * (1.0 / blk)
    kmean_hi = kmean.astype(bf16)
    kmean_lo = (kmean - kmean_hi.astype(f32)).astype(bf16)
    past = bidx < n

    for h in range(2):
        mine = (rowi >= HEAD_DIM) if h else (rowi < HEAD_DIM)
        qpad = jnp.where(mine, qpair, jnp.zeros_like(qpair))
        gate = _dot(kmean_hi, qpad) + _dot(kmean_lo, qpad)
        g = jnp.where(past, gate, NEG)
        sel = bidx == n
        for _ in range(MOBA_TOPK):
            mx = jnp.max(g, axis=0, keepdims=True)
            idx = jnp.min(jnp.where(g == mx, bidx, nblk), axis=0, keepdims=True)
            pick = bidx == idx
            sel = sel | (pick & past)
            g = jnp.where(pick, -jnp.inf, g)
        rhs_ref[h, 0:HEAD_PAIR, :] = qpad
        rhs_ref[h, HEAD_PAIR:HEAD_PAIR + nblk, :] = jnp.where(sel, 0.0, NEG).astype(bf16)
        rhs_ref[h, HEAD_PAIR + nblk:HEAD_PAIR + K_FEAT, :] = arow_ref[0, h]
        acc_ref[h] = jnp.zeros((V_ROWS, blk), f32)
        m_ref[h] = jnp.full((1, blk), M_INIT, f32)

    def scores(i, s_ref):
        k0 = pl.multiple_of(i * tile, tile)
        kt = kaug_ref[0, 0, pl.ds(k0, tile), :]
        for h in range(2):
            s_ref[h] = _dot(kt, rhs_ref[h])

    def update(i, s_ref, last):
        k0 = pl.multiple_of(i * tile, tile)
        rel = tile * i - blk * n
        for h in range(2):
            s = s_ref[h]
            if last:
                krow = lax.broadcasted_iota(jnp.int32, (tile, blk), 0)
                qcol = lax.broadcasted_iota(jnp.int32, (tile, blk), 1)
                s = jnp.where(krow + rel <= qcol, s, NEG)
            shift = slope_ref[2 * hp + h] * rel.astype(f32)
            m_old = m_ref[h]
            smax = jnp.max(s.reshape(MAX_SLABS, tile // MAX_SLABS, blk), axis=0)
            m_new = jnp.maximum(m_old, jnp.max(smax, axis=0, keepdims=True) + shift)
            p = jnp.exp2(s - (m_new - shift)).astype(bf16)
            pv = _dot(vaug_ref[0, h, :, pl.ds(k0, tile)], p)
            acc_ref[h] = jnp.exp2(m_old - m_new) * acc_ref[h] + pv
            m_ref[h] = m_new

    scores(0, sa_ref)
    npair = (ntile - 1) // 2

    def body(i2, carry):
        t = 2 * i2
        scores(t + 1, sb_ref)
        update(t, sa_ref, False)
        scores(t + 2, sa_ref)
        update(t + 1, sb_ref, False)
        return carry

    lax.fori_loop(0, npair, body, 0)
    t_end = 2 * npair

    @pl.when(ntile - t_end == 1)
    def _():
        update(t_end, sa_ref, True)

    @pl.when(ntile - t_end == 2)
    def _():
        scores(t_end + 1, sb_ref)
        update(t_end, sa_ref, False)
        update(t_end + 1, sb_ref, True)

    for h in range(2):
        acc = acc_ref[h]
        o_ref[0, h * HEAD_DIM:(h + 1) * HEAD_DIM, :] = (
            acc[0:HEAD_DIM] / acc[HEAD_DIM:HEAD_DIM + 1]).astype(bf16)


def _moba(slopes, qt, kaug, vaug, ksum, arows):
    b, _, s = qt.shape
    nblk = s // MOBA_BLOCK
    assert nblk + arows.shape[2] == K_FEAT
    pairs = MOBA_HEADS // 2
    return pl.pallas_call(
        functools.partial(_moba_kernel, nblk=nblk),
        out_shape=jax.ShapeDtypeStruct((b, MOBA_WIDTH, s), bf16),
        grid=(b, pairs, nblk),
        in_specs=[
            pl.BlockSpec(memory_space=pltpu.SMEM),
            pl.BlockSpec((1, HEAD_PAIR, MOBA_BLOCK), lambda bi, hp, n: (bi, hp, n)),
            pl.BlockSpec((1, 1, s, HEAD_PAIR + K_FEAT), lambda bi, hp, n: (bi, hp, 0, 0)),
            pl.BlockSpec((1, 2, V_ROWS, s), lambda bi, hp, n: (bi, hp, 0, 0)),
            pl.BlockSpec((1, nblk, HEAD_PAIR), lambda bi, hp, n: (bi, 0, hp)),
            pl.BlockSpec((1, 2, K_FEAT - nblk, MOBA_BLOCK), lambda bi, hp, n: (hp, 0, 0, 0)),
        ],
        out_specs=pl.BlockSpec((1, HEAD_PAIR, MOBA_BLOCK), lambda bi, hp, n: (bi, hp, n)),
        scratch_shapes=[
            pltpu.VMEM((2, HEAD_PAIR + K_FEAT, MOBA_BLOCK), bf16),
            pltpu.VMEM((2, MOBA_TILE, MOBA_BLOCK), f32),
            pltpu.VMEM((2, MOBA_TILE, MOBA_BLOCK), f32),
            pltpu.VMEM((2, V_ROWS, MOBA_BLOCK), f32),
            pltpu.VMEM((2, 1, MOBA_BLOCK), f32),
        ],
        compiler_params=pltpu.CompilerParams(
            dimension_semantics=("parallel", "parallel", "arbitrary"), vmem_limit_bytes=VMEM_LIMIT),
        name="moba",
    )(slopes, qt, kaug, vaug, ksum.reshape(b, nblk, MOBA_WIDTH), arows)


def _merge_kernel(x_ref, gates_ref, d0_ref, d1_ref, d2_ref, ybt_ref, wua_ref, wub_ref, wo_ref, o_ref):
    groups = (d0_ref[0], d1_ref[0], d2_ref[0])
    pairs = []
    for hp in range(2):
        outs = [d[:, hp * 2 * LANES:hp * 2 * LANES + LANES] for d in groups]
        lses = [d[:, hp * 2 * LANES + LANES:(hp + 1) * 2 * LANES] for d in groups]
        mx = jnp.maximum(jnp.maximum(lses[0], lses[1]), lses[2])
        es = [jnp.exp(l - mx) for l in lses]
        den = es[0] + es[1] + es[2]
        pairs.append((es[0] * outs[0] + es[1] * outs[1] + es[2] * outs[2]) / den)
    ya = jnp.concatenate(pairs, axis=1).astype(bf16)
    ta = _dot(ya, wua_ref[...])
    tb = _dot_tn(ybt_ref[0], wub_ref[...])
    gates = gates_ref[0]
    merged = gates[:, 0:D_MODEL].astype(f32) * ta + gates[:, D_MODEL:].astype(f32) * tb
    o_ref[0] = x_ref[0] + _dot(merged.astype(bf16), wo_ref[...])


def _merge(x3d, gates, dsa_outs, ybt, wua, wub, wo, *, tm):
    b, s, _ = x3d.shape
    tok = lambda w: pl.BlockSpec((1, tm, w), lambda bi, i: (bi, i, 0))
    return pl.pallas_call(
        _merge_kernel,
        out_shape=jax.ShapeDtypeStruct((b, s, D_MODEL), f32),
        grid=(b, s // tm),
        in_specs=[
            tok(D_MODEL), tok(2 * D_MODEL), tok(4 * LANES), tok(4 * LANES), tok(4 * LANES),
            pl.BlockSpec((1, MOBA_WIDTH, tm), lambda bi, i: (bi, 0, i)),
            _resident((DSA_OUT, D_MODEL)),
            _resident((MOBA_WIDTH, D_MODEL)),
            _resident((D_MODEL, D_MODEL)),
        ],
        out_specs=tok(D_MODEL),
        compiler_params=pltpu.CompilerParams(
            dimension_semantics=("parallel", "parallel"), vmem_limit_bytes=VMEM_LIMIT),
        name="merge",
    )(x3d, gates, *dsa_outs, ybt, wua, wub, wo)


def _alibi_slopes(n):
    return 2.0 ** (-8.0 * np.arange(1, n + 1, dtype=np.float64) / n)


def _split3(x):
    hi = x.astype(bf16)
    r1 = x - hi.astype(f32)
    mid = r1.astype(bf16)
    lo = (r1 - mid.astype(f32)).astype(bf16)
    return hi, mid, lo


def _moba_constants(s):
    nblk = s // MOBA_BLOCK
    n_arow = K_FEAT - nblk
    pos = np.arange(s)
    kfeat = np.zeros((s, K_FEAT), np.float32)
    kfeat[pos, pos // MOBA_BLOCK] = 1.0
    kfeat[:, nblk:nblk + 3] = 1.0
    kfeat[:, nblk + 3:nblk + 6] = (pos % MOBA_BLOCK)[:, None]
    kfeat[:, nblk + 6:nblk + 9] = ((pos // MOBA_BLOCK) % 2)[:, None]
    slopes = jnp.asarray(_alibi_slopes(MOBA_HEADS) * LOG2E, f32)
    off = jnp.arange(MOBA_BLOCK, dtype=f32)
    wide = lambda v: jnp.broadcast_to(v[:, None], (MOBA_HEADS, MOBA_BLOCK))
    qterm = _split3(-slopes[:, None] * off[None, :])
    sterm = _split3(wide(slopes))
    bterm = _split3(wide(slopes * MOBA_BLOCK))
    rows = jnp.stack(list(qterm) + list(sterm) + list(bterm), axis=1)
    rows = jnp.concatenate(
        [rows, jnp.zeros((MOBA_HEADS, n_arow - 9, MOBA_BLOCK), bf16)], axis=1)
    arows = rows.reshape(MOBA_HEADS // 2, 2, n_arow, MOBA_BLOCK)
    return jnp.asarray(kfeat, bf16), arows, slopes


def _dsa_bias(group, dilation):
    slopes = _alibi_slopes(DSA_HEADS)[group * DSA_HEADS_PER_GROUP:(group + 1) * DSA_HEADS_PER_GROUP]
    qi = np.arange(DSA_BLK)[:, None]
    kj = np.arange(2 * DSA_BLK)[None, :]
    delta = (qi + DSA_BLK - kj) * dilation
    bias = -slopes[:, None, None].astype(np.float32) * delta.astype(np.float32)[None]
    return jnp.asarray(bias.reshape(2, 2, DSA_BLK, 2 * DSA_BLK), f32)


def _layer(x, norm_ffn1, ffn1_gate, ffn1_up, ffn1_down, norm_mix, w_in, w_up_a, w_up_b, w_out,
           norm_ffn2, ffn2_gate, ffn2_up, ffn2_down, gain_final, *, final_norm):
    b, s, _ = x.shape
    tm = 256
    row = lambda g: g.reshape(1, D_MODEL).astype(f32)
    x1 = _ffn(x.reshape(b * s, D_MODEL), row(norm_ffn1), ffn1_gate.astype(bf16), ffn1_up.astype(bf16),
              ffn1_down.astype(bf16), row(gain_final), final_norm=False, tm=tm).reshape(b, s, D_MODEL)

    scale = HEAD_DIM ** -0.5
    o = 0
    parts = []
    for width in (DSA_WIDTH, DSA_WIDTH, DSA_WIDTH, MOBA_WIDTH, MOBA_WIDTH, MOBA_WIDTH, 2 * D_MODEL):
        parts.append(w_in[:, o:o + width])
        o += width
    wqa, wka, wva, wqb, wkb, wvb, wgate = parts
    wa = jnp.concatenate([wqa * scale, wka, wva], axis=1).astype(bf16)
    wqvt = jnp.concatenate([wqb * (scale * LOG2E), wvb], axis=1).T.astype(bf16)
    kfeat, arows, slopes_b = _moba_constants(s)
    vpad = jnp.zeros((V_ROWS - HEAD_DIM, tm), bf16).at[0].set(1.0)
    qkva, kaug, qt, vaug, ksum, gates = _proj(
        x1, row(norm_mix), wa, wkb.astype(bf16), wqvt, wgate.astype(bf16), kfeat, vpad, tm=tm)

    dsa_outs = [
        _dsa_group(qkva, _dsa_bias(gi, dil), group=gi, dilation=dil)
        for gi, (_, dil) in enumerate(DSA_GROUPS)
    ]
    ybt = _moba(slopes_b, qt, kaug, vaug, ksum, arows)
    x2 = _merge(x1, gates, dsa_outs, ybt, w_up_a.astype(bf16), w_up_b.astype(bf16),
                w_out.astype(bf16), tm=tm)
    x3 = _ffn(x2.reshape(b * s, D_MODEL), row(norm_ffn2), ffn2_gate.astype(bf16), ffn2_up.astype(bf16),
              ffn2_down.astype(bf16), row(gain_final), final_norm=final_norm, tm=tm)
    return x3.reshape(b, s, D_MODEL)


def kernel(x, norm_ffn1, ffn1_gate, ffn1_up, ffn1_down, norm_mix, w_in, w_up_a, w_up_b, w_out,
           norm_ffn2, ffn2_gate, ffn2_up, ffn2_down, norm_final):
    depth = norm_ffn1.shape[0]
    for layer in range(depth):
        last = layer == depth - 1
        x = _layer(x, norm_ffn1[layer], ffn1_gate[layer], ffn1_up[layer], ffn1_down[layer],
                   norm_mix[layer], w_in[layer], w_up_a[layer], w_up_b[layer], w_out[layer],
                   norm_ffn2[layer], ffn2_gate[layer], ffn2_up[layer], ffn2_down[layer],
                   norm_final, final_norm=last)
    return x
```

```python
import functools

import numpy as np
import jax
import jax.numpy as jnp
from jax import lax
from jax.experimental import pallas as pl
from jax.experimental.pallas import tpu as pltpu

D_MODEL = 1024
HEAD_DIM = 64
DSA_GROUPS = ((128, 1), (512, 4), (2048, 16))
DSA_HEADS_PER_GROUP = 4
DSA_HEADS = DSA_HEADS_PER_GROUP * len(DSA_GROUPS)
MOBA_HEADS = 8
MOBA_BLOCK = 256
MOBA_TOPK = 3
D_FF = ((8 * D_MODEL // 3 + 127) // 128) * 128
DSA_WIDTH = DSA_HEADS * HEAD_DIM
MOBA_WIDTH = MOBA_HEADS * HEAD_DIM
DSA_OUT = DSA_HEADS_PER_GROUP * HEAD_DIM
EPS = 1e-6
NEG = -1e30

LANES = 128
HEAD_PAIR = 2 * HEAD_DIM
DSA_BLK = 128
DSA_TQ = 256
DSA_BAND = DSA_TQ + DSA_BLK
DSA_GROUP_COLS = 3 * DSA_OUT
MOBA_TILE = 2 * MOBA_BLOCK
MAX_SLABS = 8
LOG2E = 1.4426950408889634
M_INIT = -3.0e38
V_ROWS = 80
K_FEAT = 128
VMEM_LIMIT = 56 * 1024 * 1024

f32 = jnp.float32
bf16 = jnp.bfloat16


def _rmsnorm(x, g):
    return x * lax.rsqrt(jnp.mean(x * x, axis=-1, keepdims=True) + EPS) * g


def _dot(a, b):
    return jnp.dot(a, b, preferred_element_type=f32)


def _dot_nt(a, b):
    return lax.dot_general(a, b, (((1,), (1,)), ((), ())), preferred_element_type=f32)


def _dot_tn(a, b):
    return lax.dot_general(a, b, (((0,), (0,)), ((), ())), preferred_element_type=f32)


def _resident(shape):
    nd = len(shape)
    return pl.BlockSpec(shape, lambda *_: (0,) * nd, pipeline_mode=pl.Buffered(1))


def _ffn_kernel(x_ref, g_ref, wg_ref, wu_ref, wd_ref, gf_ref, o_ref, *, final_norm):
    x = x_ref[...]
    h = _rmsnorm(x, g_ref[...]).astype(bf16)
    gate = _dot(h, wg_ref[...])
    up = _dot(h, wu_ref[...])
    act = (gate * jax.nn.sigmoid(gate) * up).astype(bf16)
    y = x + 0.5 * _dot(act, wd_ref[...])
    if final_norm:
        y = _rmsnorm(y, gf_ref[...])
    o_ref[...] = y


def _ffn(x2d, gain, w_gate, w_up, w_down, gain_final, *, final_norm, tm):
    n = x2d.shape[0]
    return pl.pallas_call(
        functools.partial(_ffn_kernel, final_norm=final_norm),
        out_shape=jax.ShapeDtypeStruct((n, D_MODEL), f32),
        grid=(n // tm,),
        in_specs=[
            pl.BlockSpec((tm, D_MODEL), lambda i: (i, 0)),
            _resident((1, D_MODEL)),
            _resident((D_MODEL, D_FF)),
            _resident((D_MODEL, D_FF)),
            _resident((D_FF, D_MODEL)),
            _resident((1, D_MODEL)),
        ],
        out_specs=pl.BlockSpec((tm, D_MODEL), lambda i: (i, 0)),
        compiler_params=pltpu.CompilerParams(
            dimension_semantics=("parallel",), vmem_limit_bytes=VMEM_LIMIT),
        name="ffn_final" if final_norm else "ffn",
    )(x2d, gain, w_gate, w_up, w_down, gain_final)


def _proj_kernel(x_ref, g_ref, wa_ref, wkb_ref, wqvt_ref, wgate_ref, kfeat_ref, vpad_ref,
                 a0_ref, a1_ref, a2_ref, kaug_ref, qt_ref, vaug_ref, ksum_ref, gates_ref, scr_ref, *, tm):
    h = _rmsnorm(x_ref[0], g_ref[...]).astype(bf16)
    qkv = _dot(h, wa_ref[...])
    tiles_per_group = DSA_GROUP_COLS // LANES
    for g, (a_ref, (_, dil)) in enumerate(zip((a0_ref, a1_ref, a2_ref), DSA_GROUPS)):
        for c in range(tiles_per_group):
            lo = g * DSA_GROUP_COLS + c * LANES
            if dil == 1:
                a_ref[0, 0, :, c * LANES:(c + 1) * LANES] = qkv[:, lo:lo + LANES].astype(bf16)
                continue
            scr_ref[g, c] = qkv[:, lo:lo + LANES]
            for r in range(dil):
                a_ref[0, r, :, c * LANES:(c + 1) * LANES] = (
                    scr_ref[g, c, pl.ds(r, tm // dil, stride=dil), :].astype(bf16))
    kb = _dot(h, wkb_ref[...])
    ksum_ref[...] = jnp.sum(kb.reshape(tm // MOBA_BLOCK, MOBA_BLOCK, MOBA_WIDTH), axis=1)[:, None, :]
    kb = kb.astype(bf16)
    for p in range(MOBA_HEADS // 2):
        kaug_ref[0, p, :, 0:HEAD_PAIR] = kb[:, p * HEAD_PAIR:(p + 1) * HEAD_PAIR]
        kaug_ref[0, p, :, HEAD_PAIR:HEAD_PAIR + K_FEAT] = kfeat_ref[...]
    qv = _dot_nt(wqvt_ref[...], h)
    qt_ref[0] = qv[0:MOBA_WIDTH].astype(bf16)
    for hd in range(MOBA_HEADS):
        lo = MOBA_WIDTH + hd * HEAD_DIM
        vaug_ref[0, hd, 0:HEAD_DIM, :] = qv[lo:lo + HEAD_DIM].astype(bf16)
        vaug_ref[0, hd, HEAD_DIM:V_ROWS, :] = vpad_ref[...]
    gates_ref[0] = jax.nn.sigmoid(_dot(h, wgate_ref[...])).astype(bf16)


def _proj(x3d, gain, wa, wkb, wqvt, wgate, kfeat, vpad, *, tm):
    b, s, _ = x3d.shape
    nblk = s // MOBA_BLOCK
    grp_shapes = tuple(
        jax.ShapeDtypeStruct((b, dil, s // dil, DSA_GROUP_COLS), bf16) for _, dil in DSA_GROUPS)
    grp_specs = tuple(
        pl.BlockSpec((1, dil, tm // dil, DSA_GROUP_COLS), lambda bi, i: (bi, 0, i, 0))
        for _, dil in DSA_GROUPS)
    out_shape = grp_shapes + (
        jax.ShapeDtypeStruct((b, MOBA_HEADS // 2, s, HEAD_PAIR + K_FEAT), bf16),
        jax.ShapeDtypeStruct((b, MOBA_WIDTH, s), bf16),
        jax.ShapeDtypeStruct((b, MOBA_HEADS, V_ROWS, s), bf16),
        jax.ShapeDtypeStruct((b * nblk, 1, MOBA_WIDTH), f32),
        jax.ShapeDtypeStruct((b, s, 2 * D_MODEL), bf16),
    )
    per = tm // MOBA_BLOCK
    return pl.pallas_call(
        functools.partial(_proj_kernel, tm=tm),
        out_shape=out_shape,
        grid=(b, s // tm),
        in_specs=[
            pl.BlockSpec((1, tm, D_MODEL), lambda bi, i: (bi, i, 0)),
            _resident((1, D_MODEL)),
            _resident((D_MODEL, 3 * DSA_WIDTH)),
            _resident((D_MODEL, MOBA_WIDTH)),
            _resident((2 * MOBA_WIDTH, D_MODEL)),
            _resident((D_MODEL, 2 * D_MODEL)),
            pl.BlockSpec((tm, K_FEAT), lambda bi, i: (i, 0)),
            _resident((V_ROWS - HEAD_DIM, tm)),
        ],
        out_specs=grp_specs + (
            pl.BlockSpec((1, MOBA_HEADS // 2, tm, HEAD_PAIR + K_FEAT), lambda bi, i: (bi, 0, i, 0)),
            pl.BlockSpec((1, MOBA_WIDTH, tm), lambda bi, i: (bi, 0, i)),
            pl.BlockSpec((1, MOBA_HEADS, V_ROWS, tm), lambda bi, i: (bi, 0, 0, i)),
            pl.BlockSpec((per, 1, MOBA_WIDTH), lambda bi, i: (bi * (s // tm) + i, 0, 0)),
            pl.BlockSpec((1, tm, 2 * D_MODEL), lambda bi, i: (bi, i, 0)),
        ),
        scratch_shapes=[pltpu.VMEM((len(DSA_GROUPS), DSA_GROUP_COLS // LANES, tm, LANES), f32)],
        compiler_params=pltpu.CompilerParams(
            dimension_semantics=("parallel", "parallel"), vmem_limit_bytes=VMEM_LIMIT),
        name="proj",
    )(x3d, gain, wa, wkb, wqvt, wgate, kfeat, vpad)


def _dsa_kernel(q_ref, k_ref, v_ref, bias_ref, o_ref, sa_ref, sb_ref, *, ntile):
    tq, band = DSA_TQ, DSA_BAND
    first = lax.broadcasted_iota(jnp.int32, (tq, LANES), 1) < HEAD_DIM

    def rows_of(t):
        return pl.ds(pl.multiple_of(t * tq, tq), tq)

    def band_of(t):
        return pl.ds(pl.multiple_of(jnp.maximum(t * tq - DSA_BLK, 0), DSA_BLK), band)

    def scores(t, s_ref):
        q = q_ref[0, 0, rows_of(t), :]
        kband = k_ref[0, 0, band_of(t), :]
        for h in range(2):
            qm = jnp.where(first if h == 0 else jnp.logical_not(first), q, jnp.zeros_like(q))
            s_ref[h] = _dot_nt(qm, kband)

    def finish(t, s_ref):
        variant = jnp.minimum(t, 1)
        vband = v_ref[0, 0, band_of(t), :]
        outs, lses = [], []
        for h in range(2):
            s = s_ref[h] + bias_ref[0, h, variant]
            m = jnp.max(s, axis=-1, keepdims=True)
            p = jnp.exp2(s - m)
            l = jnp.sum(p, axis=-1, keepdims=True)
            o = _dot(p.astype(bf16), vband)
            outs.append(o / l)
            lses.append(jnp.broadcast_to(m + jnp.log(l) * LOG2E, (tq, LANES)))
        o_ref[0, 0, rows_of(t), 0:LANES] = jnp.where(first, outs[0], outs[1])
        o_ref[0, 0, rows_of(t), LANES:2 * LANES] = jnp.where(first, lses[0], lses[1])

    zero = jnp.int32(0)
    scores(zero, sa_ref)

    def body(i2, carry):
        t = 2 * i2
        scores(t + 1, sb_ref)
        finish(t, sa_ref)
        scores(t + 2, sa_ref)
        finish(t + 1, sb_ref)
        return carry

    lax.fori_loop(0, ntile // 2 - 1, body, 0)
    t_end = zero + (ntile - 2)
    scores(t_end + 1, sb_ref)
    finish(t_end, sa_ref)
    finish(t_end + 1, sb_ref)


def _dsa_group(a, bias, *, group):
    b, dil, length, _ = a.shape
    pairs = DSA_HEADS_PER_GROUP // 2
    assert length % (2 * DSA_TQ) == 0
    col = lambda part: (lambda bi, r, hp: (bi, r, 0, part * pairs + hp))
    return pl.pallas_call(
        functools.partial(_dsa_kernel, ntile=length // DSA_TQ),
        out_shape=jax.ShapeDtypeStruct((b, dil, length, 2 * pairs * LANES), f32),
        grid=(b, dil, pairs),
        in_specs=[
            pl.BlockSpec((1, 1, length, LANES), col(0)),
            pl.BlockSpec((1, 1, length, LANES), col(1)),
            pl.BlockSpec((1, 1, length, LANES), col(2)),
            pl.BlockSpec((1, 2, 2, DSA_TQ, DSA_BAND), lambda bi, r, hp: (hp, 0, 0, 0, 0)),
        ],
        out_specs=pl.BlockSpec((1, 1, length, 2 * LANES), lambda bi, r, hp: (bi, r, 0, hp)),
        scratch_shapes=[pltpu.VMEM((2, DSA_TQ, DSA_BAND), f32), pltpu.VMEM((2, DSA_TQ, DSA_BAND), f32)],
        compiler_params=pltpu.CompilerParams(
            dimension_semantics=("parallel", "parallel", "parallel"), vmem_limit_bytes=VMEM_LIMIT),
        name=f"dsa_g{group}",
    )(a, a, a, bias)


def _moba_kernel(slope_ref, qt_ref, kaug_ref, vaug_ref, ksum_ref, arow_ref, o_ref,
                 rhs_ref, sa_ref, sb_ref, acc_ref, m_ref, *, nblk):
    hp = pl.program_id(1)
    n = pl.program_id(2)
    blk = MOBA_BLOCK
    tile = MOBA_TILE
    ntile = (n + 2) // 2
    qpair = qt_ref[0]
    rowi = lax.broadcasted_iota(jnp.int32, (HEAD_PAIR, blk), 0)
    bidx = lax.broadcasted_iota(jnp.int32, (nblk, blk), 0)
    kmean = ksum_ref[0] * (1.0 / blk)
    kmean_hi = kmean.astype(bf16)
    kmean_lo = (kmean - kmean_hi.astype(f32)).astype(bf16)
    past = bidx < n

    for h in range(2):
        mine = (rowi >= HEAD_DIM) if h else (rowi < HEAD_DIM)
        qpad = jnp.where(mine, qpair, jnp.zeros_like(qpair))
        gate = _dot(kmean_hi, qpad) + _dot(kmean_lo, qpad)
        g = jnp.where(past, gate, NEG)
        sel = bidx == n
        for _ in range(MOBA_TOPK):
            mx = jnp.max(g, axis=0, keepdims=True)
            idx = jnp.min(jnp.where(g == mx, bidx, nblk), axis=0, keepdims=True)
            pick = bidx == idx
            sel = sel | (pick & past)
            g = jnp.where(pick, -jnp.inf, g)
        rhs_ref[h, 0:HEAD_PAIR, :] = qpad
        rhs_ref[h, HEAD_PAIR:HEAD_PAIR + nblk, :] = jnp.where(sel, 0.0, NEG).astype(bf16)
        rhs_ref[h, HEAD_PAIR + nblk:HEAD_PAIR + K_FEAT, :] = arow_ref[0, h]
        acc_ref[h] = jnp.zeros((V_ROWS, blk), f32)
        m_ref[h] = jnp.full((1, blk), M_INIT, f32)

    def scores(i, s_ref):
        k0 = pl.multiple_of(i * tile, tile)
        kt = kaug_ref[0, 0, pl.ds(k0, tile), :]
        for h in range(2):
            s_ref[h] = _dot(kt, rhs_ref[h])

    def update(i, s_ref, last):
        k0 = pl.multiple_of(i * tile, tile)
        rel = tile * i - blk * n
        for h in range(2):
            s = s_ref[h]
            if last:
                krow = lax.broadcasted_iota(jnp.int32, (tile, blk), 0)
                qcol = lax.broadcasted_iota(jnp.int32, (tile, blk), 1)
                s = jnp.where(krow + rel <= qcol, s, NEG)
            shift = slope_ref[2 * hp + h] * rel.astype(f32)
            m_old = m_ref[h]
            smax = jnp.max(s.reshape(MAX_SLABS, tile // MAX_SLABS, blk), axis=0)
            m_new = jnp.maximum(m_old, jnp.max(smax, axis=0, keepdims=True) + shift)
            p = jnp.exp2(s - (m_new - shift)).astype(bf16)
            pv = _dot(vaug_ref[0, h, :, pl.ds(k0, tile)], p)
            acc_ref[h] = jnp.exp2(m_old - m_new) * acc_ref[h] + pv
            m_ref[h] = m_new

    scores(0, sa_ref)
    npair = (ntile - 1) // 2

    def body(i2, carry):
        t = 2 * i2
        scores(t + 1, sb_ref)
        update(t, sa_ref, False)
        scores(t + 2, sa_ref)
        update(t + 1, sb_ref, False)
        return carry

    lax.fori_loop(0, npair, body, 0)
    t_end = 2 * npair

    @pl.when(ntile - t_end == 1)
    def _():
        update(t_end, sa_ref, True)

    @pl.when(ntile - t_end == 2)
    def _():
        scores(t_end + 1, sb_ref)
        update(t_end, sa_ref, False)
        update(t_end + 1, sb_ref, True)

    for h in range(2):
        acc = acc_ref[h]
        o_ref[0, h * HEAD_DIM:(h + 1) * HEAD_DIM, :] = (
            acc[0:HEAD_DIM] / acc[HEAD_DIM:HEAD_DIM + 1]).astype(bf16)


def _moba(slopes, qt, kaug, vaug, ksum, arows):
    b, _, s = qt.shape
    nblk = s // MOBA_BLOCK
    assert nblk + arows.shape[2] == K_FEAT
    pairs = MOBA_HEADS // 2
    return pl.pallas_call(
        functools.partial(_moba_kernel, nblk=nblk),
        out_shape=jax.ShapeDtypeStruct((b, MOBA_WIDTH, s), bf16),
        grid=(b, pairs, nblk),
        in_specs=[
            pl.BlockSpec(memory_space=pltpu.SMEM),
            pl.BlockSpec((1, HEAD_PAIR, MOBA_BLOCK), lambda bi, hp, n: (bi, hp, n)),
            pl.BlockSpec((1, 1, s, HEAD_PAIR + K_FEAT), lambda bi, hp, n: (bi, hp, 0, 0)),
            pl.BlockSpec((1, 2, V_ROWS, s), lambda bi, hp, n: (bi, hp, 0, 0)),
            pl.BlockSpec((1, nblk, HEAD_PAIR), lambda bi, hp, n: (bi, 0, hp)),
            pl.BlockSpec((1, 2, K_FEAT - nblk, MOBA_BLOCK), lambda bi, hp, n: (hp, 0, 0, 0)),
        ],
        out_specs=pl.BlockSpec((1, HEAD_PAIR, MOBA_BLOCK), lambda bi, hp, n: (bi, hp, n)),
        scratch_shapes=[
            pltpu.VMEM((2, HEAD_PAIR + K_FEAT, MOBA_BLOCK), bf16),
            pltpu.VMEM((2, MOBA_TILE, MOBA_BLOCK), f32),
            pltpu.VMEM((2, MOBA_TILE, MOBA_BLOCK), f32),
            pltpu.VMEM((2, V_ROWS, MOBA_BLOCK), f32),
            pltpu.VMEM((2, 1, MOBA_BLOCK), f32),
        ],
        compiler_params=pltpu.CompilerParams(
            dimension_semantics=("parallel", "parallel", "arbitrary"), vmem_limit_bytes=VMEM_LIMIT),
        name="moba",
    )(slopes, qt, kaug, vaug, ksum.reshape(b, nblk, MOBA_WIDTH), arows)


def _merge_kernel(x_ref, gates_ref, d0_ref, d1_ref, d2_ref, ybt_ref, wua_ref, wub_ref, wo_ref, o_ref,
                  nat_ref, *, tm):
    def token_order(g, c):
        d_ref, dil = (d0_ref, d1_ref, d2_ref)[g], DSA_GROUPS[g][1]
        if dil == 1:
            return d_ref[0, 0, :, c * LANES:(c + 1) * LANES]
        for r in range(dil):
            nat_ref[g, c, pl.ds(r, tm // dil, stride=dil), :] = d_ref[0, r, :, c * LANES:(c + 1) * LANES]
        return nat_ref[g, c]

    pairs = []
    for hp in range(2):
        outs = [token_order(g, 2 * hp) for g in range(len(DSA_GROUPS))]
        lses = [token_order(g, 2 * hp + 1) for g in range(len(DSA_GROUPS))]
        mx = jnp.maximum(jnp.maximum(lses[0], lses[1]), lses[2])
        es = [jnp.exp2(l - mx) for l in lses]
        den = es[0] + es[1] + es[2]
        pairs.append((es[0] * outs[0] + es[1] * outs[1] + es[2] * outs[2]) / den)
    ya = jnp.concatenate(pairs, axis=1).astype(bf16)
    ta = _dot(ya, wua_ref[...])
    tb = _dot_tn(ybt_ref[0], wub_ref[...])
    gates = gates_ref[0]
    merged = gates[:, 0:D_MODEL].astype(f32) * ta + gates[:, D_MODEL:].astype(f32) * tb
    o_ref[0] = x_ref[0] + _dot(merged.astype(bf16), wo_ref[...])


def _merge(x3d, gates, dsa_outs, ybt, wua, wub, wo, *, tm):
    b, s, _ = x3d.shape
    tok = lambda w: pl.BlockSpec((1, tm, w), lambda bi, i: (bi, i, 0))
    grp = lambda dil: pl.BlockSpec((1, dil, tm // dil, 4 * LANES), lambda bi, i: (bi, 0, i, 0))
    return pl.pallas_call(
        functools.partial(_merge_kernel, tm=tm),
        out_shape=jax.ShapeDtypeStruct((b, s, D_MODEL), f32),
        grid=(b, s // tm),
        in_specs=[
            tok(D_MODEL), tok(2 * D_MODEL), *[grp(dil) for _, dil in DSA_GROUPS],
            pl.BlockSpec((1, MOBA_WIDTH, tm), lambda bi, i: (bi, 0, i)),
            _resident((DSA_OUT, D_MODEL)),
            _resident((MOBA_WIDTH, D_MODEL)),
            _resident((D_MODEL, D_MODEL)),
        ],
        out_specs=tok(D_MODEL),
        scratch_shapes=[pltpu.VMEM((len(DSA_GROUPS), 4, tm, LANES), f32)],
        compiler_params=pltpu.CompilerParams(
            dimension_semantics=("parallel", "parallel"), vmem_limit_bytes=VMEM_LIMIT),
        name="merge",
    )(x3d, gates, *dsa_outs, ybt, wua, wub, wo)


def _alibi_slopes(n):
    return 2.0 ** (-8.0 * np.arange(1, n + 1, dtype=np.float64) / n)


def _split3(x):
    hi = x.astype(bf16)
    r1 = x - hi.astype(f32)
    mid = r1.astype(bf16)
    lo = (r1 - mid.astype(f32)).astype(bf16)
    return hi, mid, lo


def _moba_constants(s):
    nblk = s // MOBA_BLOCK
    n_arow = K_FEAT - nblk
    pos = np.arange(s)
    kfeat = np.zeros((s, K_FEAT), np.float32)
    kfeat[pos, pos // MOBA_BLOCK] = 1.0
    kfeat[:, nblk:nblk + 3] = 1.0
    kfeat[:, nblk + 3:nblk + 6] = (pos % MOBA_BLOCK)[:, None]
    kfeat[:, nblk + 6:nblk + 9] = ((pos // MOBA_BLOCK) % 2)[:, None]
    slopes = jnp.asarray(_alibi_slopes(MOBA_HEADS) * LOG2E, f32)
    off = jnp.arange(MOBA_BLOCK, dtype=f32)
    wide = lambda v: jnp.broadcast_to(v[:, None], (MOBA_HEADS, MOBA_BLOCK))
    qterm = _split3(-slopes[:, None] * off[None, :])
    sterm = _split3(wide(slopes))
    bterm = _split3(wide(slopes * MOBA_BLOCK))
    rows = jnp.stack(list(qterm) + list(sterm) + list(bterm), axis=1)
    rows = jnp.concatenate(
        [rows, jnp.zeros((MOBA_HEADS, n_arow - 9, MOBA_BLOCK), bf16)], axis=1)
    arows = rows.reshape(MOBA_HEADS // 2, 2, n_arow, MOBA_BLOCK)
    return jnp.asarray(kfeat, bf16), arows, slopes


def _dsa_bias(group, dilation):
    slopes = _alibi_slopes(DSA_HEADS)[group * DSA_HEADS_PER_GROUP:(group + 1) * DSA_HEADS_PER_GROUP]
    slopes = jnp.asarray(slopes * LOG2E * dilation, f32).reshape(2, 2, 1, 1, 1)
    qi = jnp.arange(DSA_TQ)[:, None]
    kj = jnp.arange(DSA_BAND)[None, :]
    delta = jnp.stack([qi - kj, qi + DSA_BLK - kj])[None, None]
    valid = (delta >= 0) & (delta <= DSA_BLK)
    return jnp.where(valid, -slopes * delta.astype(f32), NEG)


def _layer(x, norm_ffn1, ffn1_gate, ffn1_up, ffn1_down, norm_mix, w_in, w_up_a, w_up_b, w_out,
           norm_ffn2, ffn2_gate, ffn2_up, ffn2_down, gain_final, *, final_norm):
    b, s, _ = x.shape
    tm = 256
    row = lambda g: g.reshape(1, D_MODEL).astype(f32)
    x1 = _ffn(x.reshape(b * s, D_MODEL), row(norm_ffn1), ffn1_gate.astype(bf16), ffn1_up.astype(bf16),
              ffn1_down.astype(bf16), row(gain_final), final_norm=False, tm=tm).reshape(b, s, D_MODEL)

    scale = HEAD_DIM ** -0.5
    o = 0
    parts = []
    for width in (DSA_WIDTH, DSA_WIDTH, DSA_WIDTH, MOBA_WIDTH, MOBA_WIDTH, MOBA_WIDTH, 2 * D_MODEL):
        parts.append(w_in[:, o:o + width])
        o += width
    wqa, wka, wva, wqb, wkb, wvb, wgate = parts
    qscale = scale * LOG2E
    grp = lambda w, g: w[:, g * DSA_OUT:(g + 1) * DSA_OUT]
    wa = jnp.concatenate(
        [jnp.concatenate([grp(wqa, g) * qscale, grp(wka, g), grp(wva, g)], axis=1)
         for g in range(len(DSA_GROUPS))], axis=1).astype(bf16)
    wqvt = jnp.concatenate([wqb * qscale, wvb], axis=1).T.astype(bf16)
    kfeat, arows, slopes_b = _moba_constants(s)
    vpad = jnp.zeros((V_ROWS - HEAD_DIM, tm), bf16).at[0].set(1.0)
    a0, a1, a2, kaug, qt, vaug, ksum, gates = _proj(
        x1, row(norm_mix), wa, wkb.astype(bf16), wqvt, wgate.astype(bf16), kfeat, vpad, tm=tm)

    dsa_outs = [
        _dsa_group(a, _dsa_bias(gi, dil), group=gi)
        for gi, (a, (_, dil)) in enumerate(zip((a0, a1, a2), DSA_GROUPS))
    ]
    ybt = _moba(slopes_b, qt, kaug, vaug, ksum, arows)
    x2 = _merge(x1, gates, dsa_outs, ybt, w_up_a.astype(bf16), w_up_b.astype(bf16),
                w_out.astype(bf16), tm=tm)
    x3 = _ffn(x2.reshape(b * s, D_MODEL), row(norm_ffn2), ffn2_gate.astype(bf16), ffn2_up.astype(bf16),
              ffn2_down.astype(bf16), row(gain_final), final_norm=final_norm, tm=tm)
    return x3.reshape(b, s, D_MODEL)


def kernel(x, norm_ffn1, ffn1_gate, ffn1_up, ffn1_down, norm_mix, w_in, w_up_a, w_up_b, w_out,
           norm_ffn2, ffn2_gate, ffn2_up, ffn2_down, norm_final):
    depth = norm_ffn1.shape[0]
    for layer in range(depth):
        last = layer == depth - 1
        x = _layer(x, norm_ffn1[layer], ffn1_gate[layer], ffn1_up[layer], ffn1_down[layer],
                   norm_mix[layer], w_in[layer], w_up_a[layer], w_up_b[layer], w_out[layer],
                   norm_ffn2[layer], ffn2_gate[layer], ffn2_up[layer], ffn2_down[layer],
                   norm_final, final_norm=last)
    return x
```

```python
import functools

import numpy as np
import jax
import jax.numpy as jnp
from jax import lax
from jax.experimental import pallas as pl
from jax.experimental.pallas import tpu as pltpu

D_MODEL = 1024
HEAD_DIM = 64
DSA_GROUPS = ((128, 1), (512, 4), (2048, 16))
DSA_HEADS_PER_GROUP = 4
DSA_HEADS = DSA_HEADS_PER_GROUP * len(DSA_GROUPS)
MOBA_HEADS = 8
MOBA_BLOCK = 256
MOBA_TOPK = 3
D_FF = ((8 * D_MODEL // 3 + 127) // 128) * 128
DSA_WIDTH = DSA_HEADS * HEAD_DIM
MOBA_WIDTH = MOBA_HEADS * HEAD_DIM
DSA_OUT = DSA_HEADS_PER_GROUP * HEAD_DIM
EPS = 1e-6
NEG = -1e30

LANES = 128
HEAD_PAIR = 2 * HEAD_DIM
DSA_BLK = 128
DSA_TQ = 256
DSA_BAND = DSA_TQ + DSA_BLK
DSA_GROUP_COLS = 3 * DSA_OUT
MOBA_TILE = 2 * MOBA_BLOCK
MAX_SLABS = 8
LOG2E = 1.4426950408889634
M_INIT = -3.0e38
V_ROWS = 80
K_FEAT = 128
VMEM_LIMIT = 56 * 1024 * 1024

f32 = jnp.float32
bf16 = jnp.bfloat16


def _rmsnorm(x, g):
    return x * lax.rsqrt(jnp.mean(x * x, axis=-1, keepdims=True) + EPS) * g


def _dot(a, b):
    return jnp.dot(a, b, preferred_element_type=f32)


def _dot_nt(a, b):
    return lax.dot_general(a, b, (((1,), (1,)), ((), ())), preferred_element_type=f32)


def _dot_tn(a, b):
    return lax.dot_general(a, b, (((0,), (0,)), ((), ())), preferred_element_type=f32)


def _resident(shape):
    nd = len(shape)
    return pl.BlockSpec(shape, lambda *_: (0,) * nd, pipeline_mode=pl.Buffered(1))


def _ffn_kernel(x_ref, g_ref, wg_ref, wu_ref, wd_ref, gf_ref, o_ref, *, final_norm):
    x = x_ref[...]
    h = _rmsnorm(x, g_ref[...]).astype(bf16)
    gate = _dot(h, wg_ref[...])
    up = _dot(h, wu_ref[...])
    act = (gate * jax.nn.sigmoid(gate) * up).astype(bf16)
    y = x + 0.5 * _dot(act, wd_ref[...])
    if final_norm:
        y = _rmsnorm(y, gf_ref[...])
    o_ref[...] = y


def _ffn(x2d, gain, w_gate, w_up, w_down, gain_final, *, final_norm, tm):
    n = x2d.shape[0]
    return pl.pallas_call(
        functools.partial(_ffn_kernel, final_norm=final_norm),
        out_shape=jax.ShapeDtypeStruct((n, D_MODEL), f32),
        grid=(n // tm,),
        in_specs=[
            pl.BlockSpec((tm, D_MODEL), lambda i: (i, 0)),
            _resident((1, D_MODEL)),
            _resident((D_MODEL, D_FF)),
            _resident((D_MODEL, D_FF)),
            _resident((D_FF, D_MODEL)),
            _resident((1, D_MODEL)),
        ],
        out_specs=pl.BlockSpec((tm, D_MODEL), lambda i: (i, 0)),
        compiler_params=pltpu.CompilerParams(
            dimension_semantics=("parallel",), vmem_limit_bytes=VMEM_LIMIT),
        name="ffn_final" if final_norm else "ffn",
    )(x2d, gain, w_gate, w_up, w_down, gain_final)


def _proj_kernel(x_ref, g_ref, wa_ref, wkb_ref, wqvt_ref, wgate_ref, kfeat_ref, vpad_ref,
                 a0_ref, a1_ref, a2_ref, kaug_ref, qt_ref, vaug_ref, ksum_ref, gates_ref, scr_ref, *, tm):
    h = _rmsnorm(x_ref[0], g_ref[...]).astype(bf16)
    qkv = _dot(h, wa_ref[...])
    tiles_per_group = DSA_GROUP_COLS // LANES
    for g, (a_ref, (_, dil)) in enumerate(zip((a0_ref, a1_ref, a2_ref), DSA_GROUPS)):
        for c in range(tiles_per_group):
            lo = g * DSA_GROUP_COLS + c * LANES
            if dil == 1:
                a_ref[0, 0, :, c * LANES:(c + 1) * LANES] = qkv[:, lo:lo + LANES].astype(bf16)
                continue
            scr_ref[g, c] = qkv[:, lo:lo + LANES]
            for r in range(dil):
                a_ref[0, r, :, c * LANES:(c + 1) * LANES] = (
                    scr_ref[g, c, pl.ds(r, tm // dil, stride=dil), :].astype(bf16))
    kb = _dot(h, wkb_ref[...])
    ksum_ref[...] = jnp.sum(kb.reshape(tm // MOBA_BLOCK, MOBA_BLOCK, MOBA_WIDTH), axis=1)[:, None, :]
    kb = kb.astype(bf16)
    for p in range(MOBA_HEADS // 2):
        kaug_ref[0, p, :, 0:HEAD_PAIR] = kb[:, p * HEAD_PAIR:(p + 1) * HEAD_PAIR]
        kaug_ref[0, p, :, HEAD_PAIR:HEAD_PAIR + K_FEAT] = kfeat_ref[...]
    qv = _dot_nt(wqvt_ref[...], h)
    qt_ref[0] = qv[0:MOBA_WIDTH].astype(bf16)
    for hd in range(MOBA_HEADS):
        lo = MOBA_WIDTH + hd * HEAD_DIM
        vaug_ref[0, hd, 0:HEAD_DIM, :] = qv[lo:lo + HEAD_DIM].astype(bf16)
        vaug_ref[0, hd, HEAD_DIM:V_ROWS, :] = vpad_ref[...]
    gates_ref[0] = jax.nn.sigmoid(_dot(h, wgate_ref[...])).astype(bf16)


def _proj(x3d, gain, wa, wkb, wqvt, wgate, kfeat, vpad, *, tm):
    b, s, _ = x3d.shape
    nblk = s // MOBA_BLOCK
    grp_shapes = tuple(
        jax.ShapeDtypeStruct((b, dil, s // dil, DSA_GROUP_COLS), bf16) for _, dil in DSA_GROUPS)
    grp_specs = tuple(
        pl.BlockSpec((1, dil, tm // dil, DSA_GROUP_COLS), lambda bi, i: (bi, 0, i, 0))
        for _, dil in DSA_GROUPS)
    out_shape = grp_shapes + (
        jax.ShapeDtypeStruct((b, MOBA_HEADS // 2, s, HEAD_PAIR + K_FEAT), bf16),
        jax.ShapeDtypeStruct((b, MOBA_WIDTH, s), bf16),
        jax.ShapeDtypeStruct((b, MOBA_HEADS, V_ROWS, s), bf16),
        jax.ShapeDtypeStruct((b * nblk, 1, MOBA_WIDTH), f32),
        jax.ShapeDtypeStruct((b, s, 2 * D_MODEL), bf16),
    )
    per = tm // MOBA_BLOCK
    return pl.pallas_call(
        functools.partial(_proj_kernel, tm=tm),
        out_shape=out_shape,
        grid=(b, s // tm),
        in_specs=[
            pl.BlockSpec((1, tm, D_MODEL), lambda bi, i: (bi, i, 0)),
            _resident((1, D_MODEL)),
            _resident((D_MODEL, 3 * DSA_WIDTH)),
            _resident((D_MODEL, MOBA_WIDTH)),
            _resident((2 * MOBA_WIDTH, D_MODEL)),
            _resident((D_MODEL, 2 * D_MODEL)),
            pl.BlockSpec((tm, K_FEAT), lambda bi, i: (i, 0)),
            _resident((V_ROWS - HEAD_DIM, tm)),
        ],
        out_specs=grp_specs + (
            pl.BlockSpec((1, MOBA_HEADS // 2, tm, HEAD_PAIR + K_FEAT), lambda bi, i: (bi, 0, i, 0)),
            pl.BlockSpec((1, MOBA_WIDTH, tm), lambda bi, i: (bi, 0, i)),
            pl.BlockSpec((1, MOBA_HEADS, V_ROWS, tm), lambda bi, i: (bi, 0, 0, i)),
            pl.BlockSpec((per, 1, MOBA_WIDTH), lambda bi, i: (bi * (s // tm) + i, 0, 0)),
            pl.BlockSpec((1, tm, 2 * D_MODEL), lambda bi, i: (bi, i, 0)),
        ),
        scratch_shapes=[pltpu.VMEM((len(DSA_GROUPS), DSA_GROUP_COLS // LANES, tm, LANES), f32)],
        compiler_params=pltpu.CompilerParams(
            dimension_semantics=("parallel", "parallel"), vmem_limit_bytes=VMEM_LIMIT),
        name="proj",
    )(x3d, gain, wa, wkb, wqvt, wgate, kfeat, vpad)


def _dsa_kernel(q_ref, k_ref, v_ref, bias_ref, o_ref, sa_ref, sb_ref, *, ntile):
    tq, band = DSA_TQ, DSA_BAND
    first = lax.broadcasted_iota(jnp.int32, (tq, LANES), 1) < HEAD_DIM

    def rows_of(t):
        return pl.ds(pl.multiple_of(t * tq, tq), tq)

    def band_of(t):
        return pl.ds(pl.multiple_of(jnp.maximum(t * tq - DSA_BLK, 0), DSA_BLK), band)

    def scores(t, s_ref):
        q = q_ref[0, 0, rows_of(t), :]
        kband = k_ref[0, 0, band_of(t), :]
        for h in range(2):
            qm = jnp.where(first if h == 0 else jnp.logical_not(first), q, jnp.zeros_like(q))
            s_ref[h] = _dot_nt(qm, kband)

    def finish(t, s_ref):
        variant = jnp.minimum(t, 1)
        vband = v_ref[0, 0, band_of(t), :]
        outs, lses = [], []
        for h in range(2):
            s = s_ref[h] + bias_ref[0, h, variant]
            m = jnp.max(s, axis=-1, keepdims=True)
            p = jnp.exp2(s - m)
            l = jnp.sum(p, axis=-1, keepdims=True)
            o = _dot(p.astype(bf16), vband)
            outs.append(o / l)
            lses.append(jnp.broadcast_to(m + jnp.log(l) * LOG2E, (tq, LANES)))
        o_ref[0, 0, rows_of(t), 0:LANES] = jnp.where(first, outs[0], outs[1])
        o_ref[0, 0, rows_of(t), LANES:2 * LANES] = jnp.where(first, lses[0], lses[1])

    zero = jnp.int32(0)
    scores(zero, sa_ref)

    def body(i2, carry):
        t = 2 * i2
        scores(t + 1, sb_ref)
        finish(t, sa_ref)
        scores(t + 2, sa_ref)
        finish(t + 1, sb_ref)
        return carry

    lax.fori_loop(0, ntile // 2 - 1, body, 0)
    t_end = zero + (ntile - 2)
    scores(t_end + 1, sb_ref)
    finish(t_end, sa_ref)
    finish(t_end + 1, sb_ref)


def _dsa_group(a, bias, *, group):
    b, dil, length, _ = a.shape
    pairs = DSA_HEADS_PER_GROUP // 2
    assert length % (2 * DSA_TQ) == 0
    col = lambda part: (lambda bi, r, hp: (bi, r, 0, part * pairs + hp))
    return pl.pallas_call(
        functools.partial(_dsa_kernel, ntile=length // DSA_TQ),
        out_shape=jax.ShapeDtypeStruct((b, dil, length, 2 * pairs * LANES), f32),
        grid=(b, dil, pairs),
        in_specs=[
            pl.BlockSpec((1, 1, length, LANES), col(0)),
            pl.BlockSpec((1, 1, length, LANES), col(1)),
            pl.BlockSpec((1, 1, length, LANES), col(2)),
            pl.BlockSpec((1, 2, 2, DSA_TQ, DSA_BAND), lambda bi, r, hp: (hp, 0, 0, 0, 0)),
        ],
        out_specs=pl.BlockSpec((1, 1, length, 2 * LANES), lambda bi, r, hp: (bi, r, 0, hp)),
        scratch_shapes=[pltpu.VMEM((2, DSA_TQ, DSA_BAND), f32), pltpu.VMEM((2, DSA_TQ, DSA_BAND), f32)],
        compiler_params=pltpu.CompilerParams(
            dimension_semantics=("parallel", "parallel", "parallel"), vmem_limit_bytes=VMEM_LIMIT),
        name=f"dsa_g{group}",
    )(a, a, a, bias)


def _moba_kernel(slope_ref, qt_ref, kaug_ref, vaug_ref, ksum_ref, arow_ref, cmask_ref, o_ref,
                 rhs_ref, sa_ref, sb_ref, acc_ref, m_ref, *, nblk):
    hp = pl.program_id(1)
    n2 = pl.program_id(2)
    blk = MOBA_BLOCK
    tile = MOBA_TILE
    ntile = n2 + 1
    chains = [(qb, h) for qb in range(2) for h in range(2)]
    qpair = qt_ref[0]
    rowi = lax.broadcasted_iota(jnp.int32, (HEAD_PAIR, tile), 0)
    bidx = lax.broadcasted_iota(jnp.int32, (nblk, tile), 0)
    own = 2 * n2 + (lax.broadcasted_iota(jnp.int32, (nblk, tile), 1) >= blk).astype(jnp.int32)
    past = bidx < own
    kmean = ksum_ref[0] * (1.0 / blk)
    kmean_hi = kmean.astype(bf16)
    kmean_lo = (kmean - kmean_hi.astype(f32)).astype(bf16)

    for h in range(2):
        mine = (rowi >= HEAD_DIM) if h else (rowi < HEAD_DIM)
        qpad = jnp.where(mine, qpair, jnp.zeros_like(qpair))
        gate = _dot(kmean_hi, qpad) + _dot(kmean_lo, qpad)
        g = jnp.where(past, gate, NEG)
        sel = bidx == own
        for _ in range(MOBA_TOPK):
            mx = jnp.max(g, axis=0, keepdims=True)
            idx = jnp.min(jnp.where(g == mx, bidx, nblk), axis=0, keepdims=True)
            pick = bidx == idx
            sel = sel | (pick & past)
            g = jnp.where(pick, -jnp.inf, g)
        maskbias = jnp.where(sel, 0.0, NEG).astype(bf16)
        for qb in range(2):
            c = 2 * qb + h
            cols = slice(qb * blk, (qb + 1) * blk)
            rhs_ref[c, 0:HEAD_PAIR, :] = qpad[:, cols]
            rhs_ref[c, HEAD_PAIR:HEAD_PAIR + nblk, :] = maskbias[:, cols]
            rhs_ref[c, HEAD_PAIR + nblk:HEAD_PAIR + K_FEAT, :] = arow_ref[0, h]
            acc_ref[c] = jnp.zeros((V_ROWS, blk), f32)
            m_ref[c] = jnp.full((1, blk), M_INIT, f32)

    def scores(i, s_ref):
        k0 = pl.multiple_of(i * tile, tile)
        kt = kaug_ref[0, 0, pl.ds(k0, tile), :]
        for c in range(len(chains)):
            s_ref[c] = _dot(kt, rhs_ref[c])

    def update(i, s_ref, diagonal):
        k0 = pl.multiple_of(i * tile, tile)
        for c, (qb, h) in enumerate(chains):
            s = s_ref[c]
            if diagonal:
                s = s + cmask_ref[qb]
            rel = tile * (i - n2) - blk * qb
            shift = slope_ref[2 * hp + h] * rel.astype(f32)
            m_old = m_ref[c]
            smax = jnp.max(s.reshape(MAX_SLABS, tile // MAX_SLABS, blk), axis=0)
            m_new = jnp.maximum(m_old, jnp.max(smax, axis=0, keepdims=True) + shift)
            p = jnp.exp2(s - (m_new - shift)).astype(bf16)
            pv = _dot(vaug_ref[0, h, :, pl.ds(k0, tile)], p)
            acc_ref[c] = jnp.exp2(m_old - m_new) * acc_ref[c] + pv
            m_ref[c] = m_new

    scores(0, sa_ref)
    npair = (ntile - 1) // 2

    def body(i2, carry):
        t = 2 * i2
        scores(t + 1, sb_ref)
        update(t, sa_ref, False)
        scores(t + 2, sa_ref)
        update(t + 1, sb_ref, False)
        return carry

    lax.fori_loop(0, npair, body, 0)
    t_end = 2 * npair

    @pl.when(ntile - t_end == 1)
    def _():
        update(t_end, sa_ref, True)

    @pl.when(ntile - t_end == 2)
    def _():
        scores(t_end + 1, sb_ref)
        update(t_end, sa_ref, False)
        update(t_end + 1, sb_ref, True)

    for c, (qb, h) in enumerate(chains):
        acc = acc_ref[c]
        o_ref[0, h * HEAD_DIM:(h + 1) * HEAD_DIM, qb * blk:(qb + 1) * blk] = (
            acc[0:HEAD_DIM] / acc[HEAD_DIM:HEAD_DIM + 1]).astype(bf16)


def _moba(slopes, qt, kaug, vaug, ksum, arows, cmask):
    b, _, s = qt.shape
    nblk = s // MOBA_BLOCK
    assert nblk + arows.shape[2] == K_FEAT and nblk % 2 == 0
    pairs = MOBA_HEADS // 2
    nchain = 4
    return pl.pallas_call(
        functools.partial(_moba_kernel, nblk=nblk),
        out_shape=jax.ShapeDtypeStruct((b, MOBA_WIDTH, s), bf16),
        grid=(b, pairs, nblk // 2),
        in_specs=[
            pl.BlockSpec(memory_space=pltpu.SMEM),
            pl.BlockSpec((1, HEAD_PAIR, MOBA_TILE), lambda bi, hp, n2: (bi, hp, n2)),
            pl.BlockSpec((1, 1, s, HEAD_PAIR + K_FEAT), lambda bi, hp, n2: (bi, hp, 0, 0)),
            pl.BlockSpec((1, 2, V_ROWS, s), lambda bi, hp, n2: (bi, hp, 0, 0)),
            pl.BlockSpec((1, nblk, HEAD_PAIR), lambda bi, hp, n2: (bi, 0, hp)),
            pl.BlockSpec((1, 2, K_FEAT - nblk, MOBA_BLOCK), lambda bi, hp, n2: (hp, 0, 0, 0)),
            _resident((2, MOBA_TILE, MOBA_BLOCK)),
        ],
        out_specs=pl.BlockSpec((1, HEAD_PAIR, MOBA_TILE), lambda bi, hp, n2: (bi, hp, n2)),
        scratch_shapes=[
            pltpu.VMEM((nchain, HEAD_PAIR + K_FEAT, MOBA_BLOCK), bf16),
            pltpu.VMEM((nchain, MOBA_TILE, MOBA_BLOCK), f32),
            pltpu.VMEM((nchain, MOBA_TILE, MOBA_BLOCK), f32),
            pltpu.VMEM((nchain, V_ROWS, MOBA_BLOCK), f32),
            pltpu.VMEM((nchain, 1, MOBA_BLOCK), f32),
        ],
        compiler_params=pltpu.CompilerParams(
            dimension_semantics=("parallel", "parallel", "arbitrary"), vmem_limit_bytes=VMEM_LIMIT),
        name="moba",
    )(slopes, qt, kaug, vaug, ksum.reshape(b, nblk, MOBA_WIDTH), arows, cmask)


def _merge_kernel(x_ref, gates_ref, d0_ref, d1_ref, d2_ref, ybt_ref, wua_ref, wub_ref, wo_ref, o_ref,
                  nat_ref, *, tm):
    def token_order(g, c):
        d_ref, dil = (d0_ref, d1_ref, d2_ref)[g], DSA_GROUPS[g][1]
        if dil == 1:
            return d_ref[0, 0, :, c * LANES:(c + 1) * LANES]
        for r in range(dil):
            nat_ref[g, c, pl.ds(r, tm // dil, stride=dil), :] = d_ref[0, r, :, c * LANES:(c + 1) * LANES]
        return nat_ref[g, c]

    pairs = []
    for hp in range(2):
        outs = [token_order(g, 2 * hp) for g in range(len(DSA_GROUPS))]
        lses = [token_order(g, 2 * hp + 1) for g in range(len(DSA_GROUPS))]
        mx = jnp.maximum(jnp.maximum(lses[0], lses[1]), lses[2])
        es = [jnp.exp2(l - mx) for l in lses]
        den = es[0] + es[1] + es[2]
        pairs.append((es[0] * outs[0] + es[1] * outs[1] + es[2] * outs[2]) / den)
    ya = jnp.concatenate(pairs, axis=1).astype(bf16)
    ta = _dot(ya, wua_ref[...])
    tb = _dot_tn(ybt_ref[0], wub_ref[...])
    gates = gates_ref[0]
    merged = gates[:, 0:D_MODEL].astype(f32) * ta + gates[:, D_MODEL:].astype(f32) * tb
    o_ref[0] = x_ref[0] + _dot(merged.astype(bf16), wo_ref[...])


def _merge(x3d, gates, dsa_outs, ybt, wua, wub, wo, *, tm):
    b, s, _ = x3d.shape
    tok = lambda w: pl.BlockSpec((1, tm, w), lambda bi, i: (bi, i, 0))
    grp = lambda dil: pl.BlockSpec((1, dil, tm // dil, 4 * LANES), lambda bi, i: (bi, 0, i, 0))
    return pl.pallas_call(
        functools.partial(_merge_kernel, tm=tm),
        out_shape=jax.ShapeDtypeStruct((b, s, D_MODEL), f32),
        grid=(b, s // tm),
        in_specs=[
            tok(D_MODEL), tok(2 * D_MODEL), *[grp(dil) for _, dil in DSA_GROUPS],
            pl.BlockSpec((1, MOBA_WIDTH, tm), lambda bi, i: (bi, 0, i)),
            _resident((DSA_OUT, D_MODEL)),
            _resident((MOBA_WIDTH, D_MODEL)),
            _resident((D_MODEL, D_MODEL)),
        ],
        out_specs=tok(D_MODEL),
        scratch_shapes=[pltpu.VMEM((len(DSA_GROUPS), 4, tm, LANES), f32)],
        compiler_params=pltpu.CompilerParams(
            dimension_semantics=("parallel", "parallel"), vmem_limit_bytes=VMEM_LIMIT),
        name="merge",
    )(x3d, gates, *dsa_outs, ybt, wua, wub, wo)


def _alibi_slopes(n):
    return 2.0 ** (-8.0 * np.arange(1, n + 1, dtype=np.float64) / n)


def _split3(x):
    hi = x.astype(bf16)
    r1 = x - hi.astype(f32)
    mid = r1.astype(bf16)
    lo = (r1 - mid.astype(f32)).astype(bf16)
    return hi, mid, lo


def _moba_constants(s):
    nblk = s // MOBA_BLOCK
    n_arow = K_FEAT - nblk
    pos = np.arange(s)
    kfeat = np.zeros((s, K_FEAT), np.float32)
    kfeat[pos, pos // MOBA_BLOCK] = 1.0
    kfeat[:, nblk:nblk + 3] = 1.0
    kfeat[:, nblk + 3:nblk + 6] = (pos % MOBA_BLOCK)[:, None]
    kfeat[:, nblk + 6:nblk + 9] = ((pos // MOBA_BLOCK) % 2)[:, None]
    slopes = jnp.asarray(_alibi_slopes(MOBA_HEADS) * LOG2E, f32)
    off = jnp.arange(MOBA_BLOCK, dtype=f32)
    wide = lambda v: jnp.broadcast_to(v[:, None], (MOBA_HEADS, MOBA_BLOCK))
    qterm = _split3(-slopes[:, None] * off[None, :])
    sterm = _split3(wide(slopes))
    bterm = _split3(wide(slopes * MOBA_BLOCK))
    rows = jnp.stack(list(qterm) + list(sterm) + list(bterm), axis=1)
    rows = jnp.concatenate(
        [rows, jnp.zeros((MOBA_HEADS, n_arow - 9, MOBA_BLOCK), bf16)], axis=1)
    arows = rows.reshape(MOBA_HEADS // 2, 2, n_arow, MOBA_BLOCK)
    r = jnp.arange(MOBA_TILE)[None, :, None]
    c = jnp.arange(MOBA_BLOCK)[None, None, :] + MOBA_BLOCK * jnp.arange(2)[:, None, None]
    cmask = jnp.where(r <= c, 0.0, NEG).astype(f32)
    return jnp.asarray(kfeat, bf16), arows, slopes, cmask


def _dsa_bias(group, dilation):
    slopes = _alibi_slopes(DSA_HEADS)[group * DSA_HEADS_PER_GROUP:(group + 1) * DSA_HEADS_PER_GROUP]
    slopes = jnp.asarray(slopes * LOG2E * dilation, f32).reshape(2, 2, 1, 1, 1)
    qi = jnp.arange(DSA_TQ)[:, None]
    kj = jnp.arange(DSA_BAND)[None, :]
    delta = jnp.stack([qi - kj, qi + DSA_BLK - kj])[None, None]
    valid = (delta >= 0) & (delta <= DSA_BLK)
    return jnp.where(valid, -slopes * delta.astype(f32), NEG)


def _layer(x, norm_ffn1, ffn1_gate, ffn1_up, ffn1_down, norm_mix, w_in, w_up_a, w_up_b, w_out,
           norm_ffn2, ffn2_gate, ffn2_up, ffn2_down, gain_final, *, final_norm):
    b, s, _ = x.shape
    tm = 256
    tm_ffn = 512
    row = lambda g: g.reshape(1, D_MODEL).astype(f32)
    x1 = _ffn(x.reshape(b * s, D_MODEL), row(norm_ffn1), ffn1_gate.astype(bf16), ffn1_up.astype(bf16),
              ffn1_down.astype(bf16), row(gain_final), final_norm=False, tm=tm_ffn).reshape(b, s, D_MODEL)

    scale = HEAD_DIM ** -0.5
    o = 0
    parts = []
    for width in (DSA_WIDTH, DSA_WIDTH, DSA_WIDTH, MOBA_WIDTH, MOBA_WIDTH, MOBA_WIDTH, 2 * D_MODEL):
        parts.append(w_in[:, o:o + width])
        o += width
    wqa, wka, wva, wqb, wkb, wvb, wgate = parts
    qscale = scale * LOG2E
    grp = lambda w, g: w[:, g * DSA_OUT:(g + 1) * DSA_OUT]
    wa = jnp.concatenate(
        [jnp.concatenate([grp(wqa, g) * qscale, grp(wka, g), grp(wva, g)], axis=1)
         for g in range(len(DSA_GROUPS))], axis=1).astype(bf16)
    wqvt = jnp.concatenate([wqb * qscale, wvb], axis=1).T.astype(bf16)
    kfeat, arows, slopes_b, cmask = _moba_constants(s)
    vpad = jnp.zeros((V_ROWS - HEAD_DIM, tm), bf16).at[0].set(1.0)
    a0, a1, a2, kaug, qt, vaug, ksum, gates = _proj(
        x1, row(norm_mix), wa, wkb.astype(bf16), wqvt, wgate.astype(bf16), kfeat, vpad, tm=tm)

    dsa_outs = [
        _dsa_group(a, _dsa_bias(gi, dil), group=gi)
        for gi, (a, (_, dil)) in enumerate(zip((a0, a1, a2), DSA_GROUPS))
    ]
    ybt = _moba(slopes_b, qt, kaug, vaug, ksum, arows, cmask)
    x2 = _merge(x1, gates, dsa_outs, ybt, w_up_a.astype(bf16), w_up_b.astype(bf16),
                w_out.astype(bf16), tm=tm)
    x3 = _ffn(x2.reshape(b * s, D_MODEL), row(norm_ffn2), ffn2_gate.astype(bf16), ffn2_up.astype(bf16),
              ffn2_down.astype(bf16), row(gain_final), final_norm=final_norm, tm=tm_ffn)
    return x3.reshape(b, s, D_MODEL)


def kernel(x, norm_ffn1, ffn1_gate, ffn1_up, ffn1_down, norm_mix, w_in, w_up_a, w_up_b, w_out,
           norm_ffn2, ffn2_gate, ffn2_up, ffn2_down, norm_final):
    depth = norm_ffn1.shape[0]
    for layer in range(depth):
        last = layer == depth - 1
        x = _layer(x, norm_ffn1[layer], ffn1_gate[layer], ffn1_up[layer], ffn1_down[layer],
                   norm_mix[layer], w_in[layer], w_up_a[layer], w_up_b[layer], w_out[layer],
                   norm_ffn2[layer], ffn2_gate[layer], ffn2_up[layer], ffn2_down[layer],
                   norm_final, final_norm=last)
    return x
```

```python
import functools

import numpy as np
import jax
import jax.numpy as jnp
from jax import lax
from jax.experimental import pallas as pl
from jax.experimental.pallas import tpu as pltpu

D_MODEL = 1024
HEAD_DIM = 64
DSA_GROUPS = ((128, 1), (512, 4), (2048, 16))
DSA_HEADS_PER_GROUP = 4
DSA_HEADS = DSA_HEADS_PER_GROUP * len(DSA_GROUPS)
MOBA_HEADS = 8
MOBA_BLOCK = 256
MOBA_TOPK = 3
D_FF = ((8 * D_MODEL // 3 + 127) // 128) * 128
DSA_WIDTH = DSA_HEADS * HEAD_DIM
MOBA_WIDTH = MOBA_HEADS * HEAD_DIM
DSA_OUT = DSA_HEADS_PER_GROUP * HEAD_DIM
EPS = 1e-6
NEG = -1e30

LANES = 128
HEAD_PAIR = 2 * HEAD_DIM
DSA_BLK = 128
DSA_TQ = DSA_BLK
DSA_BAND = DSA_TQ + DSA_BLK
DSA_STAGE_TILES = 2
DSA_GROUP_COLS = 3 * DSA_OUT
MOBA_TILE = 2 * MOBA_BLOCK
MAX_SLABS = 8
LOG2E = 1.4426950408889634
M_INIT = -3.0e38
V_ROWS = 80
K_FEAT = 128
VMEM_LIMIT = 56 * 1024 * 1024

f32 = jnp.float32
bf16 = jnp.bfloat16


def _rmsnorm(x, g):
    return x * lax.rsqrt(jnp.mean(x * x, axis=-1, keepdims=True) + EPS) * g


def _dot(a, b):
    return jnp.dot(a, b, preferred_element_type=f32)


def _dot_nt(a, b):
    return lax.dot_general(a, b, (((1,), (1,)), ((), ())), preferred_element_type=f32)


def _dot_tn(a, b):
    return lax.dot_general(a, b, (((0,), (0,)), ((), ())), preferred_element_type=f32)


def _resident(shape):
    nd = len(shape)
    return pl.BlockSpec(shape, lambda *_: (0,) * nd, pipeline_mode=pl.Buffered(1))


def _ffn_kernel(x_ref, g_ref, wg_ref, wu_ref, wd_ref, gf_ref, o_ref, *, final_norm):
    x = x_ref[...]
    h = _rmsnorm(x, g_ref[...]).astype(bf16)
    gate = _dot(h, wg_ref[...])
    up = _dot(h, wu_ref[...])
    act = (gate * jax.nn.sigmoid(gate) * up).astype(bf16)
    y = x + 0.5 * _dot(act, wd_ref[...])
    if final_norm:
        y = _rmsnorm(y, gf_ref[...])
    o_ref[...] = y


def _ffn(x2d, gain, w_gate, w_up, w_down, gain_final, *, final_norm, tm):
    n = x2d.shape[0]
    return pl.pallas_call(
        functools.partial(_ffn_kernel, final_norm=final_norm),
        out_shape=jax.ShapeDtypeStruct((n, D_MODEL), f32),
        grid=(n // tm,),
        in_specs=[
            pl.BlockSpec((tm, D_MODEL), lambda i: (i, 0)),
            _resident((1, D_MODEL)),
            _resident((D_MODEL, D_FF)),
            _resident((D_MODEL, D_FF)),
            _resident((D_FF, D_MODEL)),
            _resident((1, D_MODEL)),
        ],
        out_specs=pl.BlockSpec((tm, D_MODEL), lambda i: (i, 0)),
        compiler_params=pltpu.CompilerParams(
            dimension_semantics=("parallel",), vmem_limit_bytes=VMEM_LIMIT),
        name="ffn_final" if final_norm else "ffn",
    )(x2d, gain, w_gate, w_up, w_down, gain_final)


def _proj_kernel(x_ref, g_ref, wa_ref, wkb_ref, wqvt_ref, wgate_ref, kfeat_ref, vpad_ref,
                 a0_ref, a1_ref, a2_ref, kaug_ref, qt_ref, vaug_ref, ksum_ref, gates_ref, scr_ref, *, tm):
    h = _rmsnorm(x_ref[0], g_ref[...]).astype(bf16)
    qkv = _dot(h, wa_ref[...])
    tiles_per_group = DSA_GROUP_COLS // LANES
    for g, (a_ref, (_, dil)) in enumerate(zip((a0_ref, a1_ref, a2_ref), DSA_GROUPS)):
        for c in range(tiles_per_group):
            lo = g * DSA_GROUP_COLS + c * LANES
            if dil == 1:
                a_ref[0, 0, :, c * LANES:(c + 1) * LANES] = qkv[:, lo:lo + LANES].astype(bf16)
                continue
            scr_ref[g, c] = qkv[:, lo:lo + LANES]
            for r in range(dil):
                a_ref[0, r, :, c * LANES:(c + 1) * LANES] = (
                    scr_ref[g, c, pl.ds(r, tm // dil, stride=dil), :].astype(bf16))
    kb = _dot(h, wkb_ref[...])
    ksum_ref[...] = jnp.sum(kb.reshape(tm // MOBA_BLOCK, MOBA_BLOCK, MOBA_WIDTH), axis=1)[:, None, :]
    kb = kb.astype(bf16)
    for p in range(MOBA_HEADS // 2):
        kaug_ref[0, p, :, 0:HEAD_PAIR] = kb[:, p * HEAD_PAIR:(p + 1) * HEAD_PAIR]
        kaug_ref[0, p, :, HEAD_PAIR:HEAD_PAIR + K_FEAT] = kfeat_ref[...]
    qv = _dot_nt(wqvt_ref[...], h)
    qt_ref[0] = qv[0:MOBA_WIDTH].astype(bf16)
    for hd in range(MOBA_HEADS):
        lo = MOBA_WIDTH + hd * HEAD_DIM
        vaug_ref[0, hd, 0:HEAD_DIM, :] = qv[lo:lo + HEAD_DIM].astype(bf16)
        vaug_ref[0, hd, HEAD_DIM:V_ROWS, :] = vpad_ref[...]
    gates_ref[0] = jax.nn.sigmoid(_dot(h, wgate_ref[...])).astype(bf16)


def _proj(x3d, gain, wa, wkb, wqvt, wgate, kfeat, vpad, *, tm):
    b, s, _ = x3d.shape
    nblk = s // MOBA_BLOCK
    grp_shapes = tuple(
        jax.ShapeDtypeStruct((b, dil, s // dil, DSA_GROUP_COLS), bf16) for _, dil in DSA_GROUPS)
    grp_specs = tuple(
        pl.BlockSpec((1, dil, tm // dil, DSA_GROUP_COLS), lambda bi, i: (bi, 0, i, 0))
        for _, dil in DSA_GROUPS)
    out_shape = grp_shapes + (
        jax.ShapeDtypeStruct((b, MOBA_HEADS // 2, s, HEAD_PAIR + K_FEAT), bf16),
        jax.ShapeDtypeStruct((b, MOBA_WIDTH, s), bf16),
        jax.ShapeDtypeStruct((b, MOBA_HEADS, V_ROWS, s), bf16),
        jax.ShapeDtypeStruct((b * nblk, 1, MOBA_WIDTH), f32),
        jax.ShapeDtypeStruct((b, s, 2 * D_MODEL), bf16),
    )
    per = tm // MOBA_BLOCK
    return pl.pallas_call(
        functools.partial(_proj_kernel, tm=tm),
        out_shape=out_shape,
        grid=(b, s // tm),
        in_specs=[
            pl.BlockSpec((1, tm, D_MODEL), lambda bi, i: (bi, i, 0)),
            _resident((1, D_MODEL)),
            _resident((D_MODEL, 3 * DSA_WIDTH)),
            _resident((D_MODEL, MOBA_WIDTH)),
            _resident((2 * MOBA_WIDTH, D_MODEL)),
            _resident((D_MODEL, 2 * D_MODEL)),
            pl.BlockSpec((tm, K_FEAT), lambda bi, i: (i, 0)),
            _resident((V_ROWS - HEAD_DIM, tm)),
        ],
        out_specs=grp_specs + (
            pl.BlockSpec((1, MOBA_HEADS // 2, tm, HEAD_PAIR + K_FEAT), lambda bi, i: (bi, 0, i, 0)),
            pl.BlockSpec((1, MOBA_WIDTH, tm), lambda bi, i: (bi, 0, i)),
            pl.BlockSpec((1, MOBA_HEADS, V_ROWS, tm), lambda bi, i: (bi, 0, 0, i)),
            pl.BlockSpec((per, 1, MOBA_WIDTH), lambda bi, i: (bi * (s // tm) + i, 0, 0)),
            pl.BlockSpec((1, tm, 2 * D_MODEL), lambda bi, i: (bi, i, 0)),
        ),
        scratch_shapes=[pltpu.VMEM((len(DSA_GROUPS), DSA_GROUP_COLS // LANES, tm, LANES), f32)],
        compiler_params=pltpu.CompilerParams(
            dimension_semantics=("parallel", "parallel"), vmem_limit_bytes=VMEM_LIMIT),
        name="proj",
    )(x3d, gain, wa, wkb, wqvt, wgate, kfeat, vpad)


def _dsa_kernel(q_ref, k_ref, v_ref, bias_ref, o_ref, sa_ref, sb_ref, *, nstage, seg_tiles):
    tq, band = DSA_TQ, DSA_BAND
    first = lax.broadcasted_iota(jnp.int32, (tq, LANES), 1) < HEAD_DIM
    chains = [(u, h) for u in range(DSA_STAGE_TILES) for h in range(2)]

    def rows_of(t):
        return pl.ds(pl.multiple_of(t * tq, tq), tq)

    def seg_start(t):
        return (t % seg_tiles) == 0

    def band_of(t):
        start = jnp.where(seg_start(t), t * tq, t * tq - DSA_BLK)
        return pl.ds(pl.multiple_of(start, DSA_BLK), band)

    def scores(stage, s_ref):
        for u in range(DSA_STAGE_TILES):
            t = stage * DSA_STAGE_TILES + u
            q = q_ref[0, rows_of(t), :]
            kband = k_ref[0, band_of(t), :]
            for h in range(2):
                qm = jnp.where(first if h == 0 else jnp.logical_not(first), q, jnp.zeros_like(q))
                s_ref[2 * u + h] = _dot_nt(qm, kband)

    def finish(stage, s_ref):
        for u in range(DSA_STAGE_TILES):
            t = stage * DSA_STAGE_TILES + u
            variant = jnp.where(seg_start(t), 0, 1)
            vband = v_ref[0, band_of(t), :]
            outs, lses = [], []
            for h in range(2):
                s = s_ref[2 * u + h] + bias_ref[0, h, variant]
                m = jnp.max(s, axis=-1, keepdims=True)
                p = jnp.exp2(s - m)
                l = jnp.sum(p, axis=-1, keepdims=True)
                o = _dot(p.astype(bf16), vband)
                outs.append(o / l)
                lses.append(jnp.broadcast_to(m + jnp.log(l) * LOG2E, (tq, LANES)))
            o_ref[0, rows_of(t), 0:LANES] = jnp.where(first, outs[0], outs[1])
            o_ref[0, rows_of(t), LANES:2 * LANES] = jnp.where(first, lses[0], lses[1])

    zero = jnp.int32(0)
    scores(zero, sa_ref)

    def body(i2, carry):
        st = 2 * i2
        scores(st + 1, sb_ref)
        finish(st, sa_ref)
        scores(st + 2, sa_ref)
        finish(st + 1, sb_ref)
        return carry

    lax.fori_loop(0, nstage // 2 - 1, body, 0)
    st_end = zero + (nstage - 2)
    scores(st_end + 1, sb_ref)
    finish(st_end, sa_ref)
    finish(st_end + 1, sb_ref)


def _dsa_group(a, bias, *, group):
    b, dil, length, _ = a.shape
    s = dil * length
    pairs = DSA_HEADS_PER_GROUP // 2
    seg_tiles = length // DSA_TQ
    nstage = s // (DSA_TQ * DSA_STAGE_TILES)
    assert seg_tiles >= 2 and length % DSA_TQ == 0 and nstage % 2 == 0
    flat = a.reshape(b, s, DSA_GROUP_COLS)
    nchain = 2 * DSA_STAGE_TILES
    col = lambda part: (lambda bi, hp: (bi, 0, part * pairs + hp))
    out = pl.pallas_call(
        functools.partial(_dsa_kernel, nstage=nstage, seg_tiles=seg_tiles),
        out_shape=jax.ShapeDtypeStruct((b, s, 2 * pairs * LANES), f32),
        grid=(b, pairs),
        in_specs=[
            pl.BlockSpec((1, s, LANES), col(0)),
            pl.BlockSpec((1, s, LANES), col(1)),
            pl.BlockSpec((1, s, LANES), col(2)),
            pl.BlockSpec((1, 2, 2, DSA_TQ, DSA_BAND), lambda bi, hp: (hp, 0, 0, 0, 0)),
        ],
        out_specs=pl.BlockSpec((1, s, 2 * LANES), lambda bi, hp: (bi, 0, hp)),
        scratch_shapes=[pltpu.VMEM((nchain, DSA_TQ, DSA_BAND), f32),
                        pltpu.VMEM((nchain, DSA_TQ, DSA_BAND), f32)],
        compiler_params=pltpu.CompilerParams(
            dimension_semantics=("parallel", "parallel"), vmem_limit_bytes=VMEM_LIMIT),
        name=f"dsa_g{group}",
    )(flat, flat, flat, bias)
    return out.reshape(b, dil, length, 2 * pairs * LANES)


def _moba_kernel(slope_ref, qt_ref, kaug_ref, vaug_ref, ksum_ref, arow_ref, cmask_ref, o_ref,
                 rhs_ref, sa_ref, sb_ref, acc_ref, m_ref, *, nblk):
    hp = pl.program_id(1)
    n2 = pl.program_id(2)
    blk = MOBA_BLOCK
    tile = MOBA_TILE
    ntile = n2 + 1
    chains = [(qb, h) for qb in range(2) for h in range(2)]
    qpair = qt_ref[0]
    rowi = lax.broadcasted_iota(jnp.int32, (HEAD_PAIR, tile), 0)
    bidx = lax.broadcasted_iota(jnp.int32, (nblk, tile), 0)
    own = 2 * n2 + (lax.broadcasted_iota(jnp.int32, (nblk, tile), 1) >= blk).astype(jnp.int32)
    past = bidx < own
    kmean = ksum_ref[0] * (1.0 / blk)
    kmean_hi = kmean.astype(bf16)
    kmean_lo = (kmean - kmean_hi.astype(f32)).astype(bf16)

    for h in range(2):
        mine = (rowi >= HEAD_DIM) if h else (rowi < HEAD_DIM)
        qpad = jnp.where(mine, qpair, jnp.zeros_like(qpair))
        gate = _dot(kmean_hi, qpad) + _dot(kmean_lo, qpad)
        g = jnp.where(past, gate, NEG)
        sel = bidx == own
        for _ in range(MOBA_TOPK):
            mx = jnp.max(g, axis=0, keepdims=True)
            idx = jnp.min(jnp.where(g == mx, bidx, nblk), axis=0, keepdims=True)
            pick = bidx == idx
            sel = sel | (pick & past)
            g = jnp.where(pick, -jnp.inf, g)
        maskbias = jnp.where(sel, 0.0, NEG).astype(bf16)
        for qb in range(2):
            c = 2 * qb + h
            cols = slice(qb * blk, (qb + 1) * blk)
            rhs_ref[c, 0:HEAD_PAIR, :] = qpad[:, cols]
            rhs_ref[c, HEAD_PAIR:HEAD_PAIR + nblk, :] = maskbias[:, cols]
            rhs_ref[c, HEAD_PAIR + nblk:HEAD_PAIR + K_FEAT, :] = arow_ref[0, h]
            acc_ref[c] = jnp.zeros((V_ROWS, blk), f32)
            m_ref[c] = jnp.full((1, blk), M_INIT, f32)

    def scores(i, s_ref):
        k0 = pl.multiple_of(i * tile, tile)
        kt = kaug_ref[0, 0, pl.ds(k0, tile), :]
        for c in range(len(chains)):
            s_ref[c] = _dot(kt, rhs_ref[c])

    def update(i, s_ref, diagonal):
        k0 = pl.multiple_of(i * tile, tile)
        for c, (qb, h) in enumerate(chains):
            s = s_ref[c]
            if diagonal:
                s = s + cmask_ref[qb]
            rel = tile * (i - n2) - blk * qb
            shift = slope_ref[2 * hp + h] * rel.astype(f32)
            m_old = m_ref[c]
            smax = jnp.max(s.reshape(MAX_SLABS, tile // MAX_SLABS, blk), axis=0)
            m_new = jnp.maximum(m_old, jnp.max(smax, axis=0, keepdims=True) + shift)
            p = jnp.exp2(s - (m_new - shift)).astype(bf16)
            pv = _dot(vaug_ref[0, h, :, pl.ds(k0, tile)], p)
            acc_ref[c] = jnp.exp2(m_old - m_new) * acc_ref[c] + pv
            m_ref[c] = m_new

    scores(0, sa_ref)
    npair = (ntile - 1) // 2

    def body(i2, carry):
        t = 2 * i2
        scores(t + 1, sb_ref)
        update(t, sa_ref, False)
        scores(t + 2, sa_ref)
        update(t + 1, sb_ref, False)
        return carry

    lax.fori_loop(0, npair, body, 0)
    t_end = 2 * npair

    @pl.when(ntile - t_end == 1)
    def _():
        update(t_end, sa_ref, True)

    @pl.when(ntile - t_end == 2)
    def _():
        scores(t_end + 1, sb_ref)
        update(t_end, sa_ref, False)
        update(t_end + 1, sb_ref, True)

    for c, (qb, h) in enumerate(chains):
        acc = acc_ref[c]
        o_ref[0, h * HEAD_DIM:(h + 1) * HEAD_DIM, qb * blk:(qb + 1) * blk] = (
            acc[0:HEAD_DIM] / acc[HEAD_DIM:HEAD_DIM + 1]).astype(bf16)


def _moba(slopes, qt, kaug, vaug, ksum, arows, cmask):
    b, _, s = qt.shape
    nblk = s // MOBA_BLOCK
    assert nblk + arows.shape[2] == K_FEAT and nblk % 2 == 0
    pairs = MOBA_HEADS // 2
    nchain = 4
    return pl.pallas_call(
        functools.partial(_moba_kernel, nblk=nblk),
        out_shape=jax.ShapeDtypeStruct((b, MOBA_WIDTH, s), bf16),
        grid=(b, pairs, nblk // 2),
        in_specs=[
            pl.BlockSpec(memory_space=pltpu.SMEM),
            pl.BlockSpec((1, HEAD_PAIR, MOBA_TILE), lambda bi, hp, n2: (bi, hp, n2)),
            pl.BlockSpec((1, 1, s, HEAD_PAIR + K_FEAT), lambda bi, hp, n2: (bi, hp, 0, 0)),
            pl.BlockSpec((1, 2, V_ROWS, s), lambda bi, hp, n2: (bi, hp, 0, 0)),
            pl.BlockSpec((1, nblk, HEAD_PAIR), lambda bi, hp, n2: (bi, 0, hp)),
            pl.BlockSpec((1, 2, K_FEAT - nblk, MOBA_BLOCK), lambda bi, hp, n2: (hp, 0, 0, 0)),
            _resident((2, MOBA_TILE, MOBA_BLOCK)),
        ],
        out_specs=pl.BlockSpec((1, HEAD_PAIR, MOBA_TILE), lambda bi, hp, n2: (bi, hp, n2)),
        scratch_shapes=[
            pltpu.VMEM((nchain, HEAD_PAIR + K_FEAT, MOBA_BLOCK), bf16),
            pltpu.VMEM((nchain, MOBA_TILE, MOBA_BLOCK), f32),
            pltpu.VMEM((nchain, MOBA_TILE, MOBA_BLOCK), f32),
            pltpu.VMEM((nchain, V_ROWS, MOBA_BLOCK), f32),
            pltpu.VMEM((nchain, 1, MOBA_BLOCK), f32),
        ],
        compiler_params=pltpu.CompilerParams(
            dimension_semantics=("parallel", "parallel", "arbitrary"), vmem_limit_bytes=VMEM_LIMIT),
        name="moba",
    )(slopes, qt, kaug, vaug, ksum.reshape(b, nblk, MOBA_WIDTH), arows, cmask)


def _merge_kernel(x_ref, gates_ref, d0_ref, d1_ref, d2_ref, ybt_ref, wua_ref, wub_ref, wo_ref, o_ref,
                  nat_ref, *, tm):
    def token_order(g, c):
        d_ref, dil = (d0_ref, d1_ref, d2_ref)[g], DSA_GROUPS[g][1]
        if dil == 1:
            return d_ref[0, 0, :, c * LANES:(c + 1) * LANES]
        for r in range(dil):
            nat_ref[g, c, pl.ds(r, tm // dil, stride=dil), :] = d_ref[0, r, :, c * LANES:(c + 1) * LANES]
        return nat_ref[g, c]

    pairs = []
    for hp in range(2):
        outs = [token_order(g, 2 * hp) for g in range(len(DSA_GROUPS))]
        lses = [token_order(g, 2 * hp + 1) for g in range(len(DSA_GROUPS))]
        mx = jnp.maximum(jnp.maximum(lses[0], lses[1]), lses[2])
        es = [jnp.exp2(l - mx) for l in lses]
        den = es[0] + es[1] + es[2]
        pairs.append((es[0] * outs[0] + es[1] * outs[1] + es[2] * outs[2]) / den)
    ya = jnp.concatenate(pairs, axis=1).astype(bf16)
    ta = _dot(ya, wua_ref[...])
    tb = _dot_tn(ybt_ref[0], wub_ref[...])
    gates = gates_ref[0]
    merged = gates[:, 0:D_MODEL].astype(f32) * ta + gates[:, D_MODEL:].astype(f32) * tb
    o_ref[0] = x_ref[0] + _dot(merged.astype(bf16), wo_ref[...])


def _merge(x3d, gates, dsa_outs, ybt, wua, wub, wo, *, tm):
    b, s, _ = x3d.shape
    tok = lambda w: pl.BlockSpec((1, tm, w), lambda bi, i: (bi, i, 0))
    grp = lambda dil: pl.BlockSpec((1, dil, tm // dil, 4 * LANES), lambda bi, i: (bi, 0, i, 0))
    return pl.pallas_call(
        functools.partial(_merge_kernel, tm=tm),
        out_shape=jax.ShapeDtypeStruct((b, s, D_MODEL), f32),
        grid=(b, s // tm),
        in_specs=[
            tok(D_MODEL), tok(2 * D_MODEL), *[grp(dil) for _, dil in DSA_GROUPS],
            pl.BlockSpec((1, MOBA_WIDTH, tm), lambda bi, i: (bi, 0, i)),
            _resident((DSA_OUT, D_MODEL)),
            _resident((MOBA_WIDTH, D_MODEL)),
            _resident((D_MODEL, D_MODEL)),
        ],
        out_specs=tok(D_MODEL),
        scratch_shapes=[pltpu.VMEM((len(DSA_GROUPS), 4, tm, LANES), f32)],
        compiler_params=pltpu.CompilerParams(
            dimension_semantics=("parallel", "parallel"), vmem_limit_bytes=VMEM_LIMIT),
        name="merge",
    )(x3d, gates, *dsa_outs, ybt, wua, wub, wo)


def _alibi_slopes(n):
    return 2.0 ** (-8.0 * np.arange(1, n + 1, dtype=np.float64) / n)


def _split3(x):
    hi = x.astype(bf16)
    r1 = x - hi.astype(f32)
    mid = r1.astype(bf16)
    lo = (r1 - mid.astype(f32)).astype(bf16)
    return hi, mid, lo


def _moba_constants(s):
    nblk = s // MOBA_BLOCK
    n_arow = K_FEAT - nblk
    pos = np.arange(s)
    kfeat = np.zeros((s, K_FEAT), np.float32)
    kfeat[pos, pos // MOBA_BLOCK] = 1.0
    kfeat[:, nblk:nblk + 3] = 1.0
    kfeat[:, nblk + 3:nblk + 6] = (pos % MOBA_BLOCK)[:, None]
    kfeat[:, nblk + 6:nblk + 9] = ((pos // MOBA_BLOCK) % 2)[:, None]
    slopes = jnp.asarray(_alibi_slopes(MOBA_HEADS) * LOG2E, f32)
    off = jnp.arange(MOBA_BLOCK, dtype=f32)
    wide = lambda v: jnp.broadcast_to(v[:, None], (MOBA_HEADS, MOBA_BLOCK))
    qterm = _split3(-slopes[:, None] * off[None, :])
    sterm = _split3(wide(slopes))
    bterm = _split3(wide(slopes * MOBA_BLOCK))
    rows = jnp.stack(list(qterm) + list(sterm) + list(bterm), axis=1)
    rows = jnp.concatenate(
        [rows, jnp.zeros((MOBA_HEADS, n_arow - 9, MOBA_BLOCK), bf16)], axis=1)
    arows = rows.reshape(MOBA_HEADS // 2, 2, n_arow, MOBA_BLOCK)
    r = jnp.arange(MOBA_TILE)[None, :, None]
    c = jnp.arange(MOBA_BLOCK)[None, None, :] + MOBA_BLOCK * jnp.arange(2)[:, None, None]
    cmask = jnp.where(r <= c, 0.0, NEG).astype(f32)
    return jnp.asarray(kfeat, bf16), arows, slopes, cmask


def _dsa_bias(group, dilation):
    slopes = _alibi_slopes(DSA_HEADS)[group * DSA_HEADS_PER_GROUP:(group + 1) * DSA_HEADS_PER_GROUP]
    slopes = jnp.asarray(slopes * LOG2E * dilation, f32).reshape(2, 2, 1, 1, 1)
    qi = jnp.arange(DSA_TQ)[:, None]
    kj = jnp.arange(DSA_BAND)[None, :]
    delta = jnp.stack([qi - kj, qi + DSA_BLK - kj])[None, None]
    valid = (delta >= 0) & (delta <= DSA_BLK)
    return jnp.where(valid, -slopes * delta.astype(f32), NEG)


def _layer(x, norm_ffn1, ffn1_gate, ffn1_up, ffn1_down, norm_mix, w_in, w_up_a, w_up_b, w_out,
           norm_ffn2, ffn2_gate, ffn2_up, ffn2_down, gain_final, *, final_norm):
    b, s, _ = x.shape
    tm = 512
    tm_ffn = 512
    row = lambda g: g.reshape(1, D_MODEL).astype(f32)
    x1 = _ffn(x.reshape(b * s, D_MODEL), row(norm_ffn1), ffn1_gate.astype(bf16), ffn1_up.astype(bf16),
              ffn1_down.astype(bf16), row(gain_final), final_norm=False, tm=tm_ffn).reshape(b, s, D_MODEL)

    scale = HEAD_DIM ** -0.5
    o = 0
    parts = []
    for width in (DSA_WIDTH, DSA_WIDTH, DSA_WIDTH, MOBA_WIDTH, MOBA_WIDTH, MOBA_WIDTH, 2 * D_MODEL):
        parts.append(w_in[:, o:o + width])
        o += width
    wqa, wka, wva, wqb, wkb, wvb, wgate = parts
    qscale = scale * LOG2E
    grp = lambda w, g: w[:, g * DSA_OUT:(g + 1) * DSA_OUT]
    wa = jnp.concatenate(
        [jnp.concatenate([grp(wqa, g) * qscale, grp(wka, g), grp(wva, g)], axis=1)
         for g in range(len(DSA_GROUPS))], axis=1).astype(bf16)
    wqvt = jnp.concatenate([wqb * qscale, wvb], axis=1).T.astype(bf16)
    kfeat, arows, slopes_b, cmask = _moba_constants(s)
    vpad = jnp.zeros((V_ROWS - HEAD_DIM, tm), bf16).at[0].set(1.0)
    a0, a1, a2, kaug, qt, vaug, ksum, gates = _proj(
        x1, row(norm_mix), wa, wkb.astype(bf16), wqvt, wgate.astype(bf16), kfeat, vpad, tm=tm)

    dsa_outs = [
        _dsa_group(a, _dsa_bias(gi, dil), group=gi)
        for gi, (a, (_, dil)) in enumerate(zip((a0, a1, a2), DSA_GROUPS))
    ]
    ybt = _moba(slopes_b, qt, kaug, vaug, ksum, arows, cmask)
    x2 = _merge(x1, gates, dsa_outs, ybt, w_up_a.astype(bf16), w_up_b.astype(bf16),
                w_out.astype(bf16), tm=tm)
    x3 = _ffn(x2.reshape(b * s, D_MODEL), row(norm_ffn2), ffn2_gate.astype(bf16), ffn2_up.astype(bf16),
              ffn2_down.astype(bf16), row(gain_final), final_norm=final_norm, tm=tm_ffn)
    return x3.reshape(b, s, D_MODEL)


def kernel(x, norm_ffn1, ffn1_gate, ffn1_up, ffn1_down, norm_mix, w_in, w_up_a, w_up_b, w_out,
           norm_ffn2, ffn2_gate, ffn2_up, ffn2_down, norm_final):
    depth = norm_ffn1.shape[0]
    for layer in range(depth):
        last = layer == depth - 1
        x = _layer(x, norm_ffn1[layer], ffn1_gate[layer], ffn1_up[layer], ffn1_down[layer],
                   norm_mix[layer], w_in[layer], w_up_a[layer], w_up_b[layer], w_out[layer],
                   norm_ffn2[layer], ffn2_gate[layer], ffn2_up[layer], ffn2_down[layer],
                   norm_final, final_norm=last)
    return x
```

```python
import functools

import numpy as np
import jax
import jax.numpy as jnp
from jax import lax
from jax.experimental import pallas as pl
from jax.experimental.pallas import tpu as pltpu

D_MODEL = 1024
HEAD_DIM = 64
DSA_GROUPS = ((128, 1), (512, 4), (2048, 16))
DSA_HEADS_PER_GROUP = 4
DSA_HEADS = DSA_HEADS_PER_GROUP * len(DSA_GROUPS)
MOBA_HEADS = 8
MOBA_BLOCK = 256
MOBA_TOPK = 3
D_FF = ((8 * D_MODEL // 3 + 127) // 128) * 128
DSA_WIDTH = DSA_HEADS * HEAD_DIM
MOBA_WIDTH = MOBA_HEADS * HEAD_DIM
DSA_OUT = DSA_HEADS_PER_GROUP * HEAD_DIM
EPS = 1e-6
NEG = -1e30

LANES = 128
HEAD_PAIR = 2 * HEAD_DIM
DSA_BLK = 128
DSA_TQ = DSA_BLK
DSA_BAND = DSA_TQ + DSA_BLK
DSA_STAGE_TILES = 2
DSA_GROUP_COLS = 3 * DSA_OUT
MOBA_TILE = 2 * MOBA_BLOCK
MAX_SLABS = 8
MOBA_STEP_PAIRS = 2
LOG2E = 1.4426950408889634
M_INIT = -3.0e38
V_ROWS = 80
K_FEAT = 128
VMEM_LIMIT = 56 * 1024 * 1024
ROW_GROUPS = 2

f32 = jnp.float32
bf16 = jnp.bfloat16


def _rmsnorm(x, g):
    return x * lax.rsqrt(jnp.mean(x * x, axis=-1, keepdims=True) + EPS) * g


def _dot(a, b):
    return jnp.dot(a, b, preferred_element_type=f32)


def _dot_nt(a, b):
    return lax.dot_general(a, b, (((1,), (1,)), ((), ())), preferred_element_type=f32)


def _dot_tn(a, b):
    return lax.dot_general(a, b, (((0,), (0,)), ((), ())), preferred_element_type=f32)


def _resident(shape):
    nd = len(shape)
    return pl.BlockSpec(shape, lambda *_: (0,) * nd, pipeline_mode=pl.Buffered(1))


def _ffn_kernel(x_ref, g_ref, wg_ref, wu_ref, wd_ref, gf_ref, o_ref, *, final_norm):
    rows = x_ref.shape[0] // ROW_GROUPS
    for part in range(ROW_GROUPS):
        sl = slice(part * rows, (part + 1) * rows)
        x = x_ref[sl, :]
        h = _rmsnorm(x, g_ref[...]).astype(bf16)
        gate = _dot(h, wg_ref[...])
        up = _dot(h, wu_ref[...])
        act = (gate * jax.nn.sigmoid(gate) * up).astype(bf16)
        y = x + 0.5 * _dot(act, wd_ref[...])
        if final_norm:
            y = _rmsnorm(y, gf_ref[...])
        o_ref[sl, :] = y


def _ffn(x2d, gain, w_gate, w_up, w_down, gain_final, *, final_norm, tm):
    n = x2d.shape[0]
    return pl.pallas_call(
        functools.partial(_ffn_kernel, final_norm=final_norm),
        out_shape=jax.ShapeDtypeStruct((n, D_MODEL), f32),
        grid=(n // tm,),
        in_specs=[
            pl.BlockSpec((tm, D_MODEL), lambda i: (i, 0)),
            _resident((1, D_MODEL)),
            _resident((D_MODEL, D_FF)),
            _resident((D_MODEL, D_FF)),
            _resident((D_FF, D_MODEL)),
            _resident((1, D_MODEL)),
        ],
        out_specs=pl.BlockSpec((tm, D_MODEL), lambda i: (i, 0)),
        compiler_params=pltpu.CompilerParams(
            dimension_semantics=("parallel",), vmem_limit_bytes=VMEM_LIMIT),
        name="ffn_final" if final_norm else "ffn",
    )(x2d, gain, w_gate, w_up, w_down, gain_final)


def _proj_kernel(x_ref, g_ref, wa_ref, wkb_ref, wqvt_ref, wgate_ref, kfeat_ref, vpad_ref,
                 a0_ref, a1_ref, a2_ref, kaug_ref, qt_ref, vaug_ref, ksum_ref, gates_ref, scr_ref, *, tm):
    h = _rmsnorm(x_ref[0], g_ref[...]).astype(bf16)
    qkv = _dot(h, wa_ref[...])
    tiles_per_group = DSA_GROUP_COLS // LANES
    for g, (a_ref, (_, dil)) in enumerate(zip((a0_ref, a1_ref, a2_ref), DSA_GROUPS)):
        for c in range(tiles_per_group):
            lo = g * DSA_GROUP_COLS + c * LANES
            if dil == 1:
                a_ref[0, 0, :, c * LANES:(c + 1) * LANES] = qkv[:, lo:lo + LANES].astype(bf16)
                continue
            scr_ref[g, c] = qkv[:, lo:lo + LANES]
            for r in range(dil):
                a_ref[0, r, :, c * LANES:(c + 1) * LANES] = (
                    scr_ref[g, c, pl.ds(r, tm // dil, stride=dil), :].astype(bf16))
    kb = _dot(h, wkb_ref[...])
    ksum_ref[...] = jnp.sum(kb.reshape(tm // MOBA_BLOCK, MOBA_BLOCK, MOBA_WIDTH), axis=1)[:, None, :]
    kb = kb.astype(bf16)
    for p in range(MOBA_HEADS // 2):
        kaug_ref[0, p, :, 0:HEAD_PAIR] = kb[:, p * HEAD_PAIR:(p + 1) * HEAD_PAIR]
        kaug_ref[0, p, :, HEAD_PAIR:HEAD_PAIR + K_FEAT] = kfeat_ref[...]
    qv = _dot_nt(wqvt_ref[...], h)
    qt_ref[0] = qv[0:MOBA_WIDTH].astype(bf16)
    for hd in range(MOBA_HEADS):
        lo = MOBA_WIDTH + hd * HEAD_DIM
        vaug_ref[0, hd, 0:HEAD_DIM, :] = qv[lo:lo + HEAD_DIM].astype(bf16)
        vaug_ref[0, hd, HEAD_DIM:V_ROWS, :] = vpad_ref[...]
    gates_ref[0] = jax.nn.sigmoid(_dot(h, wgate_ref[...])).astype(bf16)


def _proj(x3d, gain, wa, wkb, wqvt, wgate, kfeat, vpad, *, tm):
    b, s, _ = x3d.shape
    nblk = s // MOBA_BLOCK
    grp_shapes = tuple(
        jax.ShapeDtypeStruct((b, dil, s // dil, DSA_GROUP_COLS), bf16) for _, dil in DSA_GROUPS)
    grp_specs = tuple(
        pl.BlockSpec((1, dil, tm // dil, DSA_GROUP_COLS), lambda bi, i: (bi, 0, i, 0))
        for _, dil in DSA_GROUPS)
    out_shape = grp_shapes + (
        jax.ShapeDtypeStruct((b, MOBA_HEADS // 2, s, HEAD_PAIR + K_FEAT), bf16),
        jax.ShapeDtypeStruct((b, MOBA_WIDTH, s), bf16),
        jax.ShapeDtypeStruct((b, MOBA_HEADS, V_ROWS, s), bf16),
        jax.ShapeDtypeStruct((b * nblk, 1, MOBA_WIDTH), f32),
        jax.ShapeDtypeStruct((b, s, 2 * D_MODEL), bf16),
    )
    per = tm // MOBA_BLOCK
    return pl.pallas_call(
        functools.partial(_proj_kernel, tm=tm),
        out_shape=out_shape,
        grid=(b, s // tm),
        in_specs=[
            pl.BlockSpec((1, tm, D_MODEL), lambda bi, i: (bi, i, 0)),
            _resident((1, D_MODEL)),
            _resident((D_MODEL, 3 * DSA_WIDTH)),
            _resident((D_MODEL, MOBA_WIDTH)),
            _resident((2 * MOBA_WIDTH, D_MODEL)),
            _resident((D_MODEL, 2 * D_MODEL)),
            pl.BlockSpec((tm, K_FEAT), lambda bi, i: (i, 0)),
            _resident((V_ROWS - HEAD_DIM, tm)),
        ],
        out_specs=grp_specs + (
            pl.BlockSpec((1, MOBA_HEADS // 2, tm, HEAD_PAIR + K_FEAT), lambda bi, i: (bi, 0, i, 0)),
            pl.BlockSpec((1, MOBA_WIDTH, tm), lambda bi, i: (bi, 0, i)),
            pl.BlockSpec((1, MOBA_HEADS, V_ROWS, tm), lambda bi, i: (bi, 0, 0, i)),
            pl.BlockSpec((per, 1, MOBA_WIDTH), lambda bi, i: (bi * (s // tm) + i, 0, 0)),
            pl.BlockSpec((1, tm, 2 * D_MODEL), lambda bi, i: (bi, i, 0)),
        ),
        scratch_shapes=[pltpu.VMEM((len(DSA_GROUPS), DSA_GROUP_COLS // LANES, tm, LANES), f32)],
        compiler_params=pltpu.CompilerParams(
            dimension_semantics=("parallel", "parallel"), vmem_limit_bytes=VMEM_LIMIT),
        name="proj",
    )(x3d, gain, wa, wkb, wqvt, wgate, kfeat, vpad)


def _dsa_kernel(q_ref, k_ref, v_ref, bias_ref, o_ref, sa_ref, sb_ref, *, nstage, seg_tiles):
    tq, band = DSA_TQ, DSA_BAND
    first = lax.broadcasted_iota(jnp.int32, (tq, LANES), 1) < HEAD_DIM
    chains = [(u, h) for u in range(DSA_STAGE_TILES) for h in range(2)]

    def rows_of(t):
        return pl.ds(pl.multiple_of(t * tq, tq), tq)

    def seg_start(t):
        return (t % seg_tiles) == 0

    def band_of(t):
        start = jnp.where(seg_start(t), t * tq, t * tq - DSA_BLK)
        return pl.ds(pl.multiple_of(start, DSA_BLK), band)

    def scores(stage, s_ref):
        for u in range(DSA_STAGE_TILES):
            t = stage * DSA_STAGE_TILES + u
            q = q_ref[0, rows_of(t), :]
            kband = k_ref[0, band_of(t), :]
            for h in range(2):
                qm = jnp.where(first if h == 0 else jnp.logical_not(first), q, jnp.zeros_like(q))
                s_ref[2 * u + h] = _dot_nt(qm, kband)

    def finish(stage, s_ref):
        for u in range(DSA_STAGE_TILES):
            t = stage * DSA_STAGE_TILES + u
            variant = jnp.where(seg_start(t), 0, 1)
            vband = v_ref[0, band_of(t), :]
            outs, lses = [], []
            for h in range(2):
                s = s_ref[2 * u + h] + bias_ref[0, h, variant]
                m = jnp.max(s, axis=-1, keepdims=True)
                p = jnp.exp2(s - m)
                l = jnp.sum(p, axis=-1, keepdims=True)
                o = _dot(p.astype(bf16), vband)
                outs.append(o / l)
                lses.append(jnp.broadcast_to(m + jnp.log(l) * LOG2E, (tq, LANES)))
            o_ref[0, rows_of(t), 0:LANES] = jnp.where(first, outs[0], outs[1])
            o_ref[0, rows_of(t), LANES:2 * LANES] = jnp.where(first, lses[0], lses[1])

    zero = jnp.int32(0)
    scores(zero, sa_ref)

    def body(i2, carry):
        st = 2 * i2
        scores(st + 1, sb_ref)
        finish(st, sa_ref)
        scores(st + 2, sa_ref)
        finish(st + 1, sb_ref)
        return carry

    lax.fori_loop(0, nstage // 2 - 1, body, 0)
    st_end = zero + (nstage - 2)
    scores(st_end + 1, sb_ref)
    finish(st_end, sa_ref)
    finish(st_end + 1, sb_ref)


def _dsa_group(a, bias, *, group):
    b, dil, length, _ = a.shape
    s = dil * length
    pairs = DSA_HEADS_PER_GROUP // 2
    seg_tiles = length // DSA_TQ
    nstage = s // (DSA_TQ * DSA_STAGE_TILES)
    assert seg_tiles >= 2 and length % DSA_TQ == 0 and nstage % 2 == 0
    flat = a.reshape(b, s, DSA_GROUP_COLS)
    nchain = 2 * DSA_STAGE_TILES
    col = lambda part: (lambda bi, hp: (bi, 0, part * pairs + hp))
    out = pl.pallas_call(
        functools.partial(_dsa_kernel, nstage=nstage, seg_tiles=seg_tiles),
        out_shape=jax.ShapeDtypeStruct((b, s, 2 * pairs * LANES), f32),
        grid=(b, pairs),
        in_specs=[
            pl.BlockSpec((1, s, LANES), col(0)),
            pl.BlockSpec((1, s, LANES), col(1)),
            pl.BlockSpec((1, s, LANES), col(2)),
            pl.BlockSpec((1, 2, 2, DSA_TQ, DSA_BAND), lambda bi, hp: (hp, 0, 0, 0, 0)),
        ],
        out_specs=pl.BlockSpec((1, s, 2 * LANES), lambda bi, hp: (bi, 0, hp)),
        scratch_shapes=[pltpu.VMEM((nchain, DSA_TQ, DSA_BAND), f32),
                        pltpu.VMEM((nchain, DSA_TQ, DSA_BAND), f32)],
        compiler_params=pltpu.CompilerParams(
            dimension_semantics=("parallel", "parallel"), vmem_limit_bytes=VMEM_LIMIT),
        name=f"dsa_g{group}",
    )(flat, flat, flat, bias)
    return out.reshape(b, dil, length, 2 * pairs * LANES)


def _moba_kernel(slope_ref, qt_ref, kaug_ref, vaug_ref, ksum_ref, arow_ref, cmask_ref, o_ref,
                 rhs_ref, sa_ref, sb_ref, acc_ref, m_ref, *, nblk):
    hp0 = pl.program_id(1) * MOBA_STEP_PAIRS
    n2 = pl.program_id(2)
    blk = MOBA_BLOCK
    tile = MOBA_TILE
    ntile = n2 + 1
    chains = [(pp, qb, h) for pp in range(MOBA_STEP_PAIRS) for qb in range(2) for h in range(2)]
    slot = {chain: c for c, chain in enumerate(chains)}
    rowi = lax.broadcasted_iota(jnp.int32, (HEAD_PAIR, tile), 0)
    bidx = lax.broadcasted_iota(jnp.int32, (nblk, tile), 0)
    own = 2 * n2 + (lax.broadcasted_iota(jnp.int32, (nblk, tile), 1) >= blk).astype(jnp.int32)
    past = bidx < own

    gates = {}
    for pp in range(MOBA_STEP_PAIRS):
        qpair = qt_ref[0, pp * HEAD_PAIR:(pp + 1) * HEAD_PAIR, :]
        kmean = ksum_ref[0, :, pp * HEAD_PAIR:(pp + 1) * HEAD_PAIR] * (1.0 / blk)
        kmean_hi = kmean.astype(bf16)
        kmean_lo = (kmean - kmean_hi.astype(f32)).astype(bf16)
        for h in range(2):
            mine = (rowi >= HEAD_DIM) if h else (rowi < HEAD_DIM)
            qpad = jnp.where(mine, qpair, jnp.zeros_like(qpair))
            gates[pp, h] = _dot(kmean_hi, qpad) + _dot(kmean_lo, qpad)
            for qb in range(2):
                c = slot[pp, qb, h]
                rhs_ref[c, 0:HEAD_PAIR, :] = qpad[:, qb * blk:(qb + 1) * blk]
                rhs_ref[c, HEAD_PAIR:HEAD_PAIR + nblk, :] = jnp.zeros((nblk, blk), bf16)
                rhs_ref[c, HEAD_PAIR + nblk:HEAD_PAIR + K_FEAT, :] = arow_ref[pp, h]
                acc_ref[c] = jnp.zeros((V_ROWS, blk), f32)
                m_ref[c] = jnp.full((1, blk), M_INIT, f32)

    def scores(i, s_ref):
        k0 = pl.multiple_of(i * tile, tile)
        for pp in range(MOBA_STEP_PAIRS):
            kt = kaug_ref[0, pp, pl.ds(k0, tile), :]
            for qb in range(2):
                for h in range(2):
                    c = slot[pp, qb, h]
                    s_ref[c] = _dot(kt, rhs_ref[c])

    scores(0, sa_ref)

    for (pp, h), gate in gates.items():
        g = jnp.where(past, gate, NEG)
        sel = bidx == own
        for _ in range(MOBA_TOPK):
            mx = jnp.max(g, axis=0, keepdims=True)
            idx = jnp.min(jnp.where(g == mx, bidx, nblk), axis=0, keepdims=True)
            pick = bidx == idx
            sel = sel | (pick & past)
            g = jnp.where(pick, -jnp.inf, g)
        maskbias = jnp.where(sel, 0.0, NEG).astype(bf16)
        for qb in range(2):
            c = slot[pp, qb, h]
            mb = maskbias[:, qb * blk:(qb + 1) * blk]
            rhs_ref[c, HEAD_PAIR:HEAD_PAIR + nblk, :] = mb
            for kb in range(tile // blk):
                sa_ref[c, kb * blk:(kb + 1) * blk, :] += mb[kb:kb + 1, :].astype(f32)

    def update(i, s_ref, diagonal):
        k0 = pl.multiple_of(i * tile, tile)
        for c, (pp, qb, h) in enumerate(chains):
            s = s_ref[c]
            if diagonal:
                s = s + cmask_ref[qb]
            rel = tile * (i - n2) - blk * qb
            shift = slope_ref[2 * (hp0 + pp) + h] * rel.astype(f32)
            m_old = m_ref[c]
            smax = jnp.max(s.reshape(MAX_SLABS, tile // MAX_SLABS, blk), axis=0)
            m_new = jnp.maximum(m_old, jnp.max(smax, axis=0, keepdims=True) + shift)
            p = jnp.exp2(s - (m_new - shift)).astype(bf16)
            pv = _dot(vaug_ref[0, 2 * pp + h, :, pl.ds(k0, tile)], p)
            acc_ref[c] = jnp.exp2(m_old - m_new) * acc_ref[c] + pv
            m_ref[c] = m_new

    npair = (ntile - 1) // 2

    def body(i2, carry):
        t = 2 * i2
        scores(t + 1, sb_ref)
        update(t, sa_ref, False)
        scores(t + 2, sa_ref)
        update(t + 1, sb_ref, False)
        return carry

    lax.fori_loop(0, npair, body, 0)
    t_end = 2 * npair

    @pl.when(ntile - t_end == 1)
    def _():
        update(t_end, sa_ref, True)

    @pl.when(ntile - t_end == 2)
    def _():
        scores(t_end + 1, sb_ref)
        update(t_end, sa_ref, False)
        update(t_end + 1, sb_ref, True)

    for c, (pp, qb, h) in enumerate(chains):
        acc = acc_ref[c]
        row0 = pp * HEAD_PAIR + h * HEAD_DIM
        o_ref[0, row0:row0 + HEAD_DIM, qb * blk:(qb + 1) * blk] = (
            acc[0:HEAD_DIM] / acc[HEAD_DIM:HEAD_DIM + 1]).astype(bf16)


def _moba(slopes, qt, kaug, vaug, ksum, arows, cmask):
    b, _, s = qt.shape
    nblk = s // MOBA_BLOCK
    pairs = MOBA_HEADS // 2
    assert nblk + arows.shape[2] == K_FEAT and nblk % 2 == 0 and pairs % MOBA_STEP_PAIRS == 0
    npp = MOBA_STEP_PAIRS
    nchain = 4 * npp
    return pl.pallas_call(
        functools.partial(_moba_kernel, nblk=nblk),
        out_shape=jax.ShapeDtypeStruct((b, MOBA_WIDTH, s), bf16),
        grid=(b, pairs // npp, nblk // 2),
        in_specs=[
            pl.BlockSpec(memory_space=pltpu.SMEM),
            pl.BlockSpec((1, npp * HEAD_PAIR, MOBA_TILE), lambda bi, hp, n2: (bi, hp, n2)),
            pl.BlockSpec((1, npp, s, HEAD_PAIR + K_FEAT), lambda bi, hp, n2: (bi, hp, 0, 0)),
            pl.BlockSpec((1, 2 * npp, V_ROWS, s), lambda bi, hp, n2: (bi, hp, 0, 0)),
            pl.BlockSpec((1, nblk, npp * HEAD_PAIR), lambda bi, hp, n2: (bi, 0, hp)),
            pl.BlockSpec((npp, 2, K_FEAT - nblk, MOBA_BLOCK), lambda bi, hp, n2: (hp, 0, 0, 0)),
            _resident((2, MOBA_TILE, MOBA_BLOCK)),
        ],
        out_specs=pl.BlockSpec((1, npp * HEAD_PAIR, MOBA_TILE), lambda bi, hp, n2: (bi, hp, n2)),
        scratch_shapes=[
            pltpu.VMEM((nchain, HEAD_PAIR + K_FEAT, MOBA_BLOCK), bf16),
            pltpu.VMEM((nchain, MOBA_TILE, MOBA_BLOCK), f32),
            pltpu.VMEM((nchain, MOBA_TILE, MOBA_BLOCK), f32),
            pltpu.VMEM((nchain, V_ROWS, MOBA_BLOCK), f32),
            pltpu.VMEM((nchain, 1, MOBA_BLOCK), f32),
        ],
        compiler_params=pltpu.CompilerParams(
            dimension_semantics=("parallel", "parallel", "arbitrary"), vmem_limit_bytes=VMEM_LIMIT),
        name="moba",
    )(slopes, qt, kaug, vaug, ksum.reshape(b, nblk, MOBA_WIDTH), arows, cmask)


def _merge_kernel(x_ref, gates_ref, d0_ref, d1_ref, d2_ref, ybt_ref, wua_ref, wub_ref, wo_ref, o_ref,
                  nat_ref, *, tm):
    def token_order(g, c):
        d_ref, dil = (d0_ref, d1_ref, d2_ref)[g], DSA_GROUPS[g][1]
        if dil == 1:
            return d_ref[0, 0, :, c * LANES:(c + 1) * LANES]
        for r in range(dil):
            nat_ref[g, c, pl.ds(r, tm // dil, stride=dil), :] = d_ref[0, r, :, c * LANES:(c + 1) * LANES]
        return nat_ref[g, c]

    pairs = []
    for hp in range(2):
        outs = [token_order(g, 2 * hp) for g in range(len(DSA_GROUPS))]
        lses = [token_order(g, 2 * hp + 1) for g in range(len(DSA_GROUPS))]
        mx = jnp.maximum(jnp.maximum(lses[0], lses[1]), lses[2])
        es = [jnp.exp2(l - mx) for l in lses]
        den = es[0] + es[1] + es[2]
        pairs.append((es[0] * outs[0] + es[1] * outs[1] + es[2] * outs[2]) / den)
    ya = jnp.concatenate(pairs, axis=1).astype(bf16)
    ta = _dot(ya, wua_ref[...])
    tb = _dot_tn(ybt_ref[0], wub_ref[...])
    gates = gates_ref[0]
    merged = gates[:, 0:D_MODEL].astype(f32) * ta + gates[:, D_MODEL:].astype(f32) * tb
    o_ref[0] = x_ref[0] + _dot(merged.astype(bf16), wo_ref[...])


def _merge(x3d, gates, dsa_outs, ybt, wua, wub, wo, *, tm):
    b, s, _ = x3d.shape
    tok = lambda w: pl.BlockSpec((1, tm, w), lambda bi, i: (bi, i, 0))
    grp = lambda dil: pl.BlockSpec((1, dil, tm // dil, 4 * LANES), lambda bi, i: (bi, 0, i, 0))
    return pl.pallas_call(
        functools.partial(_merge_kernel, tm=tm),
        out_shape=jax.ShapeDtypeStruct((b, s, D_MODEL), f32),
        grid=(b, s // tm),
        in_specs=[
            tok(D_MODEL), tok(2 * D_MODEL), *[grp(dil) for _, dil in DSA_GROUPS],
            pl.BlockSpec((1, MOBA_WIDTH, tm), lambda bi, i: (bi, 0, i)),
            _resident((DSA_OUT, D_MODEL)),
            _resident((MOBA_WIDTH, D_MODEL)),
            _resident((D_MODEL, D_MODEL)),
        ],
        out_specs=tok(D_MODEL),
        scratch_shapes=[pltpu.VMEM((len(DSA_GROUPS), 4, tm, LANES), f32)],
        compiler_params=pltpu.CompilerParams(
            dimension_semantics=("parallel", "parallel"), vmem_limit_bytes=VMEM_LIMIT),
        name="merge",
    )(x3d, gates, *dsa_outs, ybt, wua, wub, wo)


def _alibi_slopes(n):
    return 2.0 ** (-8.0 * np.arange(1, n + 1, dtype=np.float64) / n)


def _split3(x):
    hi = x.astype(bf16)
    r1 = x - hi.astype(f32)
    mid = r1.astype(bf16)
    lo = (r1 - mid.astype(f32)).astype(bf16)
    return hi, mid, lo


def _moba_constants(s):
    nblk = s // MOBA_BLOCK
    n_arow = K_FEAT - nblk
    pos = np.arange(s)
    kfeat = np.zeros((s, K_FEAT), np.float32)
    kfeat[pos, pos // MOBA_BLOCK] = 1.0
    kfeat[:, nblk:nblk + 3] = 1.0
    kfeat[:, nblk + 3:nblk + 6] = (pos % MOBA_BLOCK)[:, None]
    kfeat[:, nblk + 6:nblk + 9] = ((pos // MOBA_BLOCK) % 2)[:, None]
    slopes = jnp.asarray(_alibi_slopes(MOBA_HEADS) * LOG2E, f32)
    off = jnp.arange(MOBA_BLOCK, dtype=f32)
    wide = lambda v: jnp.broadcast_to(v[:, None], (MOBA_HEADS, MOBA_BLOCK))
    qterm = _split3(-slopes[:, None] * off[None, :])
    sterm = _split3(wide(slopes))
    bterm = _split3(wide(slopes * MOBA_BLOCK))
    rows = jnp.stack(list(qterm) + list(sterm) + list(bterm), axis=1)
    rows = jnp.concatenate(
        [rows, jnp.zeros((MOBA_HEADS, n_arow - 9, MOBA_BLOCK), bf16)], axis=1)
    arows = rows.reshape(MOBA_HEADS // 2, 2, n_arow, MOBA_BLOCK)
    r = jnp.arange(MOBA_TILE)[None, :, None]
    c = jnp.arange(MOBA_BLOCK)[None, None, :] + MOBA_BLOCK * jnp.arange(2)[:, None, None]
    cmask = jnp.where(r <= c, 0.0, NEG).astype(f32)
    return jnp.asarray(kfeat, bf16), arows, slopes, cmask


def _dsa_bias(group, dilation):
    slopes = _alibi_slopes(DSA_HEADS)[group * DSA_HEADS_PER_GROUP:(group + 1) * DSA_HEADS_PER_GROUP]
    slopes = jnp.asarray(slopes * LOG2E * dilation, f32).reshape(2, 2, 1, 1, 1)
    qi = jnp.arange(DSA_TQ)[:, None]
    kj = jnp.arange(DSA_BAND)[None, :]
    delta = jnp.stack([qi - kj, qi + DSA_BLK - kj])[None, None]
    valid = (delta >= 0) & (delta <= DSA_BLK)
    return jnp.where(valid, -slopes * delta.astype(f32), NEG)


def _layer(x, norm_ffn1, ffn1_gate, ffn1_up, ffn1_down, norm_mix, w_in, w_up_a, w_up_b, w_out,
           norm_ffn2, ffn2_gate, ffn2_up, ffn2_down, gain_final, *, final_norm):
    b, s, _ = x.shape
    tm = 512
    tm_ffn = 512
    row = lambda g: g.reshape(1, D_MODEL).astype(f32)
    x1 = _ffn(x.reshape(b * s, D_MODEL), row(norm_ffn1), ffn1_gate.astype(bf16), ffn1_up.astype(bf16),
              ffn1_down.astype(bf16), row(gain_final), final_norm=False, tm=tm_ffn).reshape(b, s, D_MODEL)

    scale = HEAD_DIM ** -0.5
    o = 0
    parts = []
    for width in (DSA_WIDTH, DSA_WIDTH, DSA_WIDTH, MOBA_WIDTH, MOBA_WIDTH, MOBA_WIDTH, 2 * D_MODEL):
        parts.append(w_in[:, o:o + width])
        o += width
    wqa, wka, wva, wqb, wkb, wvb, wgate = parts
    qscale = scale * LOG2E
    grp = lambda w, g: w[:, g * DSA_OUT:(g + 1) * DSA_OUT]
    wa = jnp.concatenate(
        [jnp.concatenate([grp(wqa, g) * qscale, grp(wka, g), grp(wva, g)], axis=1)
         for g in range(len(DSA_GROUPS))], axis=1).astype(bf16)
    wqvt = jnp.concatenate([wqb * qscale, wvb], axis=1).T.astype(bf16)
    kfeat, arows, slopes_b, cmask = _moba_constants(s)
    vpad = jnp.zeros((V_ROWS - HEAD_DIM, tm), bf16).at[0].set(1.0)
    a0, a1, a2, kaug, qt, vaug, ksum, gates = _proj(
        x1, row(norm_mix), wa, wkb.astype(bf16), wqvt, wgate.astype(bf16), kfeat, vpad, tm=tm)

    dsa_outs = [
        _dsa_group(a, _dsa_bias(gi, dil), group=gi)
        for gi, (a, (_, dil)) in enumerate(zip((a0, a1, a2), DSA_GROUPS))
    ]
    ybt = _moba(slopes_b, qt, kaug, vaug, ksum, arows, cmask)
    x2 = _merge(x1, gates, dsa_outs, ybt, w_up_a.astype(bf16), w_up_b.astype(bf16),
                w_out.astype(bf16), tm=tm)
    x3 = _ffn(x2.reshape(b * s, D_MODEL), row(norm_ffn2), ffn2_gate.astype(bf16), ffn2_up.astype(bf16),
              ffn2_down.astype(bf16), row(gain_final), final_norm=final_norm, tm=tm_ffn)
    return x3.reshape(b, s, D_MODEL)


def kernel(x, norm_ffn1, ffn1_gate, ffn1_up, ffn1_down, norm_mix, w_in, w_up_a, w_up_b, w_out,
           norm_ffn2, ffn2_gate, ffn2_up, ffn2_down, norm_final):
    depth = norm_ffn1.shape[0]
    for layer in range(depth):
        last = layer == depth - 1
        x = _layer(x, norm_ffn1[layer], ffn1_gate[layer], ffn1_up[layer], ffn1_down[layer],
                   norm_mix[layer], w_in[layer], w_up_a[layer], w_up_b[layer], w_out[layer],
                   norm_ffn2[layer], ffn2_gate[layer], ffn2_up[layer], ffn2_down[layer],
                   norm_final, final_norm=last)
    return x
```

```python
import functools

import numpy as np
import jax
import jax.numpy as jnp
from jax import lax
from jax.experimental import pallas as pl
from jax.experimental.pallas import tpu as pltpu

D_MODEL = 1024
HEAD_DIM = 64
DSA_GROUPS = ((128, 1), (512, 4), (2048, 16))
DSA_HEADS_PER_GROUP = 4
DSA_HEADS = DSA_HEADS_PER_GROUP * len(DSA_GROUPS)
MOBA_HEADS = 8
MOBA_BLOCK = 256
MOBA_TOPK = 3
D_FF = ((8 * D_MODEL // 3 + 127) // 128) * 128
DSA_WIDTH = DSA_HEADS * HEAD_DIM
MOBA_WIDTH = MOBA_HEADS * HEAD_DIM
DSA_OUT = DSA_HEADS_PER_GROUP * HEAD_DIM
EPS = 1e-6
NEG = -1e30

LANES = 128
HEAD_PAIR = 2 * HEAD_DIM
DSA_BLK = 128
DSA_TQ = DSA_BLK
DSA_BAND = DSA_TQ + DSA_BLK
DSA_STAGE_TILES = 2
DSA_GROUP_COLS = 3 * DSA_OUT
MOBA_TILE = 2 * MOBA_BLOCK
MAX_SLABS = 8
MOBA_STEP_PAIRS = 2
LOG2E = 1.4426950408889634
M_INIT = -3.0e38
V_ROWS = 80
K_FEAT = 128
VMEM_LIMIT = 56 * 1024 * 1024
ROW_GROUPS = 2

f32 = jnp.float32
bf16 = jnp.bfloat16


def _rmsnorm(x, g):
    return x * lax.rsqrt(jnp.mean(x * x, axis=-1, keepdims=True) + EPS) * g


def _dot(a, b):
    return jnp.dot(a, b, preferred_element_type=f32)


def _dot_nt(a, b):
    return lax.dot_general(a, b, (((1,), (1,)), ((), ())), preferred_element_type=f32)


def _dot_tn(a, b):
    return lax.dot_general(a, b, (((0,), (0,)), ((), ())), preferred_element_type=f32)


def _resident(shape):
    nd = len(shape)
    return pl.BlockSpec(shape, lambda *_: (0,) * nd, pipeline_mode=pl.Buffered(1))


def _ffn_kernel(x_ref, g_ref, wg_ref, wu_ref, wd_ref, gf_ref, o_ref, *, final_norm):
    rows = x_ref.shape[0] // ROW_GROUPS
    for part in range(ROW_GROUPS):
        sl = slice(part * rows, (part + 1) * rows)
        x = x_ref[sl, :]
        h = _rmsnorm(x, g_ref[...]).astype(bf16)
        gate = _dot(h, wg_ref[...])
        up = _dot(h, wu_ref[...])
        act = (gate * jax.nn.sigmoid(gate) * up).astype(bf16)
        y = x + 0.5 * _dot(act, wd_ref[...])
        if final_norm:
            y = _rmsnorm(y, gf_ref[...])
        o_ref[sl, :] = y


def _ffn(x2d, gain, w_gate, w_up, w_down, gain_final, *, final_norm, tm):
    n = x2d.shape[0]
    return pl.pallas_call(
        functools.partial(_ffn_kernel, final_norm=final_norm),
        out_shape=jax.ShapeDtypeStruct((n, D_MODEL), f32),
        grid=(n // tm,),
        in_specs=[
            pl.BlockSpec((tm, D_MODEL), lambda i: (i, 0)),
            _resident((1, D_MODEL)),
            _resident((D_MODEL, D_FF)),
            _resident((D_MODEL, D_FF)),
            _resident((D_FF, D_MODEL)),
            _resident((1, D_MODEL)),
        ],
        out_specs=pl.BlockSpec((tm, D_MODEL), lambda i: (i, 0)),
        compiler_params=pltpu.CompilerParams(
            dimension_semantics=("parallel",), vmem_limit_bytes=VMEM_LIMIT),
        name="ffn_final" if final_norm else "ffn",
    )(x2d, gain, w_gate, w_up, w_down, gain_final)


def _proj_kernel(x_ref, g_ref, wa_ref, wkb_ref, wqvt_ref, wgate_ref, kfeat_ref, vpad_ref,
                 a0_ref, a1_ref, a2_ref, kaug_ref, qt_ref, vaug_ref, ksum_ref, gates_ref, scr_ref, *, tm):
    rows = tm // ROW_GROUPS
    tiles_per_group = DSA_GROUP_COLS // LANES
    for part in range(ROW_GROUPS):
        sl = slice(part * rows, (part + 1) * rows)
        h = _rmsnorm(x_ref[0, sl, :], g_ref[...]).astype(bf16)
        qkv = _dot(h, wa_ref[...])
        for g, (a_ref, (_, dil)) in enumerate(zip((a0_ref, a1_ref, a2_ref), DSA_GROUPS)):
            per = rows // dil
            for c in range(tiles_per_group):
                lo = g * DSA_GROUP_COLS + c * LANES
                lanes = slice(c * LANES, (c + 1) * LANES)
                if dil == 1:
                    a_ref[0, 0, sl, lanes] = qkv[:, lo:lo + LANES].astype(bf16)
                    continue
                scr_ref[part, g, c] = qkv[:, lo:lo + LANES]
                for r in range(dil):
                    a_ref[0, r, part * per:(part + 1) * per, lanes] = (
                        scr_ref[part, g, c, pl.ds(r, per, stride=dil), :].astype(bf16))
        kb = _dot(h, wkb_ref[...])
        nb = rows // MOBA_BLOCK
        ksum_ref[part * nb:(part + 1) * nb] = (
            jnp.sum(kb.reshape(nb, MOBA_BLOCK, MOBA_WIDTH), axis=1)[:, None, :])
        kb = kb.astype(bf16)
        for p in range(MOBA_HEADS // 2):
            kaug_ref[0, p, sl, 0:HEAD_PAIR] = kb[:, p * HEAD_PAIR:(p + 1) * HEAD_PAIR]
            kaug_ref[0, p, sl, HEAD_PAIR:HEAD_PAIR + K_FEAT] = kfeat_ref[sl, :]
        qv = _dot_nt(wqvt_ref[...], h)
        qt_ref[0, :, sl] = qv[0:MOBA_WIDTH].astype(bf16)
        for hd in range(MOBA_HEADS):
            lo = MOBA_WIDTH + hd * HEAD_DIM
            vaug_ref[0, hd, 0:HEAD_DIM, sl] = qv[lo:lo + HEAD_DIM].astype(bf16)
            vaug_ref[0, hd, HEAD_DIM:V_ROWS, sl] = vpad_ref[:, sl]
        gates_ref[0, sl, :] = jax.nn.sigmoid(_dot(h, wgate_ref[...])).astype(bf16)


def _proj(x3d, gain, wa, wkb, wqvt, wgate, kfeat, vpad, *, tm):
    b, s, _ = x3d.shape
    nblk = s // MOBA_BLOCK
    grp_shapes = tuple(
        jax.ShapeDtypeStruct((b, dil, s // dil, DSA_GROUP_COLS), bf16) for _, dil in DSA_GROUPS)
    grp_specs = tuple(
        pl.BlockSpec((1, dil, tm // dil, DSA_GROUP_COLS), lambda bi, i: (bi, 0, i, 0))
        for _, dil in DSA_GROUPS)
    out_shape = grp_shapes + (
        jax.ShapeDtypeStruct((b, MOBA_HEADS // 2, s, HEAD_PAIR + K_FEAT), bf16),
        jax.ShapeDtypeStruct((b, MOBA_WIDTH, s), bf16),
        jax.ShapeDtypeStruct((b, MOBA_HEADS, V_ROWS, s), bf16),
        jax.ShapeDtypeStruct((b * nblk, 1, MOBA_WIDTH), f32),
        jax.ShapeDtypeStruct((b, s, 2 * D_MODEL), bf16),
    )
    per = tm // MOBA_BLOCK
    return pl.pallas_call(
        functools.partial(_proj_kernel, tm=tm),
        out_shape=out_shape,
        grid=(b, s // tm),
        in_specs=[
            pl.BlockSpec((1, tm, D_MODEL), lambda bi, i: (bi, i, 0)),
            _resident((1, D_MODEL)),
            _resident((D_MODEL, 3 * DSA_WIDTH)),
            _resident((D_MODEL, MOBA_WIDTH)),
            _resident((2 * MOBA_WIDTH, D_MODEL)),
            _resident((D_MODEL, 2 * D_MODEL)),
            pl.BlockSpec((tm, K_FEAT), lambda bi, i: (i, 0)),
            _resident((V_ROWS - HEAD_DIM, tm)),
        ],
        out_specs=grp_specs + (
            pl.BlockSpec((1, MOBA_HEADS // 2, tm, HEAD_PAIR + K_FEAT), lambda bi, i: (bi, 0, i, 0)),
            pl.BlockSpec((1, MOBA_WIDTH, tm), lambda bi, i: (bi, 0, i)),
            pl.BlockSpec((1, MOBA_HEADS, V_ROWS, tm), lambda bi, i: (bi, 0, 0, i)),
            pl.BlockSpec((per, 1, MOBA_WIDTH), lambda bi, i: (bi * (s // tm) + i, 0, 0)),
            pl.BlockSpec((1, tm, 2 * D_MODEL), lambda bi, i: (bi, i, 0)),
        ),
        scratch_shapes=[pltpu.VMEM(
            (ROW_GROUPS, len(DSA_GROUPS), DSA_GROUP_COLS // LANES, tm // ROW_GROUPS, LANES), f32)],
        compiler_params=pltpu.CompilerParams(
            dimension_semantics=("parallel", "parallel"), vmem_limit_bytes=VMEM_LIMIT),
        name="proj",
    )(x3d, gain, wa, wkb, wqvt, wgate, kfeat, vpad)


def _dsa_kernel(q_ref, k_ref, v_ref, bias_ref, o_ref, sa_ref, sb_ref, *, nstage, seg_tiles):
    tq, band = DSA_TQ, DSA_BAND
    first = lax.broadcasted_iota(jnp.int32, (tq, LANES), 1) < HEAD_DIM
    chains = [(u, h) for u in range(DSA_STAGE_TILES) for h in range(2)]

    def rows_of(t):
        return pl.ds(pl.multiple_of(t * tq, tq), tq)

    def seg_start(t):
        return (t % seg_tiles) == 0

    def band_of(t):
        start = jnp.where(seg_start(t), t * tq, t * tq - DSA_BLK)
        return pl.ds(pl.multiple_of(start, DSA_BLK), band)

    def scores(stage, s_ref):
        for u in range(DSA_STAGE_TILES):
            t = stage * DSA_STAGE_TILES + u
            q = q_ref[0, rows_of(t), :]
            kband = k_ref[0, band_of(t), :]
            for h in range(2):
                qm = jnp.where(first if h == 0 else jnp.logical_not(first), q, jnp.zeros_like(q))
                s_ref[2 * u + h] = _dot_nt(qm, kband)

    def finish(stage, s_ref):
        for u in range(DSA_STAGE_TILES):
            t = stage * DSA_STAGE_TILES + u
            variant = jnp.where(seg_start(t), 0, 1)
            vband = v_ref[0, band_of(t), :]
            outs, lses = [], []
            for h in range(2):
                s = s_ref[2 * u + h] + bias_ref[0, h, variant]
                m = jnp.max(s, axis=-1, keepdims=True)
                p = jnp.exp2(s - m)
                l = jnp.sum(p, axis=-1, keepdims=True)
                o = _dot(p.astype(bf16), vband)
                outs.append(o / l)
                lses.append(jnp.broadcast_to(m + jnp.log(l) * LOG2E, (tq, LANES)))
            o_ref[0, rows_of(t), 0:LANES] = jnp.where(first, outs[0], outs[1])
            o_ref[0, rows_of(t), LANES:2 * LANES] = jnp.where(first, lses[0], lses[1])

    zero = jnp.int32(0)
    scores(zero, sa_ref)

    def body(i2, carry):
        st = 2 * i2
        scores(st + 1, sb_ref)
        finish(st, sa_ref)
        scores(st + 2, sa_ref)
        finish(st + 1, sb_ref)
        return carry

    lax.fori_loop(0, nstage // 2 - 1, body, 0)
    st_end = zero + (nstage - 2)
    scores(st_end + 1, sb_ref)
    finish(st_end, sa_ref)
    finish(st_end + 1, sb_ref)


def _dsa_group(a, bias, *, group):
    b, dil, length, _ = a.shape
    s = dil * length
    pairs = DSA_HEADS_PER_GROUP // 2
    seg_tiles = length // DSA_TQ
    nstage = s // (DSA_TQ * DSA_STAGE_TILES)
    assert seg_tiles >= 2 and length % DSA_TQ == 0 and nstage % 2 == 0
    flat = a.reshape(b, s, DSA_GROUP_COLS)
    nchain = 2 * DSA_STAGE_TILES
    col = lambda part: (lambda bi, hp: (bi, 0, part * pairs + hp))
    out = pl.pallas_call(
        functools.partial(_dsa_kernel, nstage=nstage, seg_tiles=seg_tiles),
        out_shape=jax.ShapeDtypeStruct((b, s, 2 * pairs * LANES), f32),
        grid=(b, pairs),
        in_specs=[
            pl.BlockSpec((1, s, LANES), col(0)),
            pl.BlockSpec((1, s, LANES), col(1)),
            pl.BlockSpec((1, s, LANES), col(2)),
            pl.BlockSpec((1, 2, 2, DSA_TQ, DSA_BAND), lambda bi, hp: (hp, 0, 0, 0, 0)),
        ],
        out_specs=pl.BlockSpec((1, s, 2 * LANES), lambda bi, hp: (bi, 0, hp)),
        scratch_shapes=[pltpu.VMEM((nchain, DSA_TQ, DSA_BAND), f32),
                        pltpu.VMEM((nchain, DSA_TQ, DSA_BAND), f32)],
        compiler_params=pltpu.CompilerParams(
            dimension_semantics=("parallel", "parallel"), vmem_limit_bytes=VMEM_LIMIT),
        name=f"dsa_g{group}",
    )(flat, flat, flat, bias)
    return out.reshape(b, dil, length, 2 * pairs * LANES)


def _moba_kernel(slope_ref, qt_ref, kaug_ref, vaug_ref, ksum_ref, arow_ref, cmask_ref, o_ref,
                 rhs_ref, sa_ref, sb_ref, acc_ref, m_ref, *, nblk):
    hp0 = pl.program_id(1) * MOBA_STEP_PAIRS
    n2 = pl.program_id(2)
    blk = MOBA_BLOCK
    tile = MOBA_TILE
    ntile = n2 + 1
    chains = [(pp, qb, h) for pp in range(MOBA_STEP_PAIRS) for qb in range(2) for h in range(2)]
    slot = {chain: c for c, chain in enumerate(chains)}
    rowi = lax.broadcasted_iota(jnp.int32, (HEAD_PAIR, tile), 0)
    bidx = lax.broadcasted_iota(jnp.int32, (nblk, tile), 0)
    own = 2 * n2 + (lax.broadcasted_iota(jnp.int32, (nblk, tile), 1) >= blk).astype(jnp.int32)
    past = bidx < own

    gates = {}
    for pp in range(MOBA_STEP_PAIRS):
        qpair = qt_ref[0, pp * HEAD_PAIR:(pp + 1) * HEAD_PAIR, :]
        kmean = ksum_ref[0, :, pp * HEAD_PAIR:(pp + 1) * HEAD_PAIR] * (1.0 / blk)
        kmean_hi = kmean.astype(bf16)
        kmean_lo = (kmean - kmean_hi.astype(f32)).astype(bf16)
        for h in range(2):
            mine = (rowi >= HEAD_DIM) if h else (rowi < HEAD_DIM)
            qpad = jnp.where(mine, qpair, jnp.zeros_like(qpair))
            gates[pp, h] = _dot(kmean_hi, qpad) + _dot(kmean_lo, qpad)
            for qb in range(2):
                c = slot[pp, qb, h]
                rhs_ref[c, 0:HEAD_PAIR, :] = qpad[:, qb * blk:(qb + 1) * blk]
                rhs_ref[c, HEAD_PAIR:HEAD_PAIR + nblk, :] = jnp.zeros((nblk, blk), bf16)
                rhs_ref[c, HEAD_PAIR + nblk:HEAD_PAIR + K_FEAT, :] = arow_ref[pp, h]
                acc_ref[c] = jnp.zeros((V_ROWS, blk), f32)
                m_ref[c] = jnp.full((1, blk), M_INIT, f32)

    def scores(i, s_ref):
        k0 = pl.multiple_of(i * tile, tile)
        for pp in range(MOBA_STEP_PAIRS):
            kt = kaug_ref[0, pp, pl.ds(k0, tile), :]
            for qb in range(2):
                for h in range(2):
                    c = slot[pp, qb, h]
                    s_ref[c] = _dot(kt, rhs_ref[c])

    scores(0, sa_ref)

    for (pp, h), gate in gates.items():
        g = jnp.where(past, gate, NEG)
        sel = bidx == own
        for _ in range(MOBA_TOPK):
            mx = jnp.max(g, axis=0, keepdims=True)
            idx = jnp.min(jnp.where(g == mx, bidx, nblk), axis=0, keepdims=True)
            pick = bidx == idx
            sel = sel | (pick & past)
            g = jnp.where(pick, -jnp.inf, g)
        maskbias = jnp.where(sel, 0.0, NEG).astype(bf16)
        for qb in range(2):
            c = slot[pp, qb, h]
            mb = maskbias[:, qb * blk:(qb + 1) * blk]
            rhs_ref[c, HEAD_PAIR:HEAD_PAIR + nblk, :] = mb
            for kb in range(tile // blk):
                sa_ref[c, kb * blk:(kb + 1) * blk, :] += mb[kb:kb + 1, :].astype(f32)

    def update(i, s_ref, diagonal):
        k0 = pl.multiple_of(i * tile, tile)
        for c, (pp, qb, h) in enumerate(chains):
            s = s_ref[c]
            if diagonal:
                s = s + cmask_ref[qb]
            rel = tile * (i - n2) - blk * qb
            shift = slope_ref[2 * (hp0 + pp) + h] * rel.astype(f32)
            m_old = m_ref[c]
            smax = jnp.max(s.reshape(MAX_SLABS, tile // MAX_SLABS, blk), axis=0)
            m_new = jnp.maximum(m_old, jnp.max(smax, axis=0, keepdims=True) + shift)
            p = jnp.exp2(s - (m_new - shift)).astype(bf16)
            pv = _dot(vaug_ref[0, 2 * pp + h, :, pl.ds(k0, tile)], p)
            acc_ref[c] = jnp.exp2(m_old - m_new) * acc_ref[c] + pv
            m_ref[c] = m_new

    npair = (ntile - 1) // 2

    def body(i2, carry):
        t = 2 * i2
        scores(t + 1, sb_ref)
        update(t, sa_ref, False)
        scores(t + 2, sa_ref)
        update(t + 1, sb_ref, False)
        return carry

    lax.fori_loop(0, npair, body, 0)
    t_end = 2 * npair

    @pl.when(ntile - t_end == 1)
    def _():
        update(t_end, sa_ref, True)

    @pl.when(ntile - t_end == 2)
    def _():
        scores(t_end + 1, sb_ref)
        update(t_end, sa_ref, False)
        update(t_end + 1, sb_ref, True)

    for c, (pp, qb, h) in enumerate(chains):
        acc = acc_ref[c]
        row0 = pp * HEAD_PAIR + h * HEAD_DIM
        o_ref[0, row0:row0 + HEAD_DIM, qb * blk:(qb + 1) * blk] = (
            acc[0:HEAD_DIM] / acc[HEAD_DIM:HEAD_DIM + 1]).astype(bf16)


def _moba(slopes, qt, kaug, vaug, ksum, arows, cmask):
    b, _, s = qt.shape
    nblk = s // MOBA_BLOCK
    pairs = MOBA_HEADS // 2
    assert nblk + arows.shape[2] == K_FEAT and nblk % 2 == 0 and pairs % MOBA_STEP_PAIRS == 0
    npp = MOBA_STEP_PAIRS
    nchain = 4 * npp
    return pl.pallas_call(
        functools.partial(_moba_kernel, nblk=nblk),
        out_shape=jax.ShapeDtypeStruct((b, MOBA_WIDTH, s), bf16),
        grid=(b, pairs // npp, nblk // 2),
        in_specs=[
            pl.BlockSpec(memory_space=pltpu.SMEM),
            pl.BlockSpec((1, npp * HEAD_PAIR, MOBA_TILE), lambda bi, hp, n2: (bi, hp, n2)),
            pl.BlockSpec((1, npp, s, HEAD_PAIR + K_FEAT), lambda bi, hp, n2: (bi, hp, 0, 0)),
            pl.BlockSpec((1, 2 * npp, V_ROWS, s), lambda bi, hp, n2: (bi, hp, 0, 0)),
            pl.BlockSpec((1, nblk, npp * HEAD_PAIR), lambda bi, hp, n2: (bi, 0, hp)),
            pl.BlockSpec((npp, 2, K_FEAT - nblk, MOBA_BLOCK), lambda bi, hp, n2: (hp, 0, 0, 0)),
            _resident((2, MOBA_TILE, MOBA_BLOCK)),
        ],
        out_specs=pl.BlockSpec((1, npp * HEAD_PAIR, MOBA_TILE), lambda bi, hp, n2: (bi, hp, n2)),
        scratch_shapes=[
            pltpu.VMEM((nchain, HEAD_PAIR + K_FEAT, MOBA_BLOCK), bf16),
            pltpu.VMEM((nchain, MOBA_TILE, MOBA_BLOCK), f32),
            pltpu.VMEM((nchain, MOBA_TILE, MOBA_BLOCK), f32),
            pltpu.VMEM((nchain, V_ROWS, MOBA_BLOCK), f32),
            pltpu.VMEM((nchain, 1, MOBA_BLOCK), f32),
        ],
        compiler_params=pltpu.CompilerParams(
            dimension_semantics=("parallel", "parallel", "arbitrary"), vmem_limit_bytes=VMEM_LIMIT),
        name="moba",
    )(slopes, qt, kaug, vaug, ksum.reshape(b, nblk, MOBA_WIDTH), arows, cmask)


def _merge_kernel(x_ref, gates_ref, d0_ref, d1_ref, d2_ref, ybt_ref, wua_ref, wub_ref, wo_ref, o_ref,
                  nat_ref, *, tm):
    rows = tm // ROW_GROUPS

    def token_order(g, c, part):
        d_ref, dil = (d0_ref, d1_ref, d2_ref)[g], DSA_GROUPS[g][1]
        per = rows // dil
        lanes = slice(c * LANES, (c + 1) * LANES)
        if dil == 1:
            return d_ref[0, 0, part * rows:(part + 1) * rows, lanes]
        for r in range(dil):
            nat_ref[g, c, pl.ds(part * rows + r, per, stride=dil), :] = (
                d_ref[0, r, part * per:(part + 1) * per, lanes])
        return nat_ref[g, c, part * rows:(part + 1) * rows, :]

    for part in range(ROW_GROUPS):
        sl = slice(part * rows, (part + 1) * rows)
        pairs = []
        for hp in range(2):
            outs = [token_order(g, 2 * hp, part) for g in range(len(DSA_GROUPS))]
            lses = [token_order(g, 2 * hp + 1, part) for g in range(len(DSA_GROUPS))]
            mx = jnp.maximum(jnp.maximum(lses[0], lses[1]), lses[2])
            es = [jnp.exp2(l - mx) for l in lses]
            den = es[0] + es[1] + es[2]
            pairs.append((es[0] * outs[0] + es[1] * outs[1] + es[2] * outs[2]) / den)
        ya = jnp.concatenate(pairs, axis=1).astype(bf16)
        ta = _dot(ya, wua_ref[...])
        tb = _dot_tn(ybt_ref[0, :, sl], wub_ref[...])
        gates = gates_ref[0, sl, :]
        merged = gates[:, 0:D_MODEL].astype(f32) * ta + gates[:, D_MODEL:].astype(f32) * tb
        o_ref[0, sl, :] = x_ref[0, sl, :] + _dot(merged.astype(bf16), wo_ref[...])


def _merge(x3d, gates, dsa_outs, ybt, wua, wub, wo, *, tm):
    b, s, _ = x3d.shape
    tok = lambda w: pl.BlockSpec((1, tm, w), lambda bi, i: (bi, i, 0))
    grp = lambda dil: pl.BlockSpec((1, dil, tm // dil, 4 * LANES), lambda bi, i: (bi, 0, i, 0))
    return pl.pallas_call(
        functools.partial(_merge_kernel, tm=tm),
        out_shape=jax.ShapeDtypeStruct((b, s, D_MODEL), f32),
        grid=(b, s // tm),
        in_specs=[
            tok(D_MODEL), tok(2 * D_MODEL), *[grp(dil) for _, dil in DSA_GROUPS],
            pl.BlockSpec((1, MOBA_WIDTH, tm), lambda bi, i: (bi, 0, i)),
            _resident((DSA_OUT, D_MODEL)),
            _resident((MOBA_WIDTH, D_MODEL)),
            _resident((D_MODEL, D_MODEL)),
        ],
        out_specs=tok(D_MODEL),
        scratch_shapes=[pltpu.VMEM((len(DSA_GROUPS), 4, tm, LANES), f32)],
        compiler_params=pltpu.CompilerParams(
            dimension_semantics=("parallel", "parallel"), vmem_limit_bytes=VMEM_LIMIT),
        name="merge",
    )(x3d, gates, *dsa_outs, ybt, wua, wub, wo)


def _alibi_slopes(n):
    return 2.0 ** (-8.0 * np.arange(1, n + 1, dtype=np.float64) / n)


def _split3(x):
    hi = x.astype(bf16)
    r1 = x - hi.astype(f32)
    mid = r1.astype(bf16)
    lo = (r1 - mid.astype(f32)).astype(bf16)
    return hi, mid, lo


def _moba_constants(s):
    nblk = s // MOBA_BLOCK
    n_arow = K_FEAT - nblk
    pos = np.arange(s)
    kfeat = np.zeros((s, K_FEAT), np.float32)
    kfeat[pos, pos // MOBA_BLOCK] = 1.0
    kfeat[:, nblk:nblk + 3] = 1.0
    kfeat[:, nblk + 3:nblk + 6] = (pos % MOBA_BLOCK)[:, None]
    kfeat[:, nblk + 6:nblk + 9] = ((pos // MOBA_BLOCK) % 2)[:, None]
    slopes = jnp.asarray(_alibi_slopes(MOBA_HEADS) * LOG2E, f32)
    off = jnp.arange(MOBA_BLOCK, dtype=f32)
    wide = lambda v: jnp.broadcast_to(v[:, None], (MOBA_HEADS, MOBA_BLOCK))
    qterm = _split3(-slopes[:, None] * off[None, :])
    sterm = _split3(wide(slopes))
    bterm = _split3(wide(slopes * MOBA_BLOCK))
    rows = jnp.stack(list(qterm) + list(sterm) + list(bterm), axis=1)
    rows = jnp.concatenate(
        [rows, jnp.zeros((MOBA_HEADS, n_arow - 9, MOBA_BLOCK), bf16)], axis=1)
    arows = rows.reshape(MOBA_HEADS // 2, 2, n_arow, MOBA_BLOCK)
    r = jnp.arange(MOBA_TILE)[None, :, None]
    c = jnp.arange(MOBA_BLOCK)[None, None, :] + MOBA_BLOCK * jnp.arange(2)[:, None, None]
    cmask = jnp.where(r <= c, 0.0, NEG).astype(f32)
    return jnp.asarray(kfeat, bf16), arows, slopes, cmask


def _dsa_bias(group, dilation):
    slopes = _alibi_slopes(DSA_HEADS)[group * DSA_HEADS_PER_GROUP:(group + 1) * DSA_HEADS_PER_GROUP]
    slopes = jnp.asarray(slopes * LOG2E * dilation, f32).reshape(2, 2, 1, 1, 1)
    qi = jnp.arange(DSA_TQ)[:, None]
    kj = jnp.arange(DSA_BAND)[None, :]
    delta = jnp.stack([qi - kj, qi + DSA_BLK - kj])[None, None]
    valid = (delta >= 0) & (delta <= DSA_BLK)
    return jnp.where(valid, -slopes * delta.astype(f32), NEG)


def _layer(x, norm_ffn1, ffn1_gate, ffn1_up, ffn1_down, norm_mix, w_in, w_up_a, w_up_b, w_out,
           norm_ffn2, ffn2_gate, ffn2_up, ffn2_down, gain_final, *, final_norm):
    b, s, _ = x.shape
    tm = 512
    tm_ffn = 512
    row = lambda g: g.reshape(1, D_MODEL).astype(f32)
    x1 = _ffn(x.reshape(b * s, D_MODEL), row(norm_ffn1), ffn1_gate.astype(bf16), ffn1_up.astype(bf16),
              ffn1_down.astype(bf16), row(gain_final), final_norm=False, tm=tm_ffn).reshape(b, s, D_MODEL)

    scale = HEAD_DIM ** -0.5
    o = 0
    parts = []
    for width in (DSA_WIDTH, DSA_WIDTH, DSA_WIDTH, MOBA_WIDTH, MOBA_WIDTH, MOBA_WIDTH, 2 * D_MODEL):
        parts.append(w_in[:, o:o + width])
        o += width
    wqa, wka, wva, wqb, wkb, wvb, wgate = parts
    qscale = scale * LOG2E
    grp = lambda w, g: w[:, g * DSA_OUT:(g + 1) * DSA_OUT]
    wa = jnp.concatenate(
        [jnp.concatenate([grp(wqa, g) * qscale, grp(wka, g), grp(wva, g)], axis=1)
         for g in range(len(DSA_GROUPS))], axis=1).astype(bf16)
    wqvt = jnp.concatenate([wqb * qscale, wvb], axis=1).T.astype(bf16)
    kfeat, arows, slopes_b, cmask = _moba_constants(s)
    vpad = jnp.zeros((V_ROWS - HEAD_DIM, tm), bf16).at[0].set(1.0)
    a0, a1, a2, kaug, qt, vaug, ksum, gates = _proj(
        x1, row(norm_mix), wa, wkb.astype(bf16), wqvt, wgate.astype(bf16), kfeat, vpad, tm=tm)

    dsa_outs = [
        _dsa_group(a, _dsa_bias(gi, dil), group=gi)
        for gi, (a, (_, dil)) in enumerate(zip((a0, a1, a2), DSA_GROUPS))
    ]
    ybt = _moba(slopes_b, qt, kaug, vaug, ksum, arows, cmask)
    x2 = _merge(x1, gates, dsa_outs, ybt, w_up_a.astype(bf16), w_up_b.astype(bf16),
                w_out.astype(bf16), tm=tm)
    x3 = _ffn(x2.reshape(b * s, D_MODEL), row(norm_ffn2), ffn2_gate.astype(bf16), ffn2_up.astype(bf16),
              ffn2_down.astype(bf16), row(gain_final), final_norm=final_norm, tm=tm_ffn)
    return x3.reshape(b, s, D_MODEL)


def kernel(x, norm_ffn1, ffn1_gate, ffn1_up, ffn1_down, norm_mix, w_in, w_up_a, w_up_b, w_out,
           norm_ffn2, ffn2_gate, ffn2_up, ffn2_down, norm_final):
    depth = norm_ffn1.shape[0]
    for layer in range(depth):
        last = layer == depth - 1
        x = _layer(x, norm_ffn1[layer], ffn1_gate[layer], ffn1_up[layer], ffn1_down[layer],
                   norm_mix[layer], w_in[layer], w_up_a[layer], w_up_b[layer], w_out[layer],
                   norm_ffn2[layer], ffn2_gate[layer], ffn2_up[layer], ffn2_down[layer],
                   norm_final, final_norm=last)
    return x
```

```python
import functools

import numpy as np
import jax
import jax.numpy as jnp
from jax import lax
from jax.experimental import pallas as pl
from jax.experimental.pallas import tpu as pltpu

D_MODEL = 1024
HEAD_DIM = 64
DSA_GROUPS = ((128, 1), (512, 4), (2048, 16))
DSA_HEADS_PER_GROUP = 4
DSA_HEADS = DSA_HEADS_PER_GROUP * len(DSA_GROUPS)
MOBA_HEADS = 8
MOBA_BLOCK = 256
MOBA_TOPK = 3
D_FF = ((8 * D_MODEL // 3 + 127) // 128) * 128
DSA_WIDTH = DSA_HEADS * HEAD_DIM
MOBA_WIDTH = MOBA_HEADS * HEAD_DIM
DSA_OUT = DSA_HEADS_PER_GROUP * HEAD_DIM
EPS = 1e-6
NEG = -1e30

LANES = 128
HEAD_PAIR = 2 * HEAD_DIM
DSA_BLK = 128
DSA_TQ = DSA_BLK
DSA_BAND = DSA_TQ + DSA_BLK
DSA_STAGE_TILES = 2
DSA_GROUP_COLS = 3 * DSA_OUT
MOBA_TILE = 2 * MOBA_BLOCK
MAX_SLABS = 8
MOBA_STEP_PAIRS = 2
LOG2E = 1.4426950408889634
M_INIT = -3.0e38
V_ROWS = 80
K_FEAT = 128
VMEM_LIMIT = 56 * 1024 * 1024
ROW_GROUPS = 2

f32 = jnp.float32
bf16 = jnp.bfloat16


def _rmsnorm(x, g):
    return x * lax.rsqrt(jnp.mean(x * x, axis=-1, keepdims=True) + EPS) * g


def _dot(a, b):
    return jnp.dot(a, b, preferred_element_type=f32)


def _dot_nt(a, b):
    return lax.dot_general(a, b, (((1,), (1,)), ((), ())), preferred_element_type=f32)


def _dot_tn(a, b):
    return lax.dot_general(a, b, (((0,), (0,)), ((), ())), preferred_element_type=f32)


def _resident(shape):
    nd = len(shape)
    return pl.BlockSpec(shape, lambda *_: (0,) * nd, pipeline_mode=pl.Buffered(1))


def _ffn_kernel(x_ref, g_ref, wg_ref, wu_ref, wd_ref, gf_ref, o_ref, *, final_norm):
    rows = x_ref.shape[0] // ROW_GROUPS
    for part in range(ROW_GROUPS):
        sl = slice(part * rows, (part + 1) * rows)
        x = x_ref[sl, :]
        h = _rmsnorm(x, g_ref[...]).astype(bf16)
        gate = _dot(h, wg_ref[...])
        up = _dot(h, wu_ref[...])
        act = (gate * jax.nn.sigmoid(gate) * up).astype(bf16)
        y = x + 0.5 * _dot(act, wd_ref[...])
        if final_norm:
            y = _rmsnorm(y, gf_ref[...])
        o_ref[sl, :] = y


def _ffn(x2d, gain, w_gate, w_up, w_down, gain_final, *, final_norm, tm):
    n = x2d.shape[0]
    return pl.pallas_call(
        functools.partial(_ffn_kernel, final_norm=final_norm),
        out_shape=jax.ShapeDtypeStruct((n, D_MODEL), f32),
        grid=(n // tm,),
        in_specs=[
            pl.BlockSpec((tm, D_MODEL), lambda i: (i, 0)),
            _resident((1, D_MODEL)),
            _resident((D_MODEL, D_FF)),
            _resident((D_MODEL, D_FF)),
            _resident((D_FF, D_MODEL)),
            _resident((1, D_MODEL)),
        ],
        out_specs=pl.BlockSpec((tm, D_MODEL), lambda i: (i, 0)),
        compiler_params=pltpu.CompilerParams(
            dimension_semantics=("parallel",), vmem_limit_bytes=VMEM_LIMIT),
        name="ffn_final" if final_norm else "ffn",
    )(x2d, gain, w_gate, w_up, w_down, gain_final)


def _proj_kernel(x_ref, g_ref, wa_ref, wkb_ref, wqvt_ref, kfeat_ref, vpad_ref,
                 a0_ref, a1_ref, a2_ref, kaug_ref, qt_ref, vaug_ref, ksum_ref, scr_ref, *, tm):
    rows = tm // ROW_GROUPS
    tiles_per_group = DSA_GROUP_COLS // LANES
    for part in range(ROW_GROUPS):
        sl = slice(part * rows, (part + 1) * rows)
        h = _rmsnorm(x_ref[0, sl, :], g_ref[...]).astype(bf16)
        qkv = _dot(h, wa_ref[...])
        for g, (a_ref, (_, dil)) in enumerate(zip((a0_ref, a1_ref, a2_ref), DSA_GROUPS)):
            per = rows // dil
            for c in range(tiles_per_group):
                lo = g * DSA_GROUP_COLS + c * LANES
                lanes = slice(c * LANES, (c + 1) * LANES)
                if dil == 1:
                    a_ref[0, 0, sl, lanes] = qkv[:, lo:lo + LANES].astype(bf16)
                    continue
                scr_ref[part, g, c] = qkv[:, lo:lo + LANES]
                for r in range(dil):
                    a_ref[0, r, part * per:(part + 1) * per, lanes] = (
                        scr_ref[part, g, c, pl.ds(r, per, stride=dil), :].astype(bf16))
        kb = _dot(h, wkb_ref[...])
        nb = rows // MOBA_BLOCK
        ksum_ref[part * nb:(part + 1) * nb] = (
            jnp.sum(kb.reshape(nb, MOBA_BLOCK, MOBA_WIDTH), axis=1)[:, None, :])
        kb = kb.astype(bf16)
        for p in range(MOBA_HEADS // 2):
            kaug_ref[0, p, sl, 0:HEAD_PAIR] = kb[:, p * HEAD_PAIR:(p + 1) * HEAD_PAIR]
            kaug_ref[0, p, sl, HEAD_PAIR:HEAD_PAIR + K_FEAT] = kfeat_ref[sl, :]
        qv = _dot_nt(wqvt_ref[...], h)
        qt_ref[0, :, sl] = qv[0:MOBA_WIDTH].astype(bf16)
        for hd in range(MOBA_HEADS):
            lo = MOBA_WIDTH + hd * HEAD_DIM
            vaug_ref[0, hd, 0:HEAD_DIM, sl] = qv[lo:lo + HEAD_DIM].astype(bf16)
            vaug_ref[0, hd, HEAD_DIM:V_ROWS, sl] = vpad_ref[:, sl]


def _proj(x3d, gain, wa, wkb, wqvt, kfeat, vpad, *, tm):
    b, s, _ = x3d.shape
    nblk = s // MOBA_BLOCK
    grp_shapes = tuple(
        jax.ShapeDtypeStruct((b, dil, s // dil, DSA_GROUP_COLS), bf16) for _, dil in DSA_GROUPS)
    grp_specs = tuple(
        pl.BlockSpec((1, dil, tm // dil, DSA_GROUP_COLS), lambda bi, i: (bi, 0, i, 0))
        for _, dil in DSA_GROUPS)
    out_shape = grp_shapes + (
        jax.ShapeDtypeStruct((b, MOBA_HEADS // 2, s, HEAD_PAIR + K_FEAT), bf16),
        jax.ShapeDtypeStruct((b, MOBA_WIDTH, s), bf16),
        jax.ShapeDtypeStruct((b, MOBA_HEADS, V_ROWS, s), bf16),
        jax.ShapeDtypeStruct((b * nblk, 1, MOBA_WIDTH), f32),
    )
    per = tm // MOBA_BLOCK
    return pl.pallas_call(
        functools.partial(_proj_kernel, tm=tm),
        out_shape=out_shape,
        grid=(b, s // tm),
        in_specs=[
            pl.BlockSpec((1, tm, D_MODEL), lambda bi, i: (bi, i, 0)),
            _resident((1, D_MODEL)),
            _resident((D_MODEL, 3 * DSA_WIDTH)),
            _resident((D_MODEL, MOBA_WIDTH)),
            _resident((2 * MOBA_WIDTH, D_MODEL)),
            pl.BlockSpec((tm, K_FEAT), lambda bi, i: (i, 0)),
            _resident((V_ROWS - HEAD_DIM, tm)),
        ],
        out_specs=grp_specs + (
            pl.BlockSpec((1, MOBA_HEADS // 2, tm, HEAD_PAIR + K_FEAT), lambda bi, i: (bi, 0, i, 0)),
            pl.BlockSpec((1, MOBA_WIDTH, tm), lambda bi, i: (bi, 0, i)),
            pl.BlockSpec((1, MOBA_HEADS, V_ROWS, tm), lambda bi, i: (bi, 0, 0, i)),
            pl.BlockSpec((per, 1, MOBA_WIDTH), lambda bi, i: (bi * (s // tm) + i, 0, 0)),
        ),
        scratch_shapes=[pltpu.VMEM(
            (ROW_GROUPS, len(DSA_GROUPS), DSA_GROUP_COLS // LANES, tm // ROW_GROUPS, LANES), f32)],
        compiler_params=pltpu.CompilerParams(
            dimension_semantics=("parallel", "parallel"), vmem_limit_bytes=VMEM_LIMIT),
        name="proj",
    )(x3d, gain, wa, wkb, wqvt, kfeat, vpad)


def _dsa_kernel(q_ref, k_ref, v_ref, bias_ref, o_ref, sa_ref, sb_ref, *, nstage, seg_tiles):
    tq, band = DSA_TQ, DSA_BAND
    first = lax.broadcasted_iota(jnp.int32, (tq, LANES), 1) < HEAD_DIM
    chains = [(u, h) for u in range(DSA_STAGE_TILES) for h in range(2)]

    def rows_of(t):
        return pl.ds(pl.multiple_of(t * tq, tq), tq)

    def seg_start(t):
        return (t % seg_tiles) == 0

    def band_of(t):
        start = jnp.where(seg_start(t), t * tq, t * tq - DSA_BLK)
        return pl.ds(pl.multiple_of(start, DSA_BLK), band)

    def scores(stage, s_ref):
        for u in range(DSA_STAGE_TILES):
            t = stage * DSA_STAGE_TILES + u
            q = q_ref[0, rows_of(t), :]
            kband = k_ref[0, band_of(t), :]
            for h in range(2):
                qm = jnp.where(first if h == 0 else jnp.logical_not(first), q, jnp.zeros_like(q))
                s_ref[2 * u + h] = _dot_nt(qm, kband)

    def finish(stage, s_ref):
        for u in range(DSA_STAGE_TILES):
            t = stage * DSA_STAGE_TILES + u
            variant = jnp.where(seg_start(t), 0, 1)
            vband = v_ref[0, band_of(t), :]
            outs, lses = [], []
            for h in range(2):
                s = s_ref[2 * u + h] + bias_ref[0, h, variant]
                m = jnp.max(s, axis=-1, keepdims=True)
                p = jnp.exp2(s - m)
                l = jnp.sum(p, axis=-1, keepdims=True)
                o = _dot(p.astype(bf16), vband)
                outs.append(o / l)
                lses.append(jnp.broadcast_to(m + jnp.log(l) * LOG2E, (tq, LANES)))
            o_ref[0, rows_of(t), 0:LANES] = jnp.where(first, outs[0], outs[1])
            o_ref[0, rows_of(t), LANES:2 * LANES] = jnp.where(first, lses[0], lses[1])

    zero = jnp.int32(0)
    scores(zero, sa_ref)

    def body(i2, carry):
        st = 2 * i2
        scores(st + 1, sb_ref)
        finish(st, sa_ref)
        scores(st + 2, sa_ref)
        finish(st + 1, sb_ref)
        return carry

    lax.fori_loop(0, nstage // 2 - 1, body, 0)
    st_end = zero + (nstage - 2)
    scores(st_end + 1, sb_ref)
    finish(st_end, sa_ref)
    finish(st_end + 1, sb_ref)


def _dsa_group(a, bias, *, group):
    b, dil, length, _ = a.shape
    s = dil * length
    pairs = DSA_HEADS_PER_GROUP // 2
    seg_tiles = length // DSA_TQ
    nstage = s // (DSA_TQ * DSA_STAGE_TILES)
    assert seg_tiles >= 2 and length % DSA_TQ == 0 and nstage % 2 == 0
    flat = a.reshape(b, s, DSA_GROUP_COLS)
    nchain = 2 * DSA_STAGE_TILES
    col = lambda part: (lambda bi, hp: (bi, 0, part * pairs + hp))
    out = pl.pallas_call(
        functools.partial(_dsa_kernel, nstage=nstage, seg_tiles=seg_tiles),
        out_shape=jax.ShapeDtypeStruct((b, s, 2 * pairs * LANES), f32),
        grid=(b, pairs),
        in_specs=[
            pl.BlockSpec((1, s, LANES), col(0)),
            pl.BlockSpec((1, s, LANES), col(1)),
            pl.BlockSpec((1, s, LANES), col(2)),
            pl.BlockSpec((1, 2, 2, DSA_TQ, DSA_BAND), lambda bi, hp: (hp, 0, 0, 0, 0)),
        ],
        out_specs=pl.BlockSpec((1, s, 2 * LANES), lambda bi, hp: (bi, 0, hp)),
        scratch_shapes=[pltpu.VMEM((nchain, DSA_TQ, DSA_BAND), f32),
                        pltpu.VMEM((nchain, DSA_TQ, DSA_BAND), f32)],
        compiler_params=pltpu.CompilerParams(
            dimension_semantics=("parallel", "parallel"), vmem_limit_bytes=VMEM_LIMIT),
        name=f"dsa_g{group}",
    )(flat, flat, flat, bias)
    return out.reshape(b, dil, length, 2 * pairs * LANES)


def _moba_kernel(slope_ref, qt_ref, kaug_ref, vaug_ref, ksum_ref, arow_ref, cmask_ref, o_ref,
                 rhs_ref, sa_ref, sb_ref, mta_ref, mtb_ref, acc_ref, m_ref, *, nblk):
    hp0 = pl.program_id(1) * MOBA_STEP_PAIRS
    n2 = pl.program_id(2)
    blk = MOBA_BLOCK
    tile = MOBA_TILE
    ntile = n2 + 1
    chains = [(pp, qb, h) for pp in range(MOBA_STEP_PAIRS) for qb in range(2) for h in range(2)]
    slot = {chain: c for c, chain in enumerate(chains)}
    rowi = lax.broadcasted_iota(jnp.int32, (HEAD_PAIR, tile), 0)
    bidx = lax.broadcasted_iota(jnp.int32, (nblk, tile), 0)
    own = 2 * n2 + (lax.broadcasted_iota(jnp.int32, (nblk, tile), 1) >= blk).astype(jnp.int32)
    past = bidx < own

    gates = {}
    for pp in range(MOBA_STEP_PAIRS):
        qpair = qt_ref[0, pp * HEAD_PAIR:(pp + 1) * HEAD_PAIR, :]
        kmean = ksum_ref[0, :, pp * HEAD_PAIR:(pp + 1) * HEAD_PAIR] * (1.0 / blk)
        kmean_hi = kmean.astype(bf16)
        kmean_lo = (kmean - kmean_hi.astype(f32)).astype(bf16)
        for h in range(2):
            mine = (rowi >= HEAD_DIM) if h else (rowi < HEAD_DIM)
            qpad = jnp.where(mine, qpair, jnp.zeros_like(qpair))
            gates[pp, h] = _dot(kmean_hi, qpad) + _dot(kmean_lo, qpad)
            for qb in range(2):
                c = slot[pp, qb, h]
                rhs_ref[c, 0:HEAD_PAIR, :] = qpad[:, qb * blk:(qb + 1) * blk]
                rhs_ref[c, HEAD_PAIR:HEAD_PAIR + nblk, :] = jnp.zeros((nblk, blk), bf16)
                rhs_ref[c, HEAD_PAIR + nblk:HEAD_PAIR + K_FEAT, :] = arow_ref[pp, h]
                acc_ref[c] = jnp.zeros((V_ROWS, blk), f32)
                m_ref[c] = jnp.full((1, blk), M_INIT, f32)

    def key_max(s):
        slabs = s.shape[0] // (tile // MAX_SLABS)
        smax = jnp.max(s.reshape(slabs, tile // MAX_SLABS, blk), axis=0) if slabs > 1 else s
        return jnp.max(smax, axis=0, keepdims=True)

    buf_a, buf_b = (sa_ref, mta_ref), (sb_ref, mtb_ref)

    def scores(i, buf):
        s_ref, mt_ref = buf
        k0 = pl.multiple_of(i * tile, tile)
        for pp in range(MOBA_STEP_PAIRS):
            kt = kaug_ref[0, pp, pl.ds(k0, tile), :]
            for qb in range(2):
                for h in range(2):
                    c = slot[pp, qb, h]
                    s = _dot(kt, rhs_ref[c])
                    s_ref[c] = s
                    mt_ref[c] = key_max(s)

    scores(0, buf_a)

    for (pp, h), gate in gates.items():
        g = jnp.where(past, gate, NEG)
        sel = bidx == own
        for _ in range(MOBA_TOPK):
            mx = jnp.max(g, axis=0, keepdims=True)
            idx = jnp.min(jnp.where(g == mx, bidx, nblk), axis=0, keepdims=True)
            pick = bidx == idx
            sel = sel | (pick & past)
            g = jnp.where(pick, -jnp.inf, g)
        maskbias = jnp.where(sel, 0.0, NEG).astype(bf16)
        for qb in range(2):
            c = slot[pp, qb, h]
            mb = maskbias[:, qb * blk:(qb + 1) * blk]
            rhs_ref[c, HEAD_PAIR:HEAD_PAIR + nblk, :] = mb
            tile_max = []
            for kb in range(tile // blk):
                s = sa_ref[c, kb * blk:(kb + 1) * blk, :] + mb[kb:kb + 1, :].astype(f32)
                sa_ref[c, kb * blk:(kb + 1) * blk, :] = s
                tile_max.append(key_max(s))
            mta_ref[c] = functools.reduce(jnp.maximum, tile_max)

    def update(i, buf, diagonal):
        s_ref, mt_ref = buf
        k0 = pl.multiple_of(i * tile, tile)
        for c, (pp, qb, h) in enumerate(chains):
            s = s_ref[c]
            if diagonal:
                s = s + cmask_ref[qb]
                tile_max = key_max(s)
            else:
                tile_max = mt_ref[c]
            rel = tile * (i - n2) - blk * qb
            shift = slope_ref[2 * (hp0 + pp) + h] * rel.astype(f32)
            m_old = m_ref[c]
            m_new = jnp.maximum(m_old, tile_max + shift)
            p = jnp.exp2(s - (m_new - shift)).astype(bf16)
            pv = _dot(vaug_ref[0, 2 * pp + h, :, pl.ds(k0, tile)], p)
            acc_ref[c] = jnp.exp2(m_old - m_new) * acc_ref[c] + pv
            m_ref[c] = m_new

    npair = (ntile - 1) // 2

    def body(i2, carry):
        t = 2 * i2
        scores(t + 1, buf_b)
        update(t, buf_a, False)
        scores(t + 2, buf_a)
        update(t + 1, buf_b, False)
        return carry

    lax.fori_loop(0, npair, body, 0)
    t_end = 2 * npair

    @pl.when(ntile - t_end == 1)
    def _():
        update(t_end, buf_a, True)

    @pl.when(ntile - t_end == 2)
    def _():
        scores(t_end + 1, buf_b)
        update(t_end, buf_a, False)
        update(t_end + 1, buf_b, True)

    for c, (pp, qb, h) in enumerate(chains):
        acc = acc_ref[c]
        row0 = pp * HEAD_PAIR + h * HEAD_DIM
        o_ref[0, row0:row0 + HEAD_DIM, qb * blk:(qb + 1) * blk] = (
            acc[0:HEAD_DIM] / acc[HEAD_DIM:HEAD_DIM + 1]).astype(bf16)


def _moba(slopes, qt, kaug, vaug, ksum, arows, cmask):
    b, _, s = qt.shape
    nblk = s // MOBA_BLOCK
    pairs = MOBA_HEADS // 2
    assert nblk + arows.shape[2] == K_FEAT and nblk % 2 == 0 and pairs % MOBA_STEP_PAIRS == 0
    npp = MOBA_STEP_PAIRS
    nchain = 4 * npp
    return pl.pallas_call(
        functools.partial(_moba_kernel, nblk=nblk),
        out_shape=jax.ShapeDtypeStruct((b, MOBA_WIDTH, s), bf16),
        grid=(b, pairs // npp, nblk // 2),
        in_specs=[
            pl.BlockSpec(memory_space=pltpu.SMEM),
            pl.BlockSpec((1, npp * HEAD_PAIR, MOBA_TILE), lambda bi, hp, n2: (bi, hp, n2)),
            pl.BlockSpec((1, npp, s, HEAD_PAIR + K_FEAT), lambda bi, hp, n2: (bi, hp, 0, 0)),
            pl.BlockSpec((1, 2 * npp, V_ROWS, s), lambda bi, hp, n2: (bi, hp, 0, 0)),
            pl.BlockSpec((1, nblk, npp * HEAD_PAIR), lambda bi, hp, n2: (bi, 0, hp)),
            pl.BlockSpec((npp, 2, K_FEAT - nblk, MOBA_BLOCK), lambda bi, hp, n2: (hp, 0, 0, 0)),
            _resident((2, MOBA_TILE, MOBA_BLOCK)),
        ],
        out_specs=pl.BlockSpec((1, npp * HEAD_PAIR, MOBA_TILE), lambda bi, hp, n2: (bi, hp, n2)),
        scratch_shapes=[
            pltpu.VMEM((nchain, HEAD_PAIR + K_FEAT, MOBA_BLOCK), bf16),
            pltpu.VMEM((nchain, MOBA_TILE, MOBA_BLOCK), f32),
            pltpu.VMEM((nchain, MOBA_TILE, MOBA_BLOCK), f32),
            pltpu.VMEM((nchain, 1, MOBA_BLOCK), f32),
            pltpu.VMEM((nchain, 1, MOBA_BLOCK), f32),
            pltpu.VMEM((nchain, V_ROWS, MOBA_BLOCK), f32),
            pltpu.VMEM((nchain, 1, MOBA_BLOCK), f32),
        ],
        compiler_params=pltpu.CompilerParams(
            dimension_semantics=("parallel", "parallel", "arbitrary"), vmem_limit_bytes=VMEM_LIMIT),
        name="moba",
    )(slopes, qt, kaug, vaug, ksum.reshape(b, nblk, MOBA_WIDTH), arows, cmask)


def _merge_kernel(x_ref, g_ref, d0_ref, d1_ref, d2_ref, ybt_ref, wgate_ref, wua_ref, wub_ref, wo_ref, o_ref,
                  nat_ref, *, tm):
    rows = tm // ROW_GROUPS

    def token_order(g, c, part):
        d_ref, dil = (d0_ref, d1_ref, d2_ref)[g], DSA_GROUPS[g][1]
        per = rows // dil
        lanes = slice(c * LANES, (c + 1) * LANES)
        if dil == 1:
            return d_ref[0, 0, part * rows:(part + 1) * rows, lanes]
        for r in range(dil):
            nat_ref[g, c, pl.ds(part * rows + r, per, stride=dil), :] = (
                d_ref[0, r, part * per:(part + 1) * per, lanes])
        return nat_ref[g, c, part * rows:(part + 1) * rows, :]

    for part in range(ROW_GROUPS):
        sl = slice(part * rows, (part + 1) * rows)
        pairs = []
        for hp in range(2):
            outs = [token_order(g, 2 * hp, part) for g in range(len(DSA_GROUPS))]
            lses = [token_order(g, 2 * hp + 1, part) for g in range(len(DSA_GROUPS))]
            mx = jnp.maximum(jnp.maximum(lses[0], lses[1]), lses[2])
            es = [jnp.exp2(l - mx) for l in lses]
            den = es[0] + es[1] + es[2]
            pairs.append((es[0] * outs[0] + es[1] * outs[1] + es[2] * outs[2]) / den)
        ya = jnp.concatenate(pairs, axis=1).astype(bf16)
        x = x_ref[0, sl, :]
        h = _rmsnorm(x, g_ref[...]).astype(bf16)
        gates = jax.nn.sigmoid(_dot(h, wgate_ref[...]))
        ta = _dot(ya, wua_ref[...])
        tb = _dot_tn(ybt_ref[0, :, sl], wub_ref[...])
        merged = gates[:, 0:D_MODEL] * ta + gates[:, D_MODEL:] * tb
        o_ref[0, sl, :] = x + _dot(merged.astype(bf16), wo_ref[...])


def _merge(x3d, gain, dsa_outs, ybt, wgate, wua, wub, wo, *, tm):
    b, s, _ = x3d.shape
    tok = lambda w: pl.BlockSpec((1, tm, w), lambda bi, i: (bi, i, 0))
    grp = lambda dil: pl.BlockSpec((1, dil, tm // dil, 4 * LANES), lambda bi, i: (bi, 0, i, 0))
    return pl.pallas_call(
        functools.partial(_merge_kernel, tm=tm),
        out_shape=jax.ShapeDtypeStruct((b, s, D_MODEL), f32),
        grid=(b, s // tm),
        in_specs=[
            tok(D_MODEL), _resident((1, D_MODEL)), *[grp(dil) for _, dil in DSA_GROUPS],
            pl.BlockSpec((1, MOBA_WIDTH, tm), lambda bi, i: (bi, 0, i)),
            _resident((D_MODEL, 2 * D_MODEL)),
            _resident((DSA_OUT, D_MODEL)),
            _resident((MOBA_WIDTH, D_MODEL)),
            _resident((D_MODEL, D_MODEL)),
        ],
        out_specs=tok(D_MODEL),
        scratch_shapes=[pltpu.VMEM((len(DSA_GROUPS), 4, tm, LANES), f32)],
        compiler_params=pltpu.CompilerParams(
            dimension_semantics=("parallel", "parallel"), vmem_limit_bytes=VMEM_LIMIT),
        name="merge",
    )(x3d, gain, *dsa_outs, ybt, wgate, wua, wub, wo)


def _alibi_slopes(n):
    return 2.0 ** (-8.0 * np.arange(1, n + 1, dtype=np.float64) / n)


def _split3(x):
    hi = x.astype(bf16)
    r1 = x - hi.astype(f32)
    mid = r1.astype(bf16)
    lo = (r1 - mid.astype(f32)).astype(bf16)
    return hi, mid, lo


def _moba_constants(s):
    nblk = s // MOBA_BLOCK
    n_arow = K_FEAT - nblk
    pos = np.arange(s)
    kfeat = np.zeros((s, K_FEAT), np.float32)
    kfeat[pos, pos // MOBA_BLOCK] = 1.0
    kfeat[:, nblk:nblk + 3] = 1.0
    kfeat[:, nblk + 3:nblk + 6] = (pos % MOBA_BLOCK)[:, None]
    kfeat[:, nblk + 6:nblk + 9] = ((pos // MOBA_BLOCK) % 2)[:, None]
    slopes = jnp.asarray(_alibi_slopes(MOBA_HEADS) * LOG2E, f32)
    off = jnp.arange(MOBA_BLOCK, dtype=f32)
    wide = lambda v: jnp.broadcast_to(v[:, None], (MOBA_HEADS, MOBA_BLOCK))
    qterm = _split3(-slopes[:, None] * off[None, :])
    sterm = _split3(wide(slopes))
    bterm = _split3(wide(slopes * MOBA_BLOCK))
    rows = jnp.stack(list(qterm) + list(sterm) + list(bterm), axis=1)
    rows = jnp.concatenate(
        [rows, jnp.zeros((MOBA_HEADS, n_arow - 9, MOBA_BLOCK), bf16)], axis=1)
    arows = rows.reshape(MOBA_HEADS // 2, 2, n_arow, MOBA_BLOCK)
    r = jnp.arange(MOBA_TILE)[None, :, None]
    c = jnp.arange(MOBA_BLOCK)[None, None, :] + MOBA_BLOCK * jnp.arange(2)[:, None, None]
    cmask = jnp.where(r <= c, 0.0, NEG).astype(f32)
    return jnp.asarray(kfeat, bf16), arows, slopes, cmask


def _dsa_bias(group, dilation):
    slopes = _alibi_slopes(DSA_HEADS)[group * DSA_HEADS_PER_GROUP:(group + 1) * DSA_HEADS_PER_GROUP]
    slopes = jnp.asarray(slopes * LOG2E * dilation, f32).reshape(2, 2, 1, 1, 1)
    qi = jnp.arange(DSA_TQ)[:, None]
    kj = jnp.arange(DSA_BAND)[None, :]
    delta = jnp.stack([qi - kj, qi + DSA_BLK - kj])[None, None]
    valid = (delta >= 0) & (delta <= DSA_BLK)
    return jnp.where(valid, -slopes * delta.astype(f32), NEG)


def _layer(x, norm_ffn1, ffn1_gate, ffn1_up, ffn1_down, norm_mix, w_in, w_up_a, w_up_b, w_out,
           norm_ffn2, ffn2_gate, ffn2_up, ffn2_down, gain_final, *, final_norm):
    b, s, _ = x.shape
    tm = 512
    tm_ffn = 512
    row = lambda g: g.reshape(1, D_MODEL).astype(f32)
    x1 = _ffn(x.reshape(b * s, D_MODEL), row(norm_ffn1), ffn1_gate.astype(bf16), ffn1_up.astype(bf16),
              ffn1_down.astype(bf16), row(gain_final), final_norm=False, tm=tm_ffn).reshape(b, s, D_MODEL)

    scale = HEAD_DIM ** -0.5
    o = 0
    parts = []
    for width in (DSA_WIDTH, DSA_WIDTH, DSA_WIDTH, MOBA_WIDTH, MOBA_WIDTH, MOBA_WIDTH, 2 * D_MODEL):
        parts.append(w_in[:, o:o + width])
        o += width
    wqa, wka, wva, wqb, wkb, wvb, wgate = parts
    qscale = scale * LOG2E
    grp = lambda w, g: w[:, g * DSA_OUT:(g + 1) * DSA_OUT]
    wa = jnp.concatenate(
        [jnp.concatenate([grp(wqa, g) * qscale, grp(wka, g), grp(wva, g)], axis=1)
         for g in range(len(DSA_GROUPS))], axis=1).astype(bf16)
    wqvt = jnp.concatenate([wqb * qscale, wvb], axis=1).T.astype(bf16)
    kfeat, arows, slopes_b, cmask = _moba_constants(s)
    vpad = jnp.zeros((V_ROWS - HEAD_DIM, tm), bf16).at[0].set(1.0)
    a0, a1, a2, kaug, qt, vaug, ksum = _proj(
        x1, row(norm_mix), wa, wkb.astype(bf16), wqvt, kfeat, vpad, tm=tm)

    dsa_outs = [
        _dsa_group(a, _dsa_bias(gi, dil), group=gi)
        for gi, (a, (_, dil)) in enumerate(zip((a0, a1, a2), DSA_GROUPS))
    ]
    ybt = _moba(slopes_b, qt, kaug, vaug, ksum, arows, cmask)
    x2 = _merge(x1, row(norm_mix), dsa_outs, ybt, wgate.astype(bf16), w_up_a.astype(bf16),
                w_up_b.astype(bf16), w_out.astype(bf16), tm=tm)
    x3 = _ffn(x2.reshape(b * s, D_MODEL), row(norm_ffn2), ffn2_gate.astype(bf16), ffn2_up.astype(bf16),
              ffn2_down.astype(bf16), row(gain_final), final_norm=final_norm, tm=tm_ffn)
    return x3.reshape(b, s, D_MODEL)


def kernel(x, norm_ffn1, ffn1_gate, ffn1_up, ffn1_down, norm_mix, w_in, w_up_a, w_up_b, w_out,
           norm_ffn2, ffn2_gate, ffn2_up, ffn2_down, norm_final):
    depth = norm_ffn1.shape[0]
    for layer in range(depth):
        last = layer == depth - 1
        x = _layer(x, norm_ffn1[layer], ffn1_gate[layer], ffn1_up[layer], ffn1_down[layer],
                   norm_mix[layer], w_in[layer], w_up_a[layer], w_up_b[layer], w_out[layer],
                   norm_ffn2[layer], ffn2_gate[layer], ffn2_up[layer], ffn2_down[layer],
                   norm_final, final_norm=last)
    return x
```

```python
import functools

import numpy as np
import jax
import jax.numpy as jnp
from jax import lax
from jax.experimental import pallas as pl
from jax.experimental.pallas import tpu as pltpu

D_MODEL = 1024
HEAD_DIM = 64
DSA_GROUPS = ((128, 1), (512, 4), (2048, 16))
DSA_HEADS_PER_GROUP = 4
DSA_HEADS = DSA_HEADS_PER_GROUP * len(DSA_GROUPS)
MOBA_HEADS = 8
MOBA_BLOCK = 256
MOBA_TOPK = 3
D_FF = ((8 * D_MODEL // 3 + 127) // 128) * 128
DSA_WIDTH = DSA_HEADS * HEAD_DIM
MOBA_WIDTH = MOBA_HEADS * HEAD_DIM
DSA_OUT = DSA_HEADS_PER_GROUP * HEAD_DIM
EPS = 1e-6
NEG = -1e30

LANES = 128
HEAD_PAIR = 2 * HEAD_DIM
DSA_BLK = 128
DSA_TQ = DSA_BLK
DSA_BAND = DSA_TQ + DSA_BLK
DSA_STAGE_TILES = 2
DSA_GROUP_COLS = 3 * DSA_OUT
MOBA_TILE = 2 * MOBA_BLOCK
MAX_SLABS = 8
MOBA_STEP_PAIRS = 2
LOG2E = 1.4426950408889634
M_INIT = -3.0e38
V_ROWS = 80
K_FEAT = 128
VMEM_LIMIT = 56 * 1024 * 1024
ROW_GROUPS = 2
BF16_ROWS = 16

f32 = jnp.float32
bf16 = jnp.bfloat16


def _rmsnorm(x, g):
    return x * lax.rsqrt(jnp.mean(x * x, axis=-1, keepdims=True) + EPS) * g


def _dot(a, b):
    return jnp.dot(a, b, preferred_element_type=f32)


def _dot_nt(a, b):
    return lax.dot_general(a, b, (((1,), (1,)), ((), ())), preferred_element_type=f32)


def _dot_tn(a, b):
    return lax.dot_general(a, b, (((0,), (0,)), ((), ())), preferred_element_type=f32)


def _resident(shape):
    nd = len(shape)
    return pl.BlockSpec(shape, lambda *_: (0,) * nd, pipeline_mode=pl.Buffered(1))


def _cast_rows(side_in, side_out, plan):
    for o_ref, (src, lo, hi, scale) in zip(side_out, plan):
        w = side_in[src][:, lo:hi]
        if scale is not None:
            w = w * side_in[scale][...]
        o_ref[...] = w.astype(bf16)


def _ffn_kernel(x_ref, g_ref, wg_ref, wu_ref, wd_ref, gf_ref, *rest, final_norm, n_side, plan):
    side_in, (o_ref, *side_out) = rest[:n_side], rest[n_side:]
    _cast_rows(side_in, side_out, plan)
    rows = x_ref.shape[0] // ROW_GROUPS
    for part in range(ROW_GROUPS):
        sl = slice(part * rows, (part + 1) * rows)
        x = x_ref[sl, :]
        h = _rmsnorm(x, g_ref[...]).astype(bf16)
        gate = _dot(h, wg_ref[...])
        up = _dot(h, wu_ref[...])
        act = (gate * jax.nn.sigmoid(gate) * up).astype(bf16)
        y = x + 0.5 * _dot(act, wd_ref[...])
        if final_norm:
            y = _rmsnorm(y, gf_ref[...])
        o_ref[sl, :] = y


def _ffn(x2d, gain, w_gate, w_up, w_down, gain_final, *, final_norm, tm, w_in=None, qscale=None):
    n = x2d.shape[0]
    steps = n // tm
    side_args, side_specs, side_shapes, side_out_specs, plan = [], [], [], [], ()
    if w_in is not None:
        rows = w_in.shape[0] // steps
        assert rows * steps == w_in.shape[0] and rows % BF16_ROWS == 0
        block = lambda width: pl.BlockSpec((rows, width), lambda i: (i, 0))
        side_args = [w_in, qscale]
        side_specs = [block(w_in.shape[1]), _resident(qscale.shape)]
        plan = ((0, 0, 3 * DSA_WIDTH, 1),
                (0, 3 * DSA_WIDTH + MOBA_WIDTH, 3 * DSA_WIDTH + 2 * MOBA_WIDTH, None),
                (0, 3 * DSA_WIDTH + 3 * MOBA_WIDTH, w_in.shape[1], None))
        for _, lo, hi, _ in plan:
            side_shapes.append(jax.ShapeDtypeStruct((w_in.shape[0], hi - lo), bf16))
            side_out_specs.append(block(hi - lo))
    out = pl.pallas_call(
        functools.partial(_ffn_kernel, final_norm=final_norm, n_side=len(side_args), plan=plan),
        out_shape=(jax.ShapeDtypeStruct((n, D_MODEL), f32), *side_shapes),
        grid=(steps,),
        in_specs=[
            pl.BlockSpec((tm, D_MODEL), lambda i: (i, 0)),
            _resident((1, D_MODEL)),
            _resident((D_MODEL, D_FF)),
            _resident((D_MODEL, D_FF)),
            _resident((D_FF, D_MODEL)),
            _resident((1, D_MODEL)),
            *side_specs,
        ],
        out_specs=(pl.BlockSpec((tm, D_MODEL), lambda i: (i, 0)), *side_out_specs),
        compiler_params=pltpu.CompilerParams(
            dimension_semantics=("parallel",), vmem_limit_bytes=VMEM_LIMIT),
        name="ffn_final" if final_norm else "ffn",
    )(x2d, gain, w_gate, w_up, w_down, gain_final, *side_args)
    return out if side_args else out[0]


def _proj_kernel(x_ref, g_ref, wa_ref, wkb_ref, wqvt_ref, kfeat_ref, vpad_ref, *rest, tm, n_side, plan):
    side_in, rest = rest[:n_side], rest[n_side:]
    a0_ref, a1_ref, a2_ref, kaug_ref, qt_ref, vaug_ref, ksum_ref = rest[:7]
    side_out, scr_ref = rest[7:-1], rest[-1]
    _cast_rows(side_in, side_out, plan)
    rows = tm // ROW_GROUPS
    tiles_per_group = DSA_GROUP_COLS // LANES
    for part in range(ROW_GROUPS):
        sl = slice(part * rows, (part + 1) * rows)
        h = _rmsnorm(x_ref[0, sl, :], g_ref[...]).astype(bf16)
        qkv = _dot(h, wa_ref[...])
        for g, (a_ref, (_, dil)) in enumerate(zip((a0_ref, a1_ref, a2_ref), DSA_GROUPS)):
            per = rows // dil
            for c in range(tiles_per_group):
                lo = (c // 2) * DSA_WIDTH + g * DSA_OUT + (c % 2) * LANES
                lanes = slice(c * LANES, (c + 1) * LANES)
                if dil == 1:
                    a_ref[0, 0, sl, lanes] = qkv[:, lo:lo + LANES].astype(bf16)
                    continue
                scr_ref[part, g, c] = qkv[:, lo:lo + LANES]
                for r in range(dil):
                    a_ref[0, r, part * per:(part + 1) * per, lanes] = (
                        scr_ref[part, g, c, pl.ds(r, per, stride=dil), :].astype(bf16))
        kb = _dot(h, wkb_ref[...])
        nb = rows // MOBA_BLOCK
        ksum_ref[part * nb:(part + 1) * nb] = (
            jnp.sum(kb.reshape(nb, MOBA_BLOCK, MOBA_WIDTH), axis=1)[:, None, :])
        kb = kb.astype(bf16)
        for p in range(MOBA_HEADS // 2):
            kaug_ref[0, p, sl, 0:HEAD_PAIR] = kb[:, p * HEAD_PAIR:(p + 1) * HEAD_PAIR]
            kaug_ref[0, p, sl, HEAD_PAIR:HEAD_PAIR + K_FEAT] = kfeat_ref[sl, :]
        qv = _dot_nt(wqvt_ref[...], h)
        qt_ref[0, :, sl] = qv[0:MOBA_WIDTH].astype(bf16)
        for hd in range(MOBA_HEADS):
            lo = MOBA_WIDTH + hd * HEAD_DIM
            vaug_ref[0, hd, 0:HEAD_DIM, sl] = qv[lo:lo + HEAD_DIM].astype(bf16)
            vaug_ref[0, hd, HEAD_DIM:V_ROWS, sl] = vpad_ref[:, sl]


def _proj(x3d, gain, wa, wkb, wqvt, kfeat, vpad, later_weights, *, tm):
    b, s, _ = x3d.shape
    nblk = s // MOBA_BLOCK
    steps = s // tm
    side_specs, side_shapes, side_out_specs, plan = [], [], [], []
    for k, w in enumerate(later_weights):
        rows = w.shape[0] // steps
        assert rows * steps == w.shape[0] and rows % BF16_ROWS == 0
        spec = pl.BlockSpec((rows, w.shape[1]), lambda bi, i: (i, 0))
        side_specs.append(spec)
        side_out_specs.append(spec)
        side_shapes.append(jax.ShapeDtypeStruct(w.shape, bf16))
        plan.append((k, 0, w.shape[1], None))
    grp_shapes = tuple(
        jax.ShapeDtypeStruct((b, dil, s // dil, DSA_GROUP_COLS), bf16) for _, dil in DSA_GROUPS)
    grp_specs = tuple(
        pl.BlockSpec((1, dil, tm // dil, DSA_GROUP_COLS), lambda bi, i: (bi, 0, i, 0))
        for _, dil in DSA_GROUPS)
    out_shape = grp_shapes + (
        jax.ShapeDtypeStruct((b, MOBA_HEADS // 2, s, HEAD_PAIR + K_FEAT), bf16),
        jax.ShapeDtypeStruct((b, MOBA_WIDTH, s), bf16),
        jax.ShapeDtypeStruct((b, MOBA_HEADS, V_ROWS, s), bf16),
        jax.ShapeDtypeStruct((b * nblk, 1, MOBA_WIDTH), f32),
    )
    per = tm // MOBA_BLOCK
    return pl.pallas_call(
        functools.partial(_proj_kernel, tm=tm, n_side=len(later_weights), plan=tuple(plan)),
        out_shape=out_shape + tuple(side_shapes),
        grid=(b, steps),
        in_specs=[
            pl.BlockSpec((1, tm, D_MODEL), lambda bi, i: (bi, i, 0)),
            _resident((1, D_MODEL)),
            _resident((D_MODEL, 3 * DSA_WIDTH)),
            _resident((D_MODEL, MOBA_WIDTH)),
            _resident((2 * MOBA_WIDTH, D_MODEL)),
            pl.BlockSpec((tm, K_FEAT), lambda bi, i: (i, 0)),
            _resident((V_ROWS - HEAD_DIM, tm)),
            *side_specs,
        ],
        out_specs=grp_specs + (
            pl.BlockSpec((1, MOBA_HEADS // 2, tm, HEAD_PAIR + K_FEAT), lambda bi, i: (bi, 0, i, 0)),
            pl.BlockSpec((1, MOBA_WIDTH, tm), lambda bi, i: (bi, 0, i)),
            pl.BlockSpec((1, MOBA_HEADS, V_ROWS, tm), lambda bi, i: (bi, 0, 0, i)),
            pl.BlockSpec((per, 1, MOBA_WIDTH), lambda bi, i: (bi * (s // tm) + i, 0, 0)),
        ) + tuple(side_out_specs),
        scratch_shapes=[pltpu.VMEM(
            (ROW_GROUPS, len(DSA_GROUPS), DSA_GROUP_COLS // LANES, tm // ROW_GROUPS, LANES), f32)],
        compiler_params=pltpu.CompilerParams(
            dimension_semantics=("parallel", "parallel"), vmem_limit_bytes=VMEM_LIMIT),
        name="proj",
    )(x3d, gain, wa, wkb, wqvt, kfeat, vpad, *later_weights)


def _dsa_kernel(q_ref, k_ref, v_ref, bias_ref, o_ref, sa_ref, sb_ref, *, nstage, seg_tiles):
    tq, band = DSA_TQ, DSA_BAND
    first = lax.broadcasted_iota(jnp.int32, (tq, LANES), 1) < HEAD_DIM
    chains = [(u, h) for u in range(DSA_STAGE_TILES) for h in range(2)]

    def rows_of(t):
        return pl.ds(pl.multiple_of(t * tq, tq), tq)

    def seg_start(t):
        return (t % seg_tiles) == 0

    def band_of(t):
        start = jnp.where(seg_start(t), t * tq, t * tq - DSA_BLK)
        return pl.ds(pl.multiple_of(start, DSA_BLK), band)

    def scores(stage, s_ref):
        for u in range(DSA_STAGE_TILES):
            t = stage * DSA_STAGE_TILES + u
            q = q_ref[0, rows_of(t), :]
            kband = k_ref[0, band_of(t), :]
            for h in range(2):
                qm = jnp.where(first if h == 0 else jnp.logical_not(first), q, jnp.zeros_like(q))
                s_ref[2 * u + h] = _dot_nt(qm, kband)

    def finish(stage, s_ref):
        for u in range(DSA_STAGE_TILES):
            t = stage * DSA_STAGE_TILES + u
            variant = jnp.where(seg_start(t), 0, 1)
            vband = v_ref[0, band_of(t), :]
            outs, lses = [], []
            for h in range(2):
                s = s_ref[2 * u + h] + bias_ref[0, h, variant]
                m = jnp.max(s, axis=-1, keepdims=True)
                p = jnp.exp2(s - m)
                l = jnp.sum(p, axis=-1, keepdims=True)
                o = _dot(p.astype(bf16), vband)
                outs.append(o / l)
                lses.append(jnp.broadcast_to(m + jnp.log(l) * LOG2E, (tq, LANES)))
            o_ref[0, rows_of(t), 0:LANES] = jnp.where(first, outs[0], outs[1])
            o_ref[0, rows_of(t), LANES:2 * LANES] = jnp.where(first, lses[0], lses[1])

    zero = jnp.int32(0)
    scores(zero, sa_ref)

    def body(i2, carry):
        st = 2 * i2
        scores(st + 1, sb_ref)
        finish(st, sa_ref)
        scores(st + 2, sa_ref)
        finish(st + 1, sb_ref)
        return carry

    lax.fori_loop(0, nstage // 2 - 1, body, 0)
    st_end = zero + (nstage - 2)
    scores(st_end + 1, sb_ref)
    finish(st_end, sa_ref)
    finish(st_end + 1, sb_ref)


def _dsa_group(a, bias, *, group):
    b, dil, length, _ = a.shape
    s = dil * length
    pairs = DSA_HEADS_PER_GROUP // 2
    seg_tiles = length // DSA_TQ
    nstage = s // (DSA_TQ * DSA_STAGE_TILES)
    assert seg_tiles >= 2 and length % DSA_TQ == 0 and nstage % 2 == 0
    flat = a.reshape(b, s, DSA_GROUP_COLS)
    nchain = 2 * DSA_STAGE_TILES
    col = lambda part: (lambda bi, hp: (bi, 0, part * pairs + hp))
    out = pl.pallas_call(
        functools.partial(_dsa_kernel, nstage=nstage, seg_tiles=seg_tiles),
        out_shape=jax.ShapeDtypeStruct((b, s, 2 * pairs * LANES), f32),
        grid=(b, pairs),
        in_specs=[
            pl.BlockSpec((1, s, LANES), col(0)),
            pl.BlockSpec((1, s, LANES), col(1)),
            pl.BlockSpec((1, s, LANES), col(2)),
            pl.BlockSpec((1, 2, 2, DSA_TQ, DSA_BAND), lambda bi, hp: (hp, 0, 0, 0, 0)),
        ],
        out_specs=pl.BlockSpec((1, s, 2 * LANES), lambda bi, hp: (bi, 0, hp)),
        scratch_shapes=[pltpu.VMEM((nchain, DSA_TQ, DSA_BAND), f32),
                        pltpu.VMEM((nchain, DSA_TQ, DSA_BAND), f32)],
        compiler_params=pltpu.CompilerParams(
            dimension_semantics=("parallel", "parallel"), vmem_limit_bytes=VMEM_LIMIT),
        name=f"dsa_g{group}",
    )(flat, flat, flat, bias)
    return out.reshape(b, dil, length, 2 * pairs * LANES)


def _moba_kernel(slope_ref, qt_ref, kaug_ref, vaug_ref, ksum_ref, arow_ref, cmask_ref, o_ref,
                 rhs_ref, sa_ref, sb_ref, mta_ref, mtb_ref, acc_ref, m_ref, *, nblk):
    hp0 = pl.program_id(1) * MOBA_STEP_PAIRS
    n2 = pl.program_id(2)
    blk = MOBA_BLOCK
    tile = MOBA_TILE
    ntile = n2 + 1
    chains = [(pp, qb, h) for pp in range(MOBA_STEP_PAIRS) for qb in range(2) for h in range(2)]
    slot = {chain: c for c, chain in enumerate(chains)}
    rowi = lax.broadcasted_iota(jnp.int32, (HEAD_PAIR, tile), 0)
    bidx = lax.broadcasted_iota(jnp.int32, (nblk, tile), 0)
    own = 2 * n2 + (lax.broadcasted_iota(jnp.int32, (nblk, tile), 1) >= blk).astype(jnp.int32)
    past = bidx < own

    gates = {}
    for pp in range(MOBA_STEP_PAIRS):
        qpair = qt_ref[0, pp * HEAD_PAIR:(pp + 1) * HEAD_PAIR, :]
        kmean = ksum_ref[0, :, pp * HEAD_PAIR:(pp + 1) * HEAD_PAIR] * (1.0 / blk)
        kmean_hi = kmean.astype(bf16)
        kmean_lo = (kmean - kmean_hi.astype(f32)).astype(bf16)
        for h in range(2):
            mine = (rowi >= HEAD_DIM) if h else (rowi < HEAD_DIM)
            qpad = jnp.where(mine, qpair, jnp.zeros_like(qpair))
            gates[pp, h] = _dot(kmean_hi, qpad) + _dot(kmean_lo, qpad)
            for qb in range(2):
                c = slot[pp, qb, h]
                rhs_ref[c, 0:HEAD_PAIR, :] = qpad[:, qb * blk:(qb + 1) * blk]
                rhs_ref[c, HEAD_PAIR:HEAD_PAIR + nblk, :] = jnp.zeros((nblk, blk), bf16)
                rhs_ref[c, HEAD_PAIR + nblk:HEAD_PAIR + K_FEAT, :] = arow_ref[pp, h]
                acc_ref[c] = jnp.zeros((V_ROWS, blk), f32)
                m_ref[c] = jnp.full((1, blk), M_INIT, f32)

    def key_max(s):
        slabs = s.shape[0] // (tile // MAX_SLABS)
        smax = jnp.max(s.reshape(slabs, tile // MAX_SLABS, blk), axis=0) if slabs > 1 else s
        return jnp.max(smax, axis=0, keepdims=True)

    buf_a, buf_b = (sa_ref, mta_ref), (sb_ref, mtb_ref)

    def scores(i, buf):
        s_ref, mt_ref = buf
        k0 = pl.multiple_of(i * tile, tile)
        for pp in range(MOBA_STEP_PAIRS):
            kt = kaug_ref[0, pp, pl.ds(k0, tile), :]
            for qb in range(2):
                for h in range(2):
                    c = slot[pp, qb, h]
                    s = _dot(kt, rhs_ref[c])
                    s_ref[c] = s
                    mt_ref[c] = key_max(s)

    scores(0, buf_a)

    for (pp, h), gate in gates.items():
        g = jnp.where(past, gate, NEG)
        sel = bidx == own
        for _ in range(MOBA_TOPK):
            mx = jnp.max(g, axis=0, keepdims=True)
            idx = jnp.min(jnp.where(g == mx, bidx, nblk), axis=0, keepdims=True)
            pick = bidx == idx
            sel = sel | (pick & past)
            g = jnp.where(pick, -jnp.inf, g)
        maskbias = jnp.where(sel, 0.0, NEG).astype(bf16)
        for qb in range(2):
            c = slot[pp, qb, h]
            mb = maskbias[:, qb * blk:(qb + 1) * blk]
            rhs_ref[c, HEAD_PAIR:HEAD_PAIR + nblk, :] = mb
            tile_max = []
            for kb in range(tile // blk):
                s = sa_ref[c, kb * blk:(kb + 1) * blk, :] + mb[kb:kb + 1, :].astype(f32)
                sa_ref[c, kb * blk:(kb + 1) * blk, :] = s
                tile_max.append(key_max(s))
            mta_ref[c] = functools.reduce(jnp.maximum, tile_max)

    def update(i, buf, diagonal):
        s_ref, mt_ref = buf
        k0 = pl.multiple_of(i * tile, tile)
        for c, (pp, qb, h) in enumerate(chains):
            s = s_ref[c]
            if diagonal:
                s = s + cmask_ref[qb]
                tile_max = key_max(s)
            else:
                tile_max = mt_ref[c]
            rel = tile * (i - n2) - blk * qb
            shift = slope_ref[2 * (hp0 + pp) + h] * rel.astype(f32)
            m_old = m_ref[c]
            m_new = jnp.maximum(m_old, tile_max + shift)
            p = jnp.exp2(s - (m_new - shift)).astype(bf16)
            pv = _dot(vaug_ref[0, 2 * pp + h, :, pl.ds(k0, tile)], p)
            acc_ref[c] = jnp.exp2(m_old - m_new) * acc_ref[c] + pv
            m_ref[c] = m_new

    npair = (ntile - 1) // 2

    def body(i2, carry):
        t = 2 * i2
        scores(t + 1, buf_b)
        update(t, buf_a, False)
        scores(t + 2, buf_a)
        update(t + 1, buf_b, False)
        return carry

    lax.fori_loop(0, npair, body, 0)
    t_end = 2 * npair

    @pl.when(ntile - t_end == 1)
    def _():
        update(t_end, buf_a, True)

    @pl.when(ntile - t_end == 2)
    def _():
        scores(t_end + 1, buf_b)
        update(t_end, buf_a, False)
        update(t_end + 1, buf_b, True)

    for c, (pp, qb, h) in enumerate(chains):
        acc = acc_ref[c]
        row0 = pp * HEAD_PAIR + h * HEAD_DIM
        o_ref[0, row0:row0 + HEAD_DIM, qb * blk:(qb + 1) * blk] = (
            acc[0:HEAD_DIM] / acc[HEAD_DIM:HEAD_DIM + 1]).astype(bf16)


def _moba(slopes, qt, kaug, vaug, ksum, arows, cmask):
    b, _, s = qt.shape
    nblk = s // MOBA_BLOCK
    pairs = MOBA_HEADS // 2
    assert nblk + arows.shape[2] == K_FEAT and nblk % 2 == 0 and pairs % MOBA_STEP_PAIRS == 0
    npp = MOBA_STEP_PAIRS
    nchain = 4 * npp
    return pl.pallas_call(
        functools.partial(_moba_kernel, nblk=nblk),
        out_shape=jax.ShapeDtypeStruct((b, MOBA_WIDTH, s), bf16),
        grid=(b, pairs // npp, nblk // 2),
        in_specs=[
            pl.BlockSpec(memory_space=pltpu.SMEM),
            pl.BlockSpec((1, npp * HEAD_PAIR, MOBA_TILE), lambda bi, hp, n2: (bi, hp, n2)),
            pl.BlockSpec((1, npp, s, HEAD_PAIR + K_FEAT), lambda bi, hp, n2: (bi, hp, 0, 0)),
            pl.BlockSpec((1, 2 * npp, V_ROWS, s), lambda bi, hp, n2: (bi, hp, 0, 0)),
            pl.BlockSpec((1, nblk, npp * HEAD_PAIR), lambda bi, hp, n2: (bi, 0, hp)),
            pl.BlockSpec((npp, 2, K_FEAT - nblk, MOBA_BLOCK), lambda bi, hp, n2: (hp, 0, 0, 0)),
            _resident((2, MOBA_TILE, MOBA_BLOCK)),
        ],
        out_specs=pl.BlockSpec((1, npp * HEAD_PAIR, MOBA_TILE), lambda bi, hp, n2: (bi, hp, n2)),
        scratch_shapes=[
            pltpu.VMEM((nchain, HEAD_PAIR + K_FEAT, MOBA_BLOCK), bf16),
            pltpu.VMEM((nchain, MOBA_TILE, MOBA_BLOCK), f32),
            pltpu.VMEM((nchain, MOBA_TILE, MOBA_BLOCK), f32),
            pltpu.VMEM((nchain, 1, MOBA_BLOCK), f32),
            pltpu.VMEM((nchain, 1, MOBA_BLOCK), f32),
            pltpu.VMEM((nchain, V_ROWS, MOBA_BLOCK), f32),
            pltpu.VMEM((nchain, 1, MOBA_BLOCK), f32),
        ],
        compiler_params=pltpu.CompilerParams(
            dimension_semantics=("parallel", "parallel", "arbitrary"), vmem_limit_bytes=VMEM_LIMIT),
        name="moba",
    )(slopes, qt, kaug, vaug, ksum.reshape(b, nblk, MOBA_WIDTH), arows, cmask)


def _merge_kernel(x_ref, g_ref, d0_ref, d1_ref, d2_ref, ybt_ref, wgate_ref, wua_ref, wub_ref, wo_ref, o_ref,
                  nat_ref, *, tm):
    rows = tm // ROW_GROUPS

    def token_order(g, c, part):
        d_ref, dil = (d0_ref, d1_ref, d2_ref)[g], DSA_GROUPS[g][1]
        per = rows // dil
        lanes = slice(c * LANES, (c + 1) * LANES)
        if dil == 1:
            return d_ref[0, 0, part * rows:(part + 1) * rows, lanes]
        for r in range(dil):
            nat_ref[g, c, pl.ds(part * rows + r, per, stride=dil), :] = (
                d_ref[0, r, part * per:(part + 1) * per, lanes])
        return nat_ref[g, c, part * rows:(part + 1) * rows, :]

    for part in range(ROW_GROUPS):
        sl = slice(part * rows, (part + 1) * rows)
        pairs = []
        for hp in range(2):
            outs = [token_order(g, 2 * hp, part) for g in range(len(DSA_GROUPS))]
            lses = [token_order(g, 2 * hp + 1, part) for g in range(len(DSA_GROUPS))]
            mx = jnp.maximum(jnp.maximum(lses[0], lses[1]), lses[2])
            es = [jnp.exp2(l - mx) for l in lses]
            den = es[0] + es[1] + es[2]
            pairs.append((es[0] * outs[0] + es[1] * outs[1] + es[2] * outs[2]) / den)
        ya = jnp.concatenate(pairs, axis=1).astype(bf16)
        x = x_ref[0, sl, :]
        h = _rmsnorm(x, g_ref[...]).astype(bf16)
        gates = jax.nn.sigmoid(_dot(h, wgate_ref[...]))
        ta = _dot(ya, wua_ref[...])
        tb = _dot_tn(ybt_ref[0, :, sl], wub_ref[...])
        merged = gates[:, 0:D_MODEL] * ta + gates[:, D_MODEL:] * tb
        o_ref[0, sl, :] = x + _dot(merged.astype(bf16), wo_ref[...])


def _merge(x3d, gain, dsa_outs, ybt, wgate, wua, wub, wo, *, tm):
    b, s, _ = x3d.shape
    tok = lambda w: pl.BlockSpec((1, tm, w), lambda bi, i: (bi, i, 0))
    grp = lambda dil: pl.BlockSpec((1, dil, tm // dil, 4 * LANES), lambda bi, i: (bi, 0, i, 0))
    return pl.pallas_call(
        functools.partial(_merge_kernel, tm=tm),
        out_shape=jax.ShapeDtypeStruct((b, s, D_MODEL), f32),
        grid=(b, s // tm),
        in_specs=[
            tok(D_MODEL), _resident((1, D_MODEL)), *[grp(dil) for _, dil in DSA_GROUPS],
            pl.BlockSpec((1, MOBA_WIDTH, tm), lambda bi, i: (bi, 0, i)),
            _resident((D_MODEL, 2 * D_MODEL)),
            _resident((DSA_OUT, D_MODEL)),
            _resident((MOBA_WIDTH, D_MODEL)),
            _resident((D_MODEL, D_MODEL)),
        ],
        out_specs=tok(D_MODEL),
        scratch_shapes=[pltpu.VMEM((len(DSA_GROUPS), 4, tm, LANES), f32)],
        compiler_params=pltpu.CompilerParams(
            dimension_semantics=("parallel", "parallel"), vmem_limit_bytes=VMEM_LIMIT),
        name="merge",
    )(x3d, gain, *dsa_outs, ybt, wgate, wua, wub, wo)


def _alibi_slopes(n):
    return 2.0 ** (-8.0 * np.arange(1, n + 1, dtype=np.float64) / n)


def _split3(x):
    hi = x.astype(bf16)
    r1 = x - hi.astype(f32)
    mid = r1.astype(bf16)
    lo = (r1 - mid.astype(f32)).astype(bf16)
    return hi, mid, lo


def _moba_constants(s):
    nblk = s // MOBA_BLOCK
    n_arow = K_FEAT - nblk
    pos = np.arange(s)
    kfeat = np.zeros((s, K_FEAT), np.float32)
    kfeat[pos, pos // MOBA_BLOCK] = 1.0
    kfeat[:, nblk:nblk + 3] = 1.0
    kfeat[:, nblk + 3:nblk + 6] = (pos % MOBA_BLOCK)[:, None]
    kfeat[:, nblk + 6:nblk + 9] = ((pos // MOBA_BLOCK) % 2)[:, None]
    slopes = jnp.asarray(_alibi_slopes(MOBA_HEADS) * LOG2E, f32)
    off = jnp.arange(MOBA_BLOCK, dtype=f32)
    wide = lambda v: jnp.broadcast_to(v[:, None], (MOBA_HEADS, MOBA_BLOCK))
    qterm = _split3(-slopes[:, None] * off[None, :])
    sterm = _split3(wide(slopes))
    bterm = _split3(wide(slopes * MOBA_BLOCK))
    rows = jnp.stack(list(qterm) + list(sterm) + list(bterm), axis=1)
    rows = jnp.concatenate(
        [rows, jnp.zeros((MOBA_HEADS, n_arow - 9, MOBA_BLOCK), bf16)], axis=1)
    arows = rows.reshape(MOBA_HEADS // 2, 2, n_arow, MOBA_BLOCK)
    r = jnp.arange(MOBA_TILE)[None, :, None]
    c = jnp.arange(MOBA_BLOCK)[None, None, :] + MOBA_BLOCK * jnp.arange(2)[:, None, None]
    cmask = jnp.where(r <= c, 0.0, NEG).astype(f32)
    return jnp.asarray(kfeat, bf16), arows, slopes, cmask


def _dsa_bias(group, dilation):
    slopes = _alibi_slopes(DSA_HEADS)[group * DSA_HEADS_PER_GROUP:(group + 1) * DSA_HEADS_PER_GROUP]
    slopes = jnp.asarray(slopes * LOG2E * dilation, f32).reshape(2, 2, 1, 1, 1)
    qi = jnp.arange(DSA_TQ)[:, None]
    kj = jnp.arange(DSA_BAND)[None, :]
    delta = jnp.stack([qi - kj, qi + DSA_BLK - kj])[None, None]
    valid = (delta >= 0) & (delta <= DSA_BLK)
    return jnp.where(valid, -slopes * delta.astype(f32), NEG)


def _layer(x, norm_ffn1, ffn1_gate, ffn1_up, ffn1_down, norm_mix, w_in, w_up_a, w_up_b, w_out,
           norm_ffn2, ffn2_gate, ffn2_up, ffn2_down, gain_final, *, final_norm):
    b, s, _ = x.shape
    tm = 512
    tm_ffn = 512
    row = lambda g: g.reshape(1, D_MODEL).astype(f32)
    qscale = HEAD_DIM ** -0.5 * LOG2E
    qa_scale = jnp.ones((1, 3 * DSA_WIDTH), f32).at[:, :DSA_WIDTH].set(qscale)
    x1, wa, wkb, wgate = _ffn(
        x.reshape(b * s, D_MODEL), row(norm_ffn1), ffn1_gate.astype(bf16), ffn1_up.astype(bf16),
        ffn1_down.astype(bf16), row(gain_final), final_norm=False, tm=tm_ffn, w_in=w_in, qscale=qa_scale)
    x1 = x1.reshape(b, s, D_MODEL)

    qb0 = 3 * DSA_WIDTH
    wqb = w_in[:, qb0:qb0 + MOBA_WIDTH]
    wvb = w_in[:, qb0 + 2 * MOBA_WIDTH:qb0 + 3 * MOBA_WIDTH]
    wqvt = jnp.concatenate([wqb * qscale, wvb], axis=1).T.astype(bf16)
    kfeat, arows, slopes_b, cmask = _moba_constants(s)
    vpad = jnp.zeros((V_ROWS - HEAD_DIM, tm), bf16).at[0].set(1.0)
    a0, a1, a2, kaug, qt, vaug, ksum, w2_gate, w2_up, w2_down = _proj(
        x1, row(norm_mix), wa, wkb, wqvt, kfeat, vpad, (ffn2_gate, ffn2_up, ffn2_down), tm=tm)

    dsa_outs = [
        _dsa_group(a, _dsa_bias(gi, dil), group=gi)
        for gi, (a, (_, dil)) in enumerate(zip((a0, a1, a2), DSA_GROUPS))
    ]
    ybt = _moba(slopes_b, qt, kaug, vaug, ksum, arows, cmask)
    x2 = _merge(x1, row(norm_mix), dsa_outs, ybt, wgate, w_up_a.astype(bf16),
                w_up_b.astype(bf16), w_out.astype(bf16), tm=tm)
    x3 = _ffn(x2.reshape(b * s, D_MODEL), row(norm_ffn2), w2_gate, w2_up, w2_down, row(gain_final),
              final_norm=final_norm, tm=tm_ffn)
    return x3.reshape(b, s, D_MODEL)


def kernel(x, norm_ffn1, ffn1_gate, ffn1_up, ffn1_down, norm_mix, w_in, w_up_a, w_up_b, w_out,
           norm_ffn2, ffn2_gate, ffn2_up, ffn2_down, norm_final):
    depth = norm_ffn1.shape[0]
    for layer in range(depth):
        last = layer == depth - 1
        x = _layer(x, norm_ffn1[layer], ffn1_gate[layer], ffn1_up[layer], ffn1_down[layer],
                   norm_mix[layer], w_in[layer], w_up_a[layer], w_up_b[layer], w_out[layer],
                   norm_ffn2[layer], ffn2_gate[layer], ffn2_up[layer], ffn2_down[layer],
                   norm_final, final_norm=last)
    return x
```

```python
import functools

import numpy as np
import jax
import jax.numpy as jnp
from jax import lax
from jax.experimental import pallas as pl
from jax.experimental.pallas import tpu as pltpu

D_MODEL = 1024
HEAD_DIM = 64
DSA_GROUPS = ((128, 1), (512, 4), (2048, 16))
DSA_HEADS_PER_GROUP = 4
DSA_HEADS = DSA_HEADS_PER_GROUP * len(DSA_GROUPS)
MOBA_HEADS = 8
MOBA_BLOCK = 256
MOBA_TOPK = 3
D_FF = ((8 * D_MODEL // 3 + 127) // 128) * 128
DSA_WIDTH = DSA_HEADS * HEAD_DIM
MOBA_WIDTH = MOBA_HEADS * HEAD_DIM
DSA_OUT = DSA_HEADS_PER_GROUP * HEAD_DIM
EPS = 1e-6
NEG = -1e30

LANES = 128
HEAD_PAIR = 2 * HEAD_DIM
DSA_BLK = 128
DSA_TQ = DSA_BLK
DSA_BAND = DSA_TQ + DSA_BLK
DSA_STAGE_TILES = 2
DSA_GROUP_COLS = 3 * DSA_OUT
MOBA_TILE = 2 * MOBA_BLOCK
MAX_SLABS = 8
MOBA_STEP_PAIRS = 2
LOG2E = 1.4426950408889634
M_INIT = -3.0e38
V_ROWS = 80
K_FEAT = 128
VMEM_LIMIT = 56 * 1024 * 1024
ROW_GROUPS = 2
BF16_ROWS = 16

f32 = jnp.float32
bf16 = jnp.bfloat16


def _rmsnorm(x, g):
    return x * lax.rsqrt(jnp.mean(x * x, axis=-1, keepdims=True) + EPS) * g


def _dot(a, b):
    return jnp.dot(a, b, preferred_element_type=f32)


def _dot_nt(a, b):
    return lax.dot_general(a, b, (((1,), (1,)), ((), ())), preferred_element_type=f32)


def _dot_tn(a, b):
    return lax.dot_general(a, b, (((0,), (0,)), ((), ())), preferred_element_type=f32)


def _resident(shape):
    nd = len(shape)
    return pl.BlockSpec(shape, lambda *_: (0,) * nd, pipeline_mode=pl.Buffered(1))


def _cast_rows(side_in, side_out, plan):
    for o_ref, (src, lo, hi, scale) in zip(side_out, plan):
        w = side_in[src][:, lo:hi]
        if isinstance(scale, int):
            w = w * side_in[scale][...]
        elif scale is not None:
            w = w * scale
        o_ref[...] = w.astype(bf16)


def _ffn_kernel(x_ref, g_ref, wg_ref, wu_ref, wd_ref, gf_ref, *rest, final_norm, n_side, plan):
    side_in, (o_ref, *side_out) = rest[:n_side], rest[n_side:]
    _cast_rows(side_in, side_out, plan)
    rows = x_ref.shape[0] // ROW_GROUPS
    for part in range(ROW_GROUPS):
        sl = slice(part * rows, (part + 1) * rows)
        x = x_ref[sl, :]
        h = _rmsnorm(x, g_ref[...]).astype(bf16)
        gate = _dot(h, wg_ref[...])
        up = _dot(h, wu_ref[...])
        act = (gate * jax.nn.sigmoid(gate) * up).astype(bf16)
        y = x + 0.5 * _dot(act, wd_ref[...])
        if final_norm:
            y = _rmsnorm(y, gf_ref[...])
        o_ref[sl, :] = y


def _ffn(x2d, gain, w_gate, w_up, w_down, gain_final, *, final_norm, tm, w_in=None, qa_scale=None,
         qb_scale=None):
    n = x2d.shape[0]
    steps = n // tm
    side_args, side_specs, side_shapes, side_out_specs, plan = [], [], [], [], ()
    if w_in is not None:
        rows = w_in.shape[0] // steps
        assert rows * steps == w_in.shape[0] and rows % BF16_ROWS == 0
        block = lambda width: pl.BlockSpec((rows, width), lambda i: (i, 0))
        side_args = [w_in, qa_scale]
        side_specs = [block(w_in.shape[1]), _resident(qa_scale.shape)]
        qb0 = 3 * DSA_WIDTH
        plan = ((0, 0, qb0, 1),
                (0, qb0, qb0 + MOBA_WIDTH, qb_scale),
                (0, qb0 + MOBA_WIDTH, qb0 + 2 * MOBA_WIDTH, None),
                (0, qb0 + 2 * MOBA_WIDTH, qb0 + 3 * MOBA_WIDTH, None),
                (0, qb0 + 3 * MOBA_WIDTH, w_in.shape[1], None))
        for _, lo, hi, _ in plan:
            side_shapes.append(jax.ShapeDtypeStruct((w_in.shape[0], hi - lo), bf16))
            side_out_specs.append(block(hi - lo))
    out = pl.pallas_call(
        functools.partial(_ffn_kernel, final_norm=final_norm, n_side=len(side_args), plan=plan),
        out_shape=(jax.ShapeDtypeStruct((n, D_MODEL), f32), *side_shapes),
        grid=(steps,),
        in_specs=[
            pl.BlockSpec((tm, D_MODEL), lambda i: (i, 0)),
            _resident((1, D_MODEL)),
            _resident((D_MODEL, D_FF)),
            _resident((D_MODEL, D_FF)),
            _resident((D_FF, D_MODEL)),
            _resident((1, D_MODEL)),
            *side_specs,
        ],
        out_specs=(pl.BlockSpec((tm, D_MODEL), lambda i: (i, 0)), *side_out_specs),
        compiler_params=pltpu.CompilerParams(
            dimension_semantics=("parallel",), vmem_limit_bytes=VMEM_LIMIT),
        name="ffn_final" if final_norm else "ffn",
    )(x2d, gain, w_gate, w_up, w_down, gain_final, *side_args)
    return out if side_args else out[0]


def _proj_kernel(x_ref, g_ref, wa_ref, wqb_ref, wkb_ref, wvb_ref, kfeat_ref, vpad_ref, *rest,
                 tm, n_side, plan):
    side_in, rest = rest[:n_side], rest[n_side:]
    a0_ref, a1_ref, a2_ref, kaug_ref, qt_ref, vaug_ref, ksum_ref = rest[:7]
    side_out, scr_ref = rest[7:-1], rest[-1]

    @pl.when(pl.program_id(0) == 0)
    def _():
        _cast_rows(side_in, side_out, plan)

    rows = tm // ROW_GROUPS
    tiles_per_group = DSA_GROUP_COLS // LANES
    for part in range(ROW_GROUPS):
        sl = slice(part * rows, (part + 1) * rows)
        h = _rmsnorm(x_ref[0, sl, :], g_ref[...]).astype(bf16)
        qkv = _dot(h, wa_ref[...])
        for g, (a_ref, (_, dil)) in enumerate(zip((a0_ref, a1_ref, a2_ref), DSA_GROUPS)):
            per = rows // dil
            for c in range(tiles_per_group):
                lo = (c // 2) * DSA_WIDTH + g * DSA_OUT + (c % 2) * LANES
                lanes = slice(c * LANES, (c + 1) * LANES)
                if dil == 1:
                    a_ref[0, 0, sl, lanes] = qkv[:, lo:lo + LANES].astype(bf16)
                    continue
                scr_ref[part, g, c] = qkv[:, lo:lo + LANES]
                for r in range(dil):
                    a_ref[0, r, part * per:(part + 1) * per, lanes] = (
                        scr_ref[part, g, c, pl.ds(r, per, stride=dil), :].astype(bf16))
        kb = _dot(h, wkb_ref[...])
        nb = rows // MOBA_BLOCK
        ksum_ref[part * nb:(part + 1) * nb] = (
            jnp.sum(kb.reshape(nb, MOBA_BLOCK, MOBA_WIDTH), axis=1)[:, None, :])
        kb = kb.astype(bf16)
        for p in range(MOBA_HEADS // 2):
            kaug_ref[0, p, sl, 0:HEAD_PAIR] = kb[:, p * HEAD_PAIR:(p + 1) * HEAD_PAIR]
            kaug_ref[0, p, sl, HEAD_PAIR:HEAD_PAIR + K_FEAT] = kfeat_ref[sl, :]
        qt_ref[0, :, sl] = _dot(h, wqb_ref[...]).T.astype(bf16)
        vt = _dot(h, wvb_ref[...]).T
        for hd in range(MOBA_HEADS):
            vaug_ref[0, hd, 0:HEAD_DIM, sl] = vt[hd * HEAD_DIM:(hd + 1) * HEAD_DIM].astype(bf16)
            vaug_ref[0, hd, HEAD_DIM:V_ROWS, sl] = vpad_ref[:, sl]


def _proj(x3d, gain, wa, wqb, wkb, wvb, kfeat, vpad, later_weights, *, tm):
    b, s, _ = x3d.shape
    nblk = s // MOBA_BLOCK
    steps = s // tm
    side_specs, side_shapes, side_out_specs, plan = [], [], [], []
    parked = lambda bi, i: (jnp.where(bi == 0, i, steps - 1), 0)
    for k, w in enumerate(later_weights):
        rows = w.shape[0] // steps
        assert rows * steps == w.shape[0] and rows % BF16_ROWS == 0
        spec = pl.BlockSpec((rows, w.shape[1]), parked)
        side_specs.append(spec)
        side_out_specs.append(spec)
        side_shapes.append(jax.ShapeDtypeStruct(w.shape, bf16))
        plan.append((k, 0, w.shape[1], None))
    grp_shapes = tuple(
        jax.ShapeDtypeStruct((b, dil, s // dil, DSA_GROUP_COLS), bf16) for _, dil in DSA_GROUPS)
    grp_specs = tuple(
        pl.BlockSpec((1, dil, tm // dil, DSA_GROUP_COLS), lambda bi, i: (bi, 0, i, 0))
        for _, dil in DSA_GROUPS)
    out_shape = grp_shapes + (
        jax.ShapeDtypeStruct((b, MOBA_HEADS // 2, s, HEAD_PAIR + K_FEAT), bf16),
        jax.ShapeDtypeStruct((b, MOBA_WIDTH, s), bf16),
        jax.ShapeDtypeStruct((b, MOBA_HEADS, V_ROWS, s), bf16),
        jax.ShapeDtypeStruct((b * nblk, 1, MOBA_WIDTH), f32),
    )
    per = tm // MOBA_BLOCK
    return pl.pallas_call(
        functools.partial(_proj_kernel, tm=tm, n_side=len(later_weights), plan=tuple(plan)),
        out_shape=out_shape + tuple(side_shapes),
        grid=(b, steps),
        in_specs=[
            pl.BlockSpec((1, tm, D_MODEL), lambda bi, i: (bi, i, 0)),
            _resident((1, D_MODEL)),
            _resident((D_MODEL, 3 * DSA_WIDTH)),
            _resident((D_MODEL, MOBA_WIDTH)),
            _resident((D_MODEL, MOBA_WIDTH)),
            _resident((D_MODEL, MOBA_WIDTH)),
            pl.BlockSpec((tm, K_FEAT), lambda bi, i: (i, 0)),
            _resident((V_ROWS - HEAD_DIM, tm)),
            *side_specs,
        ],
        out_specs=grp_specs + (
            pl.BlockSpec((1, MOBA_HEADS // 2, tm, HEAD_PAIR + K_FEAT), lambda bi, i: (bi, 0, i, 0)),
            pl.BlockSpec((1, MOBA_WIDTH, tm), lambda bi, i: (bi, 0, i)),
            pl.BlockSpec((1, MOBA_HEADS, V_ROWS, tm), lambda bi, i: (bi, 0, 0, i)),
            pl.BlockSpec((per, 1, MOBA_WIDTH), lambda bi, i: (bi * (s // tm) + i, 0, 0)),
        ) + tuple(side_out_specs),
        scratch_shapes=[pltpu.VMEM(
            (ROW_GROUPS, len(DSA_GROUPS), DSA_GROUP_COLS // LANES, tm // ROW_GROUPS, LANES), f32)],
        compiler_params=pltpu.CompilerParams(
            dimension_semantics=("arbitrary", "arbitrary"), vmem_limit_bytes=VMEM_LIMIT),
        name="proj",
    )(x3d, gain, wa, wqb, wkb, wvb, kfeat, vpad, *later_weights)


def _dsa_kernel(q_ref, k_ref, v_ref, bias_ref, o_ref, sa_ref, sb_ref, *, nstage, seg_tiles):
    tq, band = DSA_TQ, DSA_BAND
    first = lax.broadcasted_iota(jnp.int32, (tq, LANES), 1) < HEAD_DIM
    chains = [(u, h) for u in range(DSA_STAGE_TILES) for h in range(2)]

    def rows_of(t):
        return pl.ds(pl.multiple_of(t * tq, tq), tq)

    def seg_start(t):
        return (t % seg_tiles) == 0

    def band_of(t):
        start = jnp.where(seg_start(t), t * tq, t * tq - DSA_BLK)
        return pl.ds(pl.multiple_of(start, DSA_BLK), band)

    def scores(stage, s_ref):
        for u in range(DSA_STAGE_TILES):
            t = stage * DSA_STAGE_TILES + u
            q = q_ref[0, rows_of(t), :]
            kband = k_ref[0, band_of(t), :]
            for h in range(2):
                qm = jnp.where(first if h == 0 else jnp.logical_not(first), q, jnp.zeros_like(q))
                s_ref[2 * u + h] = _dot_nt(qm, kband)

    def finish(stage, s_ref):
        for u in range(DSA_STAGE_TILES):
            t = stage * DSA_STAGE_TILES + u
            variant = jnp.where(seg_start(t), 0, 1)
            vband = v_ref[0, band_of(t), :]
            outs, lses = [], []
            for h in range(2):
                s = s_ref[2 * u + h] + bias_ref[0, h, variant]
                m = jnp.max(s, axis=-1, keepdims=True)
                p = jnp.exp2(s - m)
                l = jnp.sum(p, axis=-1, keepdims=True)
                o = _dot(p.astype(bf16), vband)
                outs.append(o / l)
                lses.append(jnp.broadcast_to(m + jnp.log(l) * LOG2E, (tq, LANES)))
            o_ref[0, rows_of(t), 0:LANES] = jnp.where(first, outs[0], outs[1])
            o_ref[0, rows_of(t), LANES:2 * LANES] = jnp.where(first, lses[0], lses[1])

    zero = jnp.int32(0)
    scores(zero, sa_ref)

    def body(i2, carry):
        st = 2 * i2
        scores(st + 1, sb_ref)
        finish(st, sa_ref)
        scores(st + 2, sa_ref)
        finish(st + 1, sb_ref)
        return carry

    lax.fori_loop(0, nstage // 2 - 1, body, 0)
    st_end = zero + (nstage - 2)
    scores(st_end + 1, sb_ref)
    finish(st_end, sa_ref)
    finish(st_end + 1, sb_ref)


def _dsa_group(a, bias, *, group):
    b, dil, length, _ = a.shape
    s = dil * length
    pairs = DSA_HEADS_PER_GROUP // 2
    seg_tiles = length // DSA_TQ
    nstage = s // (DSA_TQ * DSA_STAGE_TILES)
    assert seg_tiles >= 2 and length % DSA_TQ == 0 and nstage % 2 == 0
    flat = a.reshape(b, s, DSA_GROUP_COLS)
    nchain = 2 * DSA_STAGE_TILES
    col = lambda part: (lambda bi, hp: (bi, 0, part * pairs + hp))
    out = pl.pallas_call(
        functools.partial(_dsa_kernel, nstage=nstage, seg_tiles=seg_tiles),
        out_shape=jax.ShapeDtypeStruct((b, s, 2 * pairs * LANES), f32),
        grid=(b, pairs),
        in_specs=[
            pl.BlockSpec((1, s, LANES), col(0)),
            pl.BlockSpec((1, s, LANES), col(1)),
            pl.BlockSpec((1, s, LANES), col(2)),
            pl.BlockSpec((1, 2, 2, DSA_TQ, DSA_BAND), lambda bi, hp: (hp, 0, 0, 0, 0)),
        ],
        out_specs=pl.BlockSpec((1, s, 2 * LANES), lambda bi, hp: (bi, 0, hp)),
        scratch_shapes=[pltpu.VMEM((nchain, DSA_TQ, DSA_BAND), f32),
                        pltpu.VMEM((nchain, DSA_TQ, DSA_BAND), f32)],
        compiler_params=pltpu.CompilerParams(
            dimension_semantics=("parallel", "parallel"), vmem_limit_bytes=VMEM_LIMIT),
        name=f"dsa_g{group}",
    )(flat, flat, flat, bias)
    return out.reshape(b, dil, length, 2 * pairs * LANES)


def _moba_kernel(slope_ref, qt_ref, kaug_ref, vaug_ref, ksum_ref, arow_ref, cmask_ref, o_ref,
                 rhs_ref, sa_ref, sb_ref, mta_ref, mtb_ref, acc_ref, m_ref, *, nblk):
    hp0 = pl.program_id(1) * MOBA_STEP_PAIRS
    n2 = pl.program_id(2)
    blk = MOBA_BLOCK
    tile = MOBA_TILE
    ntile = n2 + 1
    chains = [(pp, qb, h) for pp in range(MOBA_STEP_PAIRS) for qb in range(2) for h in range(2)]
    slot = {chain: c for c, chain in enumerate(chains)}
    rowi = lax.broadcasted_iota(jnp.int32, (HEAD_PAIR, tile), 0)
    bidx = lax.broadcasted_iota(jnp.int32, (nblk, tile), 0)
    own = 2 * n2 + (lax.broadcasted_iota(jnp.int32, (nblk, tile), 1) >= blk).astype(jnp.int32)
    past = bidx < own

    gates = {}
    for pp in range(MOBA_STEP_PAIRS):
        qpair = qt_ref[0, pp * HEAD_PAIR:(pp + 1) * HEAD_PAIR, :]
        kmean = ksum_ref[0, :, pp * HEAD_PAIR:(pp + 1) * HEAD_PAIR] * (1.0 / blk)
        kmean_hi = kmean.astype(bf16)
        kmean_lo = (kmean - kmean_hi.astype(f32)).astype(bf16)
        for h in range(2):
            mine = (rowi >= HEAD_DIM) if h else (rowi < HEAD_DIM)
            qpad = jnp.where(mine, qpair, jnp.zeros_like(qpair))
            gates[pp, h] = _dot(kmean_hi, qpad) + _dot(kmean_lo, qpad)
            for qb in range(2):
                c = slot[pp, qb, h]
                rhs_ref[c, 0:HEAD_PAIR, :] = qpad[:, qb * blk:(qb + 1) * blk]
                rhs_ref[c, HEAD_PAIR:HEAD_PAIR + nblk, :] = jnp.zeros((nblk, blk), bf16)
                rhs_ref[c, HEAD_PAIR + nblk:HEAD_PAIR + K_FEAT, :] = arow_ref[pp, h]
                acc_ref[c] = jnp.zeros((V_ROWS, blk), f32)
                m_ref[c] = jnp.full((1, blk), M_INIT, f32)

    def key_max(s):
        slabs = s.shape[0] // (tile // MAX_SLABS)
        smax = jnp.max(s.reshape(slabs, tile // MAX_SLABS, blk), axis=0) if slabs > 1 else s
        return jnp.max(smax, axis=0, keepdims=True)

    buf_a, buf_b = (sa_ref, mta_ref), (sb_ref, mtb_ref)

    def scores(i, buf):
        s_ref, mt_ref = buf
        k0 = pl.multiple_of(i * tile, tile)
        for pp in range(MOBA_STEP_PAIRS):
            kt = kaug_ref[0, pp, pl.ds(k0, tile), :]
            for qb in range(2):
                for h in range(2):
                    c = slot[pp, qb, h]
                    s = _dot(kt, rhs_ref[c])
                    s_ref[c] = s
                    mt_ref[c] = key_max(s)

    scores(0, buf_a)

    for (pp, h), gate in gates.items():
        g = jnp.where(past, gate, NEG)
        sel = bidx == own
        for _ in range(MOBA_TOPK):
            mx = jnp.max(g, axis=0, keepdims=True)
            idx = jnp.min(jnp.where(g == mx, bidx, nblk), axis=0, keepdims=True)
            pick = bidx == idx
            sel = sel | (pick & past)
            g = jnp.where(pick, -jnp.inf, g)
        maskbias = jnp.where(sel, 0.0, NEG).astype(bf16)
        for qb in range(2):
            c = slot[pp, qb, h]
            mb = maskbias[:, qb * blk:(qb + 1) * blk]
            rhs_ref[c, HEAD_PAIR:HEAD_PAIR + nblk, :] = mb
            tile_max = []
            for kb in range(tile // blk):
                s = sa_ref[c, kb * blk:(kb + 1) * blk, :] + mb[kb:kb + 1, :].astype(f32)
                sa_ref[c, kb * blk:(kb + 1) * blk, :] = s
                tile_max.append(key_max(s))
            mta_ref[c] = functools.reduce(jnp.maximum, tile_max)

    def update(i, buf, diagonal):
        s_ref, mt_ref = buf
        k0 = pl.multiple_of(i * tile, tile)
        for c, (pp, qb, h) in enumerate(chains):
            s = s_ref[c]
            if diagonal:
                s = s + cmask_ref[qb]
                tile_max = key_max(s)
            else:
                tile_max = mt_ref[c]
            rel = tile * (i - n2) - blk * qb
            shift = slope_ref[2 * (hp0 + pp) + h] * rel.astype(f32)
            m_old = m_ref[c]
            m_new = jnp.maximum(m_old, tile_max + shift)
            p = jnp.exp2(s - (m_new - shift)).astype(bf16)
            pv = _dot(vaug_ref[0, 2 * pp + h, :, pl.ds(k0, tile)], p)
            acc_ref[c] = jnp.exp2(m_old - m_new) * acc_ref[c] + pv
            m_ref[c] = m_new

    npair = (ntile - 1) // 2

    def body(i2, carry):
        t = 2 * i2
        scores(t + 1, buf_b)
        update(t, buf_a, False)
        scores(t + 2, buf_a)
        update(t + 1, buf_b, False)
        return carry

    lax.fori_loop(0, npair, body, 0)
    t_end = 2 * npair

    @pl.when(ntile - t_end == 1)
    def _():
        update(t_end, buf_a, True)

    @pl.when(ntile - t_end == 2)
    def _():
        scores(t_end + 1, buf_b)
        update(t_end, buf_a, False)
        update(t_end + 1, buf_b, True)

    for c, (pp, qb, h) in enumerate(chains):
        acc = acc_ref[c]
        row0 = pp * HEAD_PAIR + h * HEAD_DIM
        o_ref[0, row0:row0 + HEAD_DIM, qb * blk:(qb + 1) * blk] = (
            acc[0:HEAD_DIM] / acc[HEAD_DIM:HEAD_DIM + 1]).astype(bf16)


def _moba(slopes, qt, kaug, vaug, ksum, arows, cmask):
    b, _, s = qt.shape
    nblk = s // MOBA_BLOCK
    pairs = MOBA_HEADS // 2
    assert nblk + arows.shape[2] == K_FEAT and nblk % 2 == 0 and pairs % MOBA_STEP_PAIRS == 0
    npp = MOBA_STEP_PAIRS
    nchain = 4 * npp
    return pl.pallas_call(
        functools.partial(_moba_kernel, nblk=nblk),
        out_shape=jax.ShapeDtypeStruct((b, MOBA_WIDTH, s), bf16),
        grid=(b, pairs // npp, nblk // 2),
        in_specs=[
            pl.BlockSpec(memory_space=pltpu.SMEM),
            pl.BlockSpec((1, npp * HEAD_PAIR, MOBA_TILE), lambda bi, hp, n2: (bi, hp, n2)),
            pl.BlockSpec((1, npp, s, HEAD_PAIR + K_FEAT), lambda bi, hp, n2: (bi, hp, 0, 0)),
            pl.BlockSpec((1, 2 * npp, V_ROWS, s), lambda bi, hp, n2: (bi, hp, 0, 0)),
            pl.BlockSpec((1, nblk, npp * HEAD_PAIR), lambda bi, hp, n2: (bi, 0, hp)),
            pl.BlockSpec((npp, 2, K_FEAT - nblk, MOBA_BLOCK), lambda bi, hp, n2: (hp, 0, 0, 0)),
            _resident((2, MOBA_TILE, MOBA_BLOCK)),
        ],
        out_specs=pl.BlockSpec((1, npp * HEAD_PAIR, MOBA_TILE), lambda bi, hp, n2: (bi, hp, n2)),
        scratch_shapes=[
            pltpu.VMEM((nchain, HEAD_PAIR + K_FEAT, MOBA_BLOCK), bf16),
            pltpu.VMEM((nchain, MOBA_TILE, MOBA_BLOCK), f32),
            pltpu.VMEM((nchain, MOBA_TILE, MOBA_BLOCK), f32),
            pltpu.VMEM((nchain, 1, MOBA_BLOCK), f32),
            pltpu.VMEM((nchain, 1, MOBA_BLOCK), f32),
            pltpu.VMEM((nchain, V_ROWS, MOBA_BLOCK), f32),
            pltpu.VMEM((nchain, 1, MOBA_BLOCK), f32),
        ],
        compiler_params=pltpu.CompilerParams(
            dimension_semantics=("parallel", "parallel", "arbitrary"), vmem_limit_bytes=VMEM_LIMIT),
        name="moba",
    )(slopes, qt, kaug, vaug, ksum.reshape(b, nblk, MOBA_WIDTH), arows, cmask)


def _merge_kernel(x_ref, g_ref, d0_ref, d1_ref, d2_ref, ybt_ref, wgate_ref, wua_ref, wub_ref, wo_ref, o_ref,
                  nat_ref, *, tm):
    rows = tm // ROW_GROUPS

    def token_order(g, c, part):
        d_ref, dil = (d0_ref, d1_ref, d2_ref)[g], DSA_GROUPS[g][1]
        per = rows // dil
        lanes = slice(c * LANES, (c + 1) * LANES)
        if dil == 1:
            return d_ref[0, 0, part * rows:(part + 1) * rows, lanes]
        for r in range(dil):
            nat_ref[g, c, pl.ds(part * rows + r, per, stride=dil), :] = (
                d_ref[0, r, part * per:(part + 1) * per, lanes])
        return nat_ref[g, c, part * rows:(part + 1) * rows, :]

    for part in range(ROW_GROUPS):
        sl = slice(part * rows, (part + 1) * rows)
        pairs = []
        for hp in range(2):
            outs = [token_order(g, 2 * hp, part) for g in range(len(DSA_GROUPS))]
            lses = [token_order(g, 2 * hp + 1, part) for g in range(len(DSA_GROUPS))]
            mx = jnp.maximum(jnp.maximum(lses[0], lses[1]), lses[2])
            es = [jnp.exp2(l - mx) for l in lses]
            den = es[0] + es[1] + es[2]
            pairs.append((es[0] * outs[0] + es[1] * outs[1] + es[2] * outs[2]) / den)
        ya = jnp.concatenate(pairs, axis=1).astype(bf16)
        x = x_ref[0, sl, :]
        h = _rmsnorm(x, g_ref[...]).astype(bf16)
        gates = jax.nn.sigmoid(_dot(h, wgate_ref[...]))
        ta = _dot(ya, wua_ref[...])
        tb = _dot_tn(ybt_ref[0, :, sl], wub_ref[...])
        merged = gates[:, 0:D_MODEL] * ta + gates[:, D_MODEL:] * tb
        o_ref[0, sl, :] = x + _dot(merged.astype(bf16), wo_ref[...])


def _merge(x3d, gain, dsa_outs, ybt, wgate, wua, wub, wo, *, tm):
    b, s, _ = x3d.shape
    tok = lambda w: pl.BlockSpec((1, tm, w), lambda bi, i: (bi, i, 0))
    grp = lambda dil: pl.BlockSpec((1, dil, tm // dil, 4 * LANES), lambda bi, i: (bi, 0, i, 0))
    return pl.pallas_call(
        functools.partial(_merge_kernel, tm=tm),
        out_shape=jax.ShapeDtypeStruct((b, s, D_MODEL), f32),
        grid=(b, s // tm),
        in_specs=[
            tok(D_MODEL), _resident((1, D_MODEL)), *[grp(dil) for _, dil in DSA_GROUPS],
            pl.BlockSpec((1, MOBA_WIDTH, tm), lambda bi, i: (bi, 0, i)),
            _resident((D_MODEL, 2 * D_MODEL)),
            _resident((DSA_OUT, D_MODEL)),
            _resident((MOBA_WIDTH, D_MODEL)),
            _resident((D_MODEL, D_MODEL)),
        ],
        out_specs=tok(D_MODEL),
        scratch_shapes=[pltpu.VMEM((len(DSA_GROUPS), 4, tm, LANES), f32)],
        compiler_params=pltpu.CompilerParams(
            dimension_semantics=("parallel", "parallel"), vmem_limit_bytes=VMEM_LIMIT),
        name="merge",
    )(x3d, gain, *dsa_outs, ybt, wgate, wua, wub, wo)


def _alibi_slopes(n):
    return 2.0 ** (-8.0 * np.arange(1, n + 1, dtype=np.float64) / n)


def _split3(x):
    hi = x.astype(bf16)
    r1 = x - hi.astype(f32)
    mid = r1.astype(bf16)
    lo = (r1 - mid.astype(f32)).astype(bf16)
    return hi, mid, lo


def _moba_constants(s):
    nblk = s // MOBA_BLOCK
    n_arow = K_FEAT - nblk
    pos = np.arange(s)
    kfeat = np.zeros((s, K_FEAT), np.float32)
    kfeat[pos, pos // MOBA_BLOCK] = 1.0
    kfeat[:, nblk:nblk + 3] = 1.0
    kfeat[:, nblk + 3:nblk + 6] = (pos % MOBA_BLOCK)[:, None]
    kfeat[:, nblk + 6:nblk + 9] = ((pos // MOBA_BLOCK) % 2)[:, None]
    slopes = jnp.asarray(_alibi_slopes(MOBA_HEADS) * LOG2E, f32)
    off = jnp.arange(MOBA_BLOCK, dtype=f32)
    wide = lambda v: jnp.broadcast_to(v[:, None], (MOBA_HEADS, MOBA_BLOCK))
    qterm = _split3(-slopes[:, None] * off[None, :])
    sterm = _split3(wide(slopes))
    bterm = _split3(wide(slopes * MOBA_BLOCK))
    rows = jnp.stack(list(qterm) + list(sterm) + list(bterm), axis=1)
    rows = jnp.concatenate(
        [rows, jnp.zeros((MOBA_HEADS, n_arow - 9, MOBA_BLOCK), bf16)], axis=1)
    arows = rows.reshape(MOBA_HEADS // 2, 2, n_arow, MOBA_BLOCK)
    r = jnp.arange(MOBA_TILE)[None, :, None]
    c = jnp.arange(MOBA_BLOCK)[None, None, :] + MOBA_BLOCK * jnp.arange(2)[:, None, None]
    cmask = jnp.where(r <= c, 0.0, NEG).astype(f32)
    return jnp.asarray(kfeat, bf16), arows, slopes, cmask


def _dsa_bias(group, dilation):
    slopes = _alibi_slopes(DSA_HEADS)[group * DSA_HEADS_PER_GROUP:(group + 1) * DSA_HEADS_PER_GROUP]
    slopes = jnp.asarray(slopes * LOG2E * dilation, f32).reshape(2, 2, 1, 1, 1)
    qi = jnp.arange(DSA_TQ)[:, None]
    kj = jnp.arange(DSA_BAND)[None, :]
    delta = jnp.stack([qi - kj, qi + DSA_BLK - kj])[None, None]
    valid = (delta >= 0) & (delta <= DSA_BLK)
    return jnp.where(valid, -slopes * delta.astype(f32), NEG)


def _layer(x, norm_ffn1, ffn1_gate, ffn1_up, ffn1_down, norm_mix, w_in, w_up_a, w_up_b, w_out,
           norm_ffn2, ffn2_gate, ffn2_up, ffn2_down, gain_final, *, final_norm):
    b, s, _ = x.shape
    tm = 512
    tm_ffn = 512
    row = lambda g: g.reshape(1, D_MODEL).astype(f32)
    qscale = HEAD_DIM ** -0.5 * LOG2E
    qa_scale = jnp.ones((1, 3 * DSA_WIDTH), f32).at[:, :DSA_WIDTH].set(qscale)
    x1, wa, wqb, wkb, wvb, wgate = _ffn(
        x.reshape(b * s, D_MODEL), row(norm_ffn1), ffn1_gate.astype(bf16), ffn1_up.astype(bf16),
        ffn1_down.astype(bf16), row(gain_final), final_norm=False, tm=tm_ffn, w_in=w_in,
        qa_scale=qa_scale, qb_scale=float(qscale))
    x1 = x1.reshape(b, s, D_MODEL)

    kfeat, arows, slopes_b, cmask = _moba_constants(s)
    vpad = jnp.zeros((V_ROWS - HEAD_DIM, tm), bf16).at[0].set(1.0)
    a0, a1, a2, kaug, qt, vaug, ksum, w2_gate, w2_up, w2_down = _proj(
        x1, row(norm_mix), wa, wqb, wkb, wvb, kfeat, vpad, (ffn2_gate, ffn2_up, ffn2_down), tm=tm)

    dsa_outs = [
        _dsa_group(a, _dsa_bias(gi, dil), group=gi)
        for gi, (a, (_, dil)) in enumerate(zip((a0, a1, a2), DSA_GROUPS))
    ]
    ybt = _moba(slopes_b, qt, kaug, vaug, ksum, arows, cmask)
    x2 = _merge(x1, row(norm_mix), dsa_outs, ybt, wgate, w_up_a.astype(bf16),
                w_up_b.astype(bf16), w_out.astype(bf16), tm=tm)
    x3 = _ffn(x2.reshape(b * s, D_MODEL), row(norm_ffn2), w2_gate, w2_up, w2_down, row(gain_final),
              final_norm=final_norm, tm=tm_ffn)
    return x3.reshape(b, s, D_MODEL)


def kernel(x, norm_ffn1, ffn1_gate, ffn1_up, ffn1_down, norm_mix, w_in, w_up_a, w_up_b, w_out,
           norm_ffn2, ffn2_gate, ffn2_up, ffn2_down, norm_final):
    depth = norm_ffn1.shape[0]
    for layer in range(depth):
        last = layer == depth - 1
        x = _layer(x, norm_ffn1[layer], ffn1_gate[layer], ffn1_up[layer], ffn1_down[layer],
                   norm_mix[layer], w_in[layer], w_up_a[layer], w_up_b[layer], w_out[layer],
                   norm_ffn2[layer], ffn2_gate[layer], ffn2_up[layer], ffn2_down[layer],
                   norm_final, final_norm=last)
    return x
```

```python
import functools

import numpy as np
import jax
import jax.numpy as jnp
from jax import lax
from jax.experimental import pallas as pl
from jax.experimental.pallas import tpu as pltpu

D_MODEL = 1024
HEAD_DIM = 64
DSA_GROUPS = ((128, 1), (512, 4), (2048, 16))
DSA_HEADS_PER_GROUP = 4
DSA_HEADS = DSA_HEADS_PER_GROUP * len(DSA_GROUPS)
MOBA_HEADS = 8
MOBA_BLOCK = 256
MOBA_TOPK = 3
D_FF = ((8 * D_MODEL // 3 + 127) // 128) * 128
DSA_WIDTH = DSA_HEADS * HEAD_DIM
MOBA_WIDTH = MOBA_HEADS * HEAD_DIM
DSA_OUT = DSA_HEADS_PER_GROUP * HEAD_DIM
EPS = 1e-6
NEG = -1e30

LANES = 128
HEAD_PAIR = 2 * HEAD_DIM
DSA_BLK = 128
DSA_TQ = DSA_BLK
DSA_BAND = DSA_TQ + DSA_BLK
DSA_STAGE_TILES = 2
DSA_GROUP_COLS = 3 * DSA_OUT
MOBA_TILE = 2 * MOBA_BLOCK
MAX_SLABS = 8
MOBA_STEP_PAIRS = 2
LOG2E = 1.4426950408889634
M_INIT = -3.0e38
V_ROWS = 80
K_FEAT = 128
VMEM_LIMIT = 56 * 1024 * 1024
GROUP_ROWS = 256
BF16_ROWS = 16

f32 = jnp.float32
bf16 = jnp.bfloat16


def _rmsnorm(x, g):
    return x * lax.rsqrt(jnp.mean(x * x, axis=-1, keepdims=True) + EPS) * g


def _dot(a, b):
    return jnp.dot(a, b, preferred_element_type=f32)


def _dot_nt(a, b):
    return lax.dot_general(a, b, (((1,), (1,)), ((), ())), preferred_element_type=f32)


def _dot_tn(a, b):
    return lax.dot_general(a, b, (((0,), (0,)), ((), ())), preferred_element_type=f32)


def _resident(shape):
    nd = len(shape)
    return pl.BlockSpec(shape, lambda *_: (0,) * nd, pipeline_mode=pl.Buffered(1))


def _cast_rows(side_in, side_out, plan):
    for o_ref, (src, lo, hi, scale) in zip(side_out, plan):
        w = side_in[src][:, lo:hi]
        if isinstance(scale, int):
            w = w * side_in[scale][...]
        elif scale is not None:
            w = w * scale
        o_ref[...] = w.astype(bf16)


def _ffn_kernel(x_ref, g_ref, wg_ref, wu_ref, wd_ref, gf_ref, *rest, final_norm, n_side, plan):
    side_in, (o_ref, *side_out) = rest[:n_side], rest[n_side:]
    _cast_rows(side_in, side_out, plan)
    rows = GROUP_ROWS
    for part in range(x_ref.shape[0] // rows):
        sl = slice(part * rows, (part + 1) * rows)
        x = x_ref[sl, :]
        h = _rmsnorm(x, g_ref[...]).astype(bf16)
        gate = _dot(h, wg_ref[...])
        up = _dot(h, wu_ref[...])
        act = (gate * jax.nn.sigmoid(gate) * up).astype(bf16)
        y = x + 0.5 * _dot(act, wd_ref[...])
        if final_norm:
            y = _rmsnorm(y, gf_ref[...])
        o_ref[sl, :] = y


def _ffn(x2d, gain, w_gate, w_up, w_down, gain_final, *, final_norm, tm, w_in=None, qa_scale=None,
         qb_scale=None):
    n = x2d.shape[0]
    steps = n // tm
    side_args, side_specs, side_shapes, side_out_specs, plan = [], [], [], [], ()
    if w_in is not None:
        rows = w_in.shape[0] // steps
        assert rows * steps == w_in.shape[0] and rows % BF16_ROWS == 0
        block = lambda width: pl.BlockSpec((rows, width), lambda i: (i, 0))
        side_args = [w_in, qa_scale]
        side_specs = [block(w_in.shape[1]), _resident(qa_scale.shape)]
        qb0 = 3 * DSA_WIDTH
        plan = ((0, 0, qb0, 1),
                (0, qb0, qb0 + MOBA_WIDTH, qb_scale),
                (0, qb0 + MOBA_WIDTH, qb0 + 2 * MOBA_WIDTH, None),
                (0, qb0 + 2 * MOBA_WIDTH, qb0 + 3 * MOBA_WIDTH, None),
                (0, qb0 + 3 * MOBA_WIDTH, w_in.shape[1], None))
        for _, lo, hi, _ in plan:
            side_shapes.append(jax.ShapeDtypeStruct((w_in.shape[0], hi - lo), bf16))
            side_out_specs.append(block(hi - lo))
    out = pl.pallas_call(
        functools.partial(_ffn_kernel, final_norm=final_norm, n_side=len(side_args), plan=plan),
        out_shape=(jax.ShapeDtypeStruct((n, D_MODEL), f32), *side_shapes),
        grid=(steps,),
        in_specs=[
            pl.BlockSpec((tm, D_MODEL), lambda i: (i, 0)),
            _resident((1, D_MODEL)),
            _resident((D_MODEL, D_FF)),
            _resident((D_MODEL, D_FF)),
            _resident((D_FF, D_MODEL)),
            _resident((1, D_MODEL)),
            *side_specs,
        ],
        out_specs=(pl.BlockSpec((tm, D_MODEL), lambda i: (i, 0)), *side_out_specs),
        compiler_params=pltpu.CompilerParams(
            dimension_semantics=("parallel",), vmem_limit_bytes=VMEM_LIMIT),
        name="ffn_final" if final_norm else "ffn",
    )(x2d, gain, w_gate, w_up, w_down, gain_final, *side_args)
    return out if side_args else out[0]


def _proj_kernel(x_ref, g_ref, wa_ref, wqb_ref, wkb_ref, wvb_ref, kfeat_ref, vpad_ref, *rest,
                 tm, n_side, plan):
    side_in, rest = rest[:n_side], rest[n_side:]
    a0_ref, a1_ref, a2_ref, kaug_ref, qt_ref, vaug_ref, ksum_ref = rest[:7]
    side_out, scr_ref = rest[7:-1], rest[-1]

    @pl.when(pl.program_id(0) == 0)
    def _():
        _cast_rows(side_in, side_out, plan)

    rows = GROUP_ROWS
    tiles_per_group = DSA_GROUP_COLS // LANES
    for part in range(tm // rows):
        sl = slice(part * rows, (part + 1) * rows)
        h = _rmsnorm(x_ref[0, sl, :], g_ref[...]).astype(bf16)
        qkv = _dot(h, wa_ref[...])
        for g, (a_ref, (_, dil)) in enumerate(zip((a0_ref, a1_ref, a2_ref), DSA_GROUPS)):
            per = rows // dil
            for c in range(tiles_per_group):
                lo = (c // 2) * DSA_WIDTH + g * DSA_OUT + (c % 2) * LANES
                lanes = slice(c * LANES, (c + 1) * LANES)
                if dil == 1:
                    a_ref[0, 0, sl, lanes] = qkv[:, lo:lo + LANES].astype(bf16)
                    continue
                scr_ref[part, g, c] = qkv[:, lo:lo + LANES]
                for r in range(dil):
                    a_ref[0, r, part * per:(part + 1) * per, lanes] = (
                        scr_ref[part, g, c, pl.ds(r, per, stride=dil), :].astype(bf16))
        kb = _dot(h, wkb_ref[...])
        nb = rows // MOBA_BLOCK
        ksum_ref[part * nb:(part + 1) * nb] = (
            jnp.sum(kb.reshape(nb, MOBA_BLOCK, MOBA_WIDTH), axis=1)[:, None, :])
        kb = kb.astype(bf16)
        for p in range(MOBA_HEADS // 2):
            kaug_ref[0, p, sl, 0:HEAD_PAIR] = kb[:, p * HEAD_PAIR:(p + 1) * HEAD_PAIR]
            kaug_ref[0, p, sl, HEAD_PAIR:HEAD_PAIR + K_FEAT] = kfeat_ref[sl, :]
        qt_ref[0, :, sl] = _dot(h, wqb_ref[...]).T.astype(bf16)
        vt = _dot(h, wvb_ref[...]).T
        for hd in range(MOBA_HEADS):
            vaug_ref[0, hd, 0:HEAD_DIM, sl] = vt[hd * HEAD_DIM:(hd + 1) * HEAD_DIM].astype(bf16)
            vaug_ref[0, hd, HEAD_DIM:V_ROWS, sl] = vpad_ref[:, sl]


def _proj(x3d, gain, wa, wqb, wkb, wvb, kfeat, vpad, later_weights, *, tm):
    b, s, _ = x3d.shape
    nblk = s // MOBA_BLOCK
    steps = s // tm
    side_specs, side_shapes, side_out_specs, plan = [], [], [], []
    parked = lambda bi, i: (jnp.where(bi == 0, i, steps - 1), 0)
    for k, w in enumerate(later_weights):
        rows = w.shape[0] // steps
        assert rows * steps == w.shape[0] and rows % BF16_ROWS == 0
        spec = pl.BlockSpec((rows, w.shape[1]), parked)
        side_specs.append(spec)
        side_out_specs.append(spec)
        side_shapes.append(jax.ShapeDtypeStruct(w.shape, bf16))
        plan.append((k, 0, w.shape[1], None))
    grp_shapes = tuple(
        jax.ShapeDtypeStruct((b, dil, s // dil, DSA_GROUP_COLS), bf16) for _, dil in DSA_GROUPS)
    grp_specs = tuple(
        pl.BlockSpec((1, dil, tm // dil, DSA_GROUP_COLS), lambda bi, i: (bi, 0, i, 0))
        for _, dil in DSA_GROUPS)
    out_shape = grp_shapes + (
        jax.ShapeDtypeStruct((b, MOBA_HEADS // 2, s, HEAD_PAIR + K_FEAT), bf16),
        jax.ShapeDtypeStruct((b, MOBA_WIDTH, s), bf16),
        jax.ShapeDtypeStruct((b, MOBA_HEADS, V_ROWS, s), bf16),
        jax.ShapeDtypeStruct((b * nblk, 1, MOBA_WIDTH), f32),
    )
    per = tm // MOBA_BLOCK
    return pl.pallas_call(
        functools.partial(_proj_kernel, tm=tm, n_side=len(later_weights), plan=tuple(plan)),
        out_shape=out_shape + tuple(side_shapes),
        grid=(b, steps),
        in_specs=[
            pl.BlockSpec((1, tm, D_MODEL), lambda bi, i: (bi, i, 0)),
            _resident((1, D_MODEL)),
            _resident((D_MODEL, 3 * DSA_WIDTH)),
            _resident((D_MODEL, MOBA_WIDTH)),
            _resident((D_MODEL, MOBA_WIDTH)),
            _resident((D_MODEL, MOBA_WIDTH)),
            pl.BlockSpec((tm, K_FEAT), lambda bi, i: (i, 0)),
            _resident((V_ROWS - HEAD_DIM, tm)),
            *side_specs,
        ],
        out_specs=grp_specs + (
            pl.BlockSpec((1, MOBA_HEADS // 2, tm, HEAD_PAIR + K_FEAT), lambda bi, i: (bi, 0, i, 0)),
            pl.BlockSpec((1, MOBA_WIDTH, tm), lambda bi, i: (bi, 0, i)),
            pl.BlockSpec((1, MOBA_HEADS, V_ROWS, tm), lambda bi, i: (bi, 0, 0, i)),
            pl.BlockSpec((per, 1, MOBA_WIDTH), lambda bi, i: (bi * (s // tm) + i, 0, 0)),
        ) + tuple(side_out_specs),
        scratch_shapes=[pltpu.VMEM(
            (tm // GROUP_ROWS, len(DSA_GROUPS), DSA_GROUP_COLS // LANES, GROUP_ROWS, LANES), f32)],
        compiler_params=pltpu.CompilerParams(
            dimension_semantics=("arbitrary", "arbitrary"), vmem_limit_bytes=VMEM_LIMIT),
        name="proj",
    )(x3d, gain, wa, wqb, wkb, wvb, kfeat, vpad, *later_weights)


def _dsa_kernel(q_ref, k_ref, v_ref, bias_ref, o_ref, sa_ref, sb_ref, *, nstage, seg_tiles):
    tq, band = DSA_TQ, DSA_BAND
    first = lax.broadcasted_iota(jnp.int32, (tq, LANES), 1) < HEAD_DIM
    chains = [(u, h) for u in range(DSA_STAGE_TILES) for h in range(2)]

    def rows_of(t):
        return pl.ds(pl.multiple_of(t * tq, tq), tq)

    def seg_start(t):
        return (t % seg_tiles) == 0

    def band_of(t):
        start = jnp.where(seg_start(t), t * tq, t * tq - DSA_BLK)
        return pl.ds(pl.multiple_of(start, DSA_BLK), band)

    def scores(stage, s_ref):
        for u in range(DSA_STAGE_TILES):
            t = stage * DSA_STAGE_TILES + u
            q = q_ref[0, rows_of(t), :]
            kband = k_ref[0, band_of(t), :]
            for h in range(2):
                qm = jnp.where(first if h == 0 else jnp.logical_not(first), q, jnp.zeros_like(q))
                s_ref[2 * u + h] = _dot_nt(qm, kband)

    def finish(stage, s_ref):
        for u in range(DSA_STAGE_TILES):
            t = stage * DSA_STAGE_TILES + u
            variant = jnp.where(seg_start(t), 0, 1)
            vband = v_ref[0, band_of(t), :]
            outs, lses = [], []
            for h in range(2):
                s = s_ref[2 * u + h] + bias_ref[0, h, variant]
                m = jnp.max(s, axis=-1, keepdims=True)
                p = jnp.exp2(s - m)
                l = jnp.sum(p, axis=-1, keepdims=True)
                o = _dot(p.astype(bf16), vband)
                outs.append(o / l)
                lses.append(jnp.broadcast_to(m + jnp.log(l) * LOG2E, (tq, LANES)))
            o_ref[0, rows_of(t), 0:LANES] = jnp.where(first, outs[0], outs[1])
            o_ref[0, rows_of(t), LANES:2 * LANES] = jnp.where(first, lses[0], lses[1])

    zero = jnp.int32(0)
    scores(zero, sa_ref)

    def body(i2, carry):
        st = 2 * i2
        scores(st + 1, sb_ref)
        finish(st, sa_ref)
        scores(st + 2, sa_ref)
        finish(st + 1, sb_ref)
        return carry

    lax.fori_loop(0, nstage // 2 - 1, body, 0)
    st_end = zero + (nstage - 2)
    scores(st_end + 1, sb_ref)
    finish(st_end, sa_ref)
    finish(st_end + 1, sb_ref)


def _dsa_group(a, bias, *, group):
    b, dil, length, _ = a.shape
    s = dil * length
    pairs = DSA_HEADS_PER_GROUP // 2
    seg_tiles = length // DSA_TQ
    nstage = s // (DSA_TQ * DSA_STAGE_TILES)
    assert seg_tiles >= 2 and length % DSA_TQ == 0 and nstage % 2 == 0
    flat = a.reshape(b, s, DSA_GROUP_COLS)
    nchain = 2 * DSA_STAGE_TILES
    col = lambda part: (lambda bi, hp: (bi, 0, part * pairs + hp))
    out = pl.pallas_call(
        functools.partial(_dsa_kernel, nstage=nstage, seg_tiles=seg_tiles),
        out_shape=jax.ShapeDtypeStruct((b, s, 2 * pairs * LANES), f32),
        grid=(b, pairs),
        in_specs=[
            pl.BlockSpec((1, s, LANES), col(0)),
            pl.BlockSpec((1, s, LANES), col(1)),
            pl.BlockSpec((1, s, LANES), col(2)),
            pl.BlockSpec((1, 2, 2, DSA_TQ, DSA_BAND), lambda bi, hp: (hp, 0, 0, 0, 0)),
        ],
        out_specs=pl.BlockSpec((1, s, 2 * LANES), lambda bi, hp: (bi, 0, hp)),
        scratch_shapes=[pltpu.VMEM((nchain, DSA_TQ, DSA_BAND), f32),
                        pltpu.VMEM((nchain, DSA_TQ, DSA_BAND), f32)],
        compiler_params=pltpu.CompilerParams(
            dimension_semantics=("parallel", "parallel"), vmem_limit_bytes=VMEM_LIMIT),
        name=f"dsa_g{group}",
    )(flat, flat, flat, bias)
    return out.reshape(b, dil, length, 2 * pairs * LANES)


def _moba_kernel(slope_ref, qt_ref, kaug_ref, vaug_ref, ksum_ref, arow_ref, cmask_ref, o_ref,
                 rhs_ref, sa_ref, sb_ref, mta_ref, mtb_ref, acc_ref, m_ref, *, nblk):
    hp0 = pl.program_id(1) * MOBA_STEP_PAIRS
    n2 = pl.program_id(2)
    blk = MOBA_BLOCK
    tile = MOBA_TILE
    ntile = n2 + 1
    chains = [(pp, qb, h) for pp in range(MOBA_STEP_PAIRS) for qb in range(2) for h in range(2)]
    slot = {chain: c for c, chain in enumerate(chains)}
    rowi = lax.broadcasted_iota(jnp.int32, (HEAD_PAIR, tile), 0)
    bidx = lax.broadcasted_iota(jnp.int32, (nblk, tile), 0)
    own = 2 * n2 + (lax.broadcasted_iota(jnp.int32, (nblk, tile), 1) >= blk).astype(jnp.int32)
    past = bidx < own

    gates = {}
    for pp in range(MOBA_STEP_PAIRS):
        qpair = qt_ref[0, pp * HEAD_PAIR:(pp + 1) * HEAD_PAIR, :]
        kmean = ksum_ref[0, :, pp * HEAD_PAIR:(pp + 1) * HEAD_PAIR] * (1.0 / blk)
        kmean_hi = kmean.astype(bf16)
        kmean_lo = (kmean - kmean_hi.astype(f32)).astype(bf16)
        for h in range(2):
            mine = (rowi >= HEAD_DIM) if h else (rowi < HEAD_DIM)
            qpad = jnp.where(mine, qpair, jnp.zeros_like(qpair))
            gates[pp, h] = _dot(kmean_hi, qpad) + _dot(kmean_lo, qpad)
            for qb in range(2):
                c = slot[pp, qb, h]
                rhs_ref[c, 0:HEAD_PAIR, :] = qpad[:, qb * blk:(qb + 1) * blk]
                rhs_ref[c, HEAD_PAIR:HEAD_PAIR + nblk, :] = jnp.zeros((nblk, blk), bf16)
                rhs_ref[c, HEAD_PAIR + nblk:HEAD_PAIR + K_FEAT, :] = arow_ref[pp, h]
                acc_ref[c] = jnp.zeros((V_ROWS, blk), f32)
                m_ref[c] = jnp.full((1, blk), M_INIT, f32)

    def key_max(s):
        slabs = s.shape[0] // (tile // MAX_SLABS)
        smax = jnp.max(s.reshape(slabs, tile // MAX_SLABS, blk), axis=0) if slabs > 1 else s
        return jnp.max(smax, axis=0, keepdims=True)

    buf_a, buf_b = (sa_ref, mta_ref), (sb_ref, mtb_ref)

    def scores(i, buf):
        s_ref, mt_ref = buf
        k0 = pl.multiple_of(i * tile, tile)
        for pp in range(MOBA_STEP_PAIRS):
            kt = kaug_ref[0, pp, pl.ds(k0, tile), :]
            for qb in range(2):
                for h in range(2):
                    c = slot[pp, qb, h]
                    s = _dot(kt, rhs_ref[c])
                    s_ref[c] = s
                    mt_ref[c] = key_max(s)

    scores(0, buf_a)

    for (pp, h), gate in gates.items():
        g = jnp.where(past, gate, NEG)
        sel = bidx == own
        for _ in range(MOBA_TOPK):
            mx = jnp.max(g, axis=0, keepdims=True)
            idx = jnp.min(jnp.where(g == mx, bidx, nblk), axis=0, keepdims=True)
            pick = bidx == idx
            sel = sel | (pick & past)
            g = jnp.where(pick, -jnp.inf, g)
        maskbias = jnp.where(sel, 0.0, NEG).astype(bf16)
        for qb in range(2):
            c = slot[pp, qb, h]
            mb = maskbias[:, qb * blk:(qb + 1) * blk]
            rhs_ref[c, HEAD_PAIR:HEAD_PAIR + nblk, :] = mb
            tile_max = []
            for kb in range(tile // blk):
                s = sa_ref[c, kb * blk:(kb + 1) * blk, :] + mb[kb:kb + 1, :].astype(f32)
                sa_ref[c, kb * blk:(kb + 1) * blk, :] = s
                tile_max.append(key_max(s))
            mta_ref[c] = functools.reduce(jnp.maximum, tile_max)

    def update(i, buf, diagonal):
        s_ref, mt_ref = buf
        k0 = pl.multiple_of(i * tile, tile)
        for c, (pp, qb, h) in enumerate(chains):
            s = s_ref[c]
            if diagonal:
                s = s + cmask_ref[qb]
                tile_max = key_max(s)
            else:
                tile_max = mt_ref[c]
            rel = tile * (i - n2) - blk * qb
            shift = slope_ref[2 * (hp0 + pp) + h] * rel.astype(f32)
            m_old = m_ref[c]
            m_new = jnp.maximum(m_old, tile_max + shift)
            p = jnp.exp2(s - (m_new - shift)).astype(bf16)
            pv = _dot(vaug_ref[0, 2 * pp + h, :, pl.ds(k0, tile)], p)
            acc_ref[c] = jnp.exp2(m_old - m_new) * acc_ref[c] + pv
            m_ref[c] = m_new

    npair = (ntile - 1) // 2

    def body(i2, carry):
        t = 2 * i2
        scores(t + 1, buf_b)
        update(t, buf_a, False)
        scores(t + 2, buf_a)
        update(t + 1, buf_b, False)
        return carry

    lax.fori_loop(0, npair, body, 0)
    t_end = 2 * npair

    @pl.when(ntile - t_end == 1)
    def _():
        update(t_end, buf_a, True)

    @pl.when(ntile - t_end == 2)
    def _():
        scores(t_end + 1, buf_b)
        update(t_end, buf_a, False)
        update(t_end + 1, buf_b, True)

    for c, (pp, qb, h) in enumerate(chains):
        acc = acc_ref[c]
        row0 = pp * HEAD_PAIR + h * HEAD_DIM
        o_ref[0, row0:row0 + HEAD_DIM, qb * blk:(qb + 1) * blk] = (
            acc[0:HEAD_DIM] / acc[HEAD_DIM:HEAD_DIM + 1]).astype(bf16)


def _moba(slopes, qt, kaug, vaug, ksum, arows, cmask):
    b, _, s = qt.shape
    nblk = s // MOBA_BLOCK
    pairs = MOBA_HEADS // 2
    assert nblk + arows.shape[2] == K_FEAT and nblk % 2 == 0 and pairs % MOBA_STEP_PAIRS == 0
    npp = MOBA_STEP_PAIRS
    nchain = 4 * npp
    return pl.pallas_call(
        functools.partial(_moba_kernel, nblk=nblk),
        out_shape=jax.ShapeDtypeStruct((b, MOBA_WIDTH, s), bf16),
        grid=(b, pairs // npp, nblk // 2),
        in_specs=[
            pl.BlockSpec(memory_space=pltpu.SMEM),
            pl.BlockSpec((1, npp * HEAD_PAIR, MOBA_TILE), lambda bi, hp, n2: (bi, hp, n2)),
            pl.BlockSpec((1, npp, s, HEAD_PAIR + K_FEAT), lambda bi, hp, n2: (bi, hp, 0, 0)),
            pl.BlockSpec((1, 2 * npp, V_ROWS, s), lambda bi, hp, n2: (bi, hp, 0, 0)),
            pl.BlockSpec((1, nblk, npp * HEAD_PAIR), lambda bi, hp, n2: (bi, 0, hp)),
            pl.BlockSpec((npp, 2, K_FEAT - nblk, MOBA_BLOCK), lambda bi, hp, n2: (hp, 0, 0, 0)),
            _resident((2, MOBA_TILE, MOBA_BLOCK)),
        ],
        out_specs=pl.BlockSpec((1, npp * HEAD_PAIR, MOBA_TILE), lambda bi, hp, n2: (bi, hp, n2)),
        scratch_shapes=[
            pltpu.VMEM((nchain, HEAD_PAIR + K_FEAT, MOBA_BLOCK), bf16),
            pltpu.VMEM((nchain, MOBA_TILE, MOBA_BLOCK), f32),
            pltpu.VMEM((nchain, MOBA_TILE, MOBA_BLOCK), f32),
            pltpu.VMEM((nchain, 1, MOBA_BLOCK), f32),
            pltpu.VMEM((nchain, 1, MOBA_BLOCK), f32),
            pltpu.VMEM((nchain, V_ROWS, MOBA_BLOCK), f32),
            pltpu.VMEM((nchain, 1, MOBA_BLOCK), f32),
        ],
        compiler_params=pltpu.CompilerParams(
            dimension_semantics=("parallel", "parallel", "arbitrary"), vmem_limit_bytes=VMEM_LIMIT),
        name="moba",
    )(slopes, qt, kaug, vaug, ksum.reshape(b, nblk, MOBA_WIDTH), arows, cmask)


def _merge_kernel(x_ref, g_ref, d0_ref, d1_ref, d2_ref, ybt_ref, wgate_ref, wua_ref, wub_ref, wo_ref, o_ref,
                  nat_ref, *, tm):
    rows = GROUP_ROWS

    def token_order(g, c, part):
        d_ref, dil = (d0_ref, d1_ref, d2_ref)[g], DSA_GROUPS[g][1]
        per = rows // dil
        lanes = slice(c * LANES, (c + 1) * LANES)
        if dil == 1:
            return d_ref[0, 0, part * rows:(part + 1) * rows, lanes]
        for r in range(dil):
            nat_ref[g, c, pl.ds(part * rows + r, per, stride=dil), :] = (
                d_ref[0, r, part * per:(part + 1) * per, lanes])
        return nat_ref[g, c, part * rows:(part + 1) * rows, :]

    for part in range(tm // rows):
        sl = slice(part * rows, (part + 1) * rows)
        pairs = []
        for hp in range(2):
            outs = [token_order(g, 2 * hp, part) for g in range(len(DSA_GROUPS))]
            lses = [token_order(g, 2 * hp + 1, part) for g in range(len(DSA_GROUPS))]
            mx = jnp.maximum(jnp.maximum(lses[0], lses[1]), lses[2])
            es = [jnp.exp2(l - mx) for l in lses]
            den = es[0] + es[1] + es[2]
            pairs.append((es[0] * outs[0] + es[1] * outs[1] + es[2] * outs[2]) / den)
        ya = jnp.concatenate(pairs, axis=1).astype(bf16)
        x = x_ref[0, sl, :]
        h = _rmsnorm(x, g_ref[...]).astype(bf16)
        gates = jax.nn.sigmoid(_dot(h, wgate_ref[...]))
        ta = _dot(ya, wua_ref[...])
        tb = _dot_tn(ybt_ref[0, :, sl], wub_ref[...])
        merged = gates[:, 0:D_MODEL] * ta + gates[:, D_MODEL:] * tb
        o_ref[0, sl, :] = x + _dot(merged.astype(bf16), wo_ref[...])


def _merge(x3d, gain, dsa_outs, ybt, wgate, wua, wub, wo, *, tm):
    b, s, _ = x3d.shape
    tok = lambda w: pl.BlockSpec((1, tm, w), lambda bi, i: (bi, i, 0))
    grp = lambda dil: pl.BlockSpec((1, dil, tm // dil, 4 * LANES), lambda bi, i: (bi, 0, i, 0))
    return pl.pallas_call(
        functools.partial(_merge_kernel, tm=tm),
        out_shape=jax.ShapeDtypeStruct((b, s, D_MODEL), f32),
        grid=(b, s // tm),
        in_specs=[
            tok(D_MODEL), _resident((1, D_MODEL)), *[grp(dil) for _, dil in DSA_GROUPS],
            pl.BlockSpec((1, MOBA_WIDTH, tm), lambda bi, i: (bi, 0, i)),
            _resident((D_MODEL, 2 * D_MODEL)),
            _resident((DSA_OUT, D_MODEL)),
            _resident((MOBA_WIDTH, D_MODEL)),
            _resident((D_MODEL, D_MODEL)),
        ],
        out_specs=tok(D_MODEL),
        scratch_shapes=[pltpu.VMEM((len(DSA_GROUPS), 4, tm, LANES), f32)],
        compiler_params=pltpu.CompilerParams(
            dimension_semantics=("parallel", "parallel"), vmem_limit_bytes=VMEM_LIMIT),
        name="merge",
    )(x3d, gain, *dsa_outs, ybt, wgate, wua, wub, wo)


def _alibi_slopes(n):
    return 2.0 ** (-8.0 * np.arange(1, n + 1, dtype=np.float64) / n)


def _split3(x):
    hi = x.astype(bf16)
    r1 = x - hi.astype(f32)
    mid = r1.astype(bf16)
    lo = (r1 - mid.astype(f32)).astype(bf16)
    return hi, mid, lo


def _moba_constants(s):
    nblk = s // MOBA_BLOCK
    n_arow = K_FEAT - nblk
    pos = np.arange(s)
    kfeat = np.zeros((s, K_FEAT), np.float32)
    kfeat[pos, pos // MOBA_BLOCK] = 1.0
    kfeat[:, nblk:nblk + 3] = 1.0
    kfeat[:, nblk + 3:nblk + 6] = (pos % MOBA_BLOCK)[:, None]
    kfeat[:, nblk + 6:nblk + 9] = ((pos // MOBA_BLOCK) % 2)[:, None]
    slopes = jnp.asarray(_alibi_slopes(MOBA_HEADS) * LOG2E, f32)
    off = jnp.arange(MOBA_BLOCK, dtype=f32)
    wide = lambda v: jnp.broadcast_to(v[:, None], (MOBA_HEADS, MOBA_BLOCK))
    qterm = _split3(-slopes[:, None] * off[None, :])
    sterm = _split3(wide(slopes))
    bterm = _split3(wide(slopes * MOBA_BLOCK))
    rows = jnp.stack(list(qterm) + list(sterm) + list(bterm), axis=1)
    rows = jnp.concatenate(
        [rows, jnp.zeros((MOBA_HEADS, n_arow - 9, MOBA_BLOCK), bf16)], axis=1)
    arows = rows.reshape(MOBA_HEADS // 2, 2, n_arow, MOBA_BLOCK)
    r = jnp.arange(MOBA_TILE)[None, :, None]
    c = jnp.arange(MOBA_BLOCK)[None, None, :] + MOBA_BLOCK * jnp.arange(2)[:, None, None]
    cmask = jnp.where(r <= c, 0.0, NEG).astype(f32)
    return jnp.asarray(kfeat, bf16), arows, slopes, cmask


def _dsa_bias(group, dilation):
    slopes = _alibi_slopes(DSA_HEADS)[group * DSA_HEADS_PER_GROUP:(group + 1) * DSA_HEADS_PER_GROUP]
    slopes = jnp.asarray(slopes * LOG2E * dilation, f32).reshape(2, 2, 1, 1, 1)
    qi = jnp.arange(DSA_TQ)[:, None]
    kj = jnp.arange(DSA_BAND)[None, :]
    delta = jnp.stack([qi - kj, qi + DSA_BLK - kj])[None, None]
    valid = (delta >= 0) & (delta <= DSA_BLK)
    return jnp.where(valid, -slopes * delta.astype(f32), NEG)


def _layer(x, norm_ffn1, ffn1_gate, ffn1_up, ffn1_down, norm_mix, w_in, w_up_a, w_up_b, w_out,
           norm_ffn2, ffn2_gate, ffn2_up, ffn2_down, gain_final, *, final_norm):
    b, s, _ = x.shape
    tm = 512
    tm_merge = 1024
    tm_ffn = 1024
    row = lambda g: g.reshape(1, D_MODEL).astype(f32)
    qscale = HEAD_DIM ** -0.5 * LOG2E
    qa_scale = jnp.ones((1, 3 * DSA_WIDTH), f32).at[:, :DSA_WIDTH].set(qscale)
    x1, wa, wqb, wkb, wvb, wgate = _ffn(
        x.reshape(b * s, D_MODEL), row(norm_ffn1), ffn1_gate.astype(bf16), ffn1_up.astype(bf16),
        ffn1_down.astype(bf16), row(gain_final), final_norm=False, tm=tm_ffn, w_in=w_in,
        qa_scale=qa_scale, qb_scale=float(qscale))
    x1 = x1.reshape(b, s, D_MODEL)

    kfeat, arows, slopes_b, cmask = _moba_constants(s)
    vpad = jnp.zeros((V_ROWS - HEAD_DIM, tm), bf16).at[0].set(1.0)
    a0, a1, a2, kaug, qt, vaug, ksum, w2_gate, w2_up, w2_down, wua, wub, wo = _proj(
        x1, row(norm_mix), wa, wqb, wkb, wvb, kfeat, vpad,
        (ffn2_gate, ffn2_up, ffn2_down, w_up_a, w_up_b, w_out), tm=tm)

    dsa_outs = [
        _dsa_group(a, _dsa_bias(gi, dil), group=gi)
        for gi, (a, (_, dil)) in enumerate(zip((a0, a1, a2), DSA_GROUPS))
    ]
    ybt = _moba(slopes_b, qt, kaug, vaug, ksum, arows, cmask)
    x2 = _merge(x1, row(norm_mix), dsa_outs, ybt, wgate, wua, wub, wo, tm=tm_merge)
    x3 = _ffn(x2.reshape(b * s, D_MODEL), row(norm_ffn2), w2_gate, w2_up, w2_down, row(gain_final),
              final_norm=final_norm, tm=tm_ffn)
    return x3.reshape(b, s, D_MODEL)


def kernel(x, norm_ffn1, ffn1_gate, ffn1_up, ffn1_down, norm_mix, w_in, w_up_a, w_up_b, w_out,
           norm_ffn2, ffn2_gate, ffn2_up, ffn2_down, norm_final):
    depth = norm_ffn1.shape[0]
    for layer in range(depth):
        last = layer == depth - 1
        x = _layer(x, norm_ffn1[layer], ffn1_gate[layer], ffn1_up[layer], ffn1_down[layer],
                   norm_mix[layer], w_in[layer], w_up_a[layer], w_up_b[layer], w_out[layer],
                   norm_ffn2[layer], ffn2_gate[layer], ffn2_up[layer], ffn2_down[layer],
                   norm_final, final_norm=last)
    return x
```

```python
import functools

import numpy as np
import jax
import jax.numpy as jnp
from jax import lax
from jax.experimental import pallas as pl
from jax.experimental.pallas import tpu as pltpu

D_MODEL = 1024
HEAD_DIM = 64
DSA_GROUPS = ((128, 1), (512, 4), (2048, 16))
DSA_HEADS_PER_GROUP = 4
DSA_HEADS = DSA_HEADS_PER_GROUP * len(DSA_GROUPS)
MOBA_HEADS = 8
MOBA_BLOCK = 256
MOBA_TOPK = 3
D_FF = ((8 * D_MODEL // 3 + 127) // 128) * 128
DSA_WIDTH = DSA_HEADS * HEAD_DIM
MOBA_WIDTH = MOBA_HEADS * HEAD_DIM
DSA_OUT = DSA_HEADS_PER_GROUP * HEAD_DIM
EPS = 1e-6
NEG = -1e30

LANES = 128
HEAD_PAIR = 2 * HEAD_DIM
DSA_BLK = 128
DSA_TQ = DSA_BLK
DSA_BAND = DSA_TQ + DSA_BLK
DSA_STAGE_TILES = 2
DSA_GROUP_COLS = 3 * DSA_OUT
MOBA_TILE = 2 * MOBA_BLOCK
MAX_SLABS = 8
MOBA_STEP_PAIRS = 2
LOG2E = 1.4426950408889634
M_INIT = -3.0e38
V_ROWS = 80
K_FEAT = 128
VMEM_LIMIT = 56 * 1024 * 1024
GROUP_ROWS = 256
BF16_ROWS = 16

f32 = jnp.float32
bf16 = jnp.bfloat16


def _rmsnorm(x, g):
    return x * lax.rsqrt(jnp.mean(x * x, axis=-1, keepdims=True) + EPS) * g


def _dot(a, b):
    return jnp.dot(a, b, preferred_element_type=f32)


def _dot_nt(a, b):
    return lax.dot_general(a, b, (((1,), (1,)), ((), ())), preferred_element_type=f32)


def _dot_tn(a, b):
    return lax.dot_general(a, b, (((0,), (0,)), ((), ())), preferred_element_type=f32)


def _fold_lanes(x, op):
    tiles = [x[:, c * LANES:(c + 1) * LANES] for c in range(x.shape[-1] // LANES)]
    return functools.reduce(op, tiles)


def _resident(shape):
    nd = len(shape)
    return pl.BlockSpec(shape, lambda *_: (0,) * nd, pipeline_mode=pl.Buffered(1))


def _cast_rows(side_in, side_out, plan):
    for o_ref, (src, lo, hi, scale) in zip(side_out, plan):
        w = side_in[src][:, lo:hi]
        if isinstance(scale, int):
            w = w * side_in[scale][...]
        elif scale is not None:
            w = w * scale
        o_ref[...] = w.astype(bf16)


def _ffn_kernel(x_ref, g_ref, wg_ref, wu_ref, wd_ref, gf_ref, *rest, final_norm, n_side, plan):
    side_in, (o_ref, *side_out) = rest[:n_side], rest[n_side:]
    _cast_rows(side_in, side_out, plan)
    rows = GROUP_ROWS
    for part in range(x_ref.shape[0] // rows):
        sl = slice(part * rows, (part + 1) * rows)
        x = x_ref[sl, :]
        h = _rmsnorm(x, g_ref[...]).astype(bf16)
        gate = _dot(h, wg_ref[...])
        up = _dot(h, wu_ref[...])
        act = (gate * jax.nn.sigmoid(gate) * up).astype(bf16)
        y = x + 0.5 * _dot(act, wd_ref[...])
        if final_norm:
            y = _rmsnorm(y, gf_ref[...])
        o_ref[sl, :] = y


def _ffn(x2d, gain, w_gate, w_up, w_down, gain_final, *, final_norm, tm, w_in=None, qa_scale=None,
         qb_scale=None):
    n = x2d.shape[0]
    steps = n // tm
    side_args, side_specs, side_shapes, side_out_specs, plan = [], [], [], [], ()
    if w_in is not None:
        rows = w_in.shape[0] // steps
        assert rows * steps == w_in.shape[0] and rows % BF16_ROWS == 0
        block = lambda width: pl.BlockSpec((rows, width), lambda i: (i, 0))
        side_args = [w_in, qa_scale]
        side_specs = [block(w_in.shape[1]), _resident(qa_scale.shape)]
        qb0 = 3 * DSA_WIDTH
        plan = ((0, 0, qb0, 1),
                (0, qb0, qb0 + MOBA_WIDTH, qb_scale),
                (0, qb0 + MOBA_WIDTH, qb0 + 2 * MOBA_WIDTH, None),
                (0, qb0 + 2 * MOBA_WIDTH, qb0 + 3 * MOBA_WIDTH, None),
                (0, qb0 + 3 * MOBA_WIDTH, w_in.shape[1], None))
        for _, lo, hi, _ in plan:
            side_shapes.append(jax.ShapeDtypeStruct((w_in.shape[0], hi - lo), bf16))
            side_out_specs.append(block(hi - lo))
    out = pl.pallas_call(
        functools.partial(_ffn_kernel, final_norm=final_norm, n_side=len(side_args), plan=plan),
        out_shape=(jax.ShapeDtypeStruct((n, D_MODEL), f32), *side_shapes),
        grid=(steps,),
        in_specs=[
            pl.BlockSpec((tm, D_MODEL), lambda i: (i, 0)),
            _resident((1, D_MODEL)),
            _resident((D_MODEL, D_FF)),
            _resident((D_MODEL, D_FF)),
            _resident((D_FF, D_MODEL)),
            _resident((1, D_MODEL)),
            *side_specs,
        ],
        out_specs=(pl.BlockSpec((tm, D_MODEL), lambda i: (i, 0)), *side_out_specs),
        compiler_params=pltpu.CompilerParams(
            dimension_semantics=("parallel",), vmem_limit_bytes=VMEM_LIMIT),
        name="ffn_final" if final_norm else "ffn",
    )(x2d, gain, w_gate, w_up, w_down, gain_final, *side_args)
    return out if side_args else out[0]


def _proj_kernel(x_ref, g_ref, wa_ref, wqb_ref, wkb_ref, wvb_ref, kfeat_ref, vpad_ref, *rest,
                 tm, n_side, plan):
    side_in, rest = rest[:n_side], rest[n_side:]
    a0_ref, a1_ref, a2_ref, kaug_ref, qt_ref, vaug_ref, ksum_ref = rest[:7]
    side_out, scr_ref = rest[7:-1], rest[-1]

    @pl.when(pl.program_id(0) == 0)
    def _():
        _cast_rows(side_in, side_out, plan)

    rows = GROUP_ROWS
    tiles_per_group = DSA_GROUP_COLS // LANES
    for part in range(tm // rows):
        sl = slice(part * rows, (part + 1) * rows)
        h = _rmsnorm(x_ref[0, sl, :], g_ref[...]).astype(bf16)
        qkv = _dot(h, wa_ref[...])
        for g, (a_ref, (_, dil)) in enumerate(zip((a0_ref, a1_ref, a2_ref), DSA_GROUPS)):
            per = rows // dil
            for c in range(tiles_per_group):
                lo = (c // 2) * DSA_WIDTH + g * DSA_OUT + (c % 2) * LANES
                lanes = slice(c * LANES, (c + 1) * LANES)
                if dil == 1:
                    a_ref[0, 0, sl, lanes] = qkv[:, lo:lo + LANES].astype(bf16)
                    continue
                scr_ref[part, g, c] = qkv[:, lo:lo + LANES]
                for r in range(dil):
                    a_ref[0, r, part * per:(part + 1) * per, lanes] = (
                        scr_ref[part, g, c, pl.ds(r, per, stride=dil), :].astype(bf16))
        kb = _dot(h, wkb_ref[...])
        nb = rows // MOBA_BLOCK
        ksum_ref[part * nb:(part + 1) * nb] = (
            jnp.sum(kb.reshape(nb, MOBA_BLOCK, MOBA_WIDTH), axis=1)[:, None, :])
        kb = kb.astype(bf16)
        for p in range(MOBA_HEADS // 2):
            kaug_ref[0, p, sl, 0:HEAD_PAIR] = kb[:, p * HEAD_PAIR:(p + 1) * HEAD_PAIR]
            kaug_ref[0, p, sl, HEAD_PAIR:HEAD_PAIR + K_FEAT] = kfeat_ref[sl, :]
        qt_ref[0, :, sl] = _dot(h, wqb_ref[...]).T.astype(bf16)
        vt = _dot(h, wvb_ref[...]).T
        for hd in range(MOBA_HEADS):
            vaug_ref[0, hd, 0:HEAD_DIM, sl] = vt[hd * HEAD_DIM:(hd + 1) * HEAD_DIM].astype(bf16)
            vaug_ref[0, hd, HEAD_DIM:V_ROWS, sl] = vpad_ref[:, sl]


def _proj(x3d, gain, wa, wqb, wkb, wvb, kfeat, vpad, later_weights, *, tm):
    b, s, _ = x3d.shape
    nblk = s // MOBA_BLOCK
    steps = s // tm
    side_specs, side_shapes, side_out_specs, plan = [], [], [], []
    parked = lambda bi, i: (jnp.where(bi == 0, i, steps - 1), 0)
    for k, w in enumerate(later_weights):
        rows = w.shape[0] // steps
        assert rows * steps == w.shape[0] and rows % BF16_ROWS == 0
        spec = pl.BlockSpec((rows, w.shape[1]), parked)
        side_specs.append(spec)
        side_out_specs.append(spec)
        side_shapes.append(jax.ShapeDtypeStruct(w.shape, bf16))
        plan.append((k, 0, w.shape[1], None))
    grp_shapes = tuple(
        jax.ShapeDtypeStruct((b, dil, s // dil, DSA_GROUP_COLS), bf16) for _, dil in DSA_GROUPS)
    grp_specs = tuple(
        pl.BlockSpec((1, dil, tm // dil, DSA_GROUP_COLS), lambda bi, i: (bi, 0, i, 0))
        for _, dil in DSA_GROUPS)
    out_shape = grp_shapes + (
        jax.ShapeDtypeStruct((b, MOBA_HEADS // 2, s, HEAD_PAIR + K_FEAT), bf16),
        jax.ShapeDtypeStruct((b, MOBA_WIDTH, s), bf16),
        jax.ShapeDtypeStruct((b, MOBA_HEADS, V_ROWS, s), bf16),
        jax.ShapeDtypeStruct((b * nblk, 1, MOBA_WIDTH), f32),
    )
    per = tm // MOBA_BLOCK
    return pl.pallas_call(
        functools.partial(_proj_kernel, tm=tm, n_side=len(later_weights), plan=tuple(plan)),
        out_shape=out_shape + tuple(side_shapes),
        grid=(b, steps),
        in_specs=[
            pl.BlockSpec((1, tm, D_MODEL), lambda bi, i: (bi, i, 0)),
            _resident((1, D_MODEL)),
            _resident((D_MODEL, 3 * DSA_WIDTH)),
            _resident((D_MODEL, MOBA_WIDTH)),
            _resident((D_MODEL, MOBA_WIDTH)),
            _resident((D_MODEL, MOBA_WIDTH)),
            pl.BlockSpec((tm, K_FEAT), lambda bi, i: (i, 0)),
            _resident((V_ROWS - HEAD_DIM, tm)),
            *side_specs,
        ],
        out_specs=grp_specs + (
            pl.BlockSpec((1, MOBA_HEADS // 2, tm, HEAD_PAIR + K_FEAT), lambda bi, i: (bi, 0, i, 0)),
            pl.BlockSpec((1, MOBA_WIDTH, tm), lambda bi, i: (bi, 0, i)),
            pl.BlockSpec((1, MOBA_HEADS, V_ROWS, tm), lambda bi, i: (bi, 0, 0, i)),
            pl.BlockSpec((per, 1, MOBA_WIDTH), lambda bi, i: (bi * (s // tm) + i, 0, 0)),
        ) + tuple(side_out_specs),
        scratch_shapes=[pltpu.VMEM(
            (tm // GROUP_ROWS, len(DSA_GROUPS), DSA_GROUP_COLS // LANES, GROUP_ROWS, LANES), f32)],
        compiler_params=pltpu.CompilerParams(
            dimension_semantics=("arbitrary", "arbitrary"), vmem_limit_bytes=VMEM_LIMIT),
        name="proj",
    )(x3d, gain, wa, wqb, wkb, wvb, kfeat, vpad, *later_weights)


def _dsa_kernel(q_ref, k_ref, v_ref, bias_ref, o_ref, sa_ref, sb_ref, *, nstage, seg_tiles):
    tq, band = DSA_TQ, DSA_BAND
    first = lax.broadcasted_iota(jnp.int32, (tq, LANES), 1) < HEAD_DIM
    chains = [(u, h) for u in range(DSA_STAGE_TILES) for h in range(2)]

    def rows_of(t):
        return pl.ds(pl.multiple_of(t * tq, tq), tq)

    def seg_start(t):
        return (t % seg_tiles) == 0

    def band_of(t):
        start = jnp.where(seg_start(t), t * tq, t * tq - DSA_BLK)
        return pl.ds(pl.multiple_of(start, DSA_BLK), band)

    def scores(stage, s_ref):
        for u in range(DSA_STAGE_TILES):
            t = stage * DSA_STAGE_TILES + u
            q = q_ref[0, rows_of(t), :]
            kband = k_ref[0, band_of(t), :]
            for h in range(2):
                qm = jnp.where(first if h == 0 else jnp.logical_not(first), q, jnp.zeros_like(q))
                s_ref[2 * u + h] = _dot_nt(qm, kband)

    def finish(stage, s_ref):
        for u in range(DSA_STAGE_TILES):
            t = stage * DSA_STAGE_TILES + u
            variant = jnp.where(seg_start(t), 0, 1)
            vband = v_ref[0, band_of(t), :]
            outs, lses = [], []
            for h in range(2):
                s = s_ref[2 * u + h] + bias_ref[0, h, variant]
                m = jnp.max(_fold_lanes(s, jnp.maximum), axis=-1, keepdims=True)
                p = jnp.exp2(s - m)
                l = jnp.sum(_fold_lanes(p, jnp.add), axis=-1, keepdims=True)
                o = _dot(p.astype(bf16), vband)
                outs.append(o / l)
                lses.append(jnp.broadcast_to(m + jnp.log(l) * LOG2E, (tq, LANES)))
            o_ref[0, rows_of(t), 0:LANES] = jnp.where(first, outs[0], outs[1])
            o_ref[0, rows_of(t), LANES:2 * LANES] = jnp.where(first, lses[0], lses[1])

    zero = jnp.int32(0)
    scores(zero, sa_ref)

    def body(i2, carry):
        st = 2 * i2
        scores(st + 1, sb_ref)
        finish(st, sa_ref)
        scores(st + 2, sa_ref)
        finish(st + 1, sb_ref)
        return carry

    lax.fori_loop(0, nstage // 2 - 1, body, 0)
    st_end = zero + (nstage - 2)
    scores(st_end + 1, sb_ref)
    finish(st_end, sa_ref)
    finish(st_end + 1, sb_ref)


def _dsa_group(a, bias, *, group):
    b, dil, length, _ = a.shape
    s = dil * length
    pairs = DSA_HEADS_PER_GROUP // 2
    seg_tiles = length // DSA_TQ
    nstage = s // (DSA_TQ * DSA_STAGE_TILES)
    assert seg_tiles >= 2 and length % DSA_TQ == 0 and nstage % 2 == 0
    flat = a.reshape(b, s, DSA_GROUP_COLS)
    nchain = 2 * DSA_STAGE_TILES
    col = lambda part: (lambda bi, hp: (bi, 0, part * pairs + hp))
    out = pl.pallas_call(
        functools.partial(_dsa_kernel, nstage=nstage, seg_tiles=seg_tiles),
        out_shape=jax.ShapeDtypeStruct((b, s, 2 * pairs * LANES), f32),
        grid=(b, pairs),
        in_specs=[
            pl.BlockSpec((1, s, LANES), col(0)),
            pl.BlockSpec((1, s, LANES), col(1)),
            pl.BlockSpec((1, s, LANES), col(2)),
            pl.BlockSpec((1, 2, 2, DSA_TQ, DSA_BAND), lambda bi, hp: (hp, 0, 0, 0, 0)),
        ],
        out_specs=pl.BlockSpec((1, s, 2 * LANES), lambda bi, hp: (bi, 0, hp)),
        scratch_shapes=[pltpu.VMEM((nchain, DSA_TQ, DSA_BAND), f32),
                        pltpu.VMEM((nchain, DSA_TQ, DSA_BAND), f32)],
        compiler_params=pltpu.CompilerParams(
            dimension_semantics=("parallel", "parallel"), vmem_limit_bytes=VMEM_LIMIT),
        name=f"dsa_g{group}",
    )(flat, flat, flat, bias)
    return out.reshape(b, dil, length, 2 * pairs * LANES)


def _moba_kernel(slope_ref, qt_ref, kaug_ref, vaug_ref, ksum_ref, arow_ref, cmask_ref, o_ref,
                 rhs_ref, sa_ref, sb_ref, mta_ref, mtb_ref, acc_ref, m_ref, *, nblk):
    hp0 = pl.program_id(1) * MOBA_STEP_PAIRS
    n2 = pl.program_id(2)
    blk = MOBA_BLOCK
    tile = MOBA_TILE
    ntile = n2 + 1
    chains = [(pp, qb, h) for pp in range(MOBA_STEP_PAIRS) for qb in range(2) for h in range(2)]
    slot = {chain: c for c, chain in enumerate(chains)}
    rowi = lax.broadcasted_iota(jnp.int32, (HEAD_PAIR, tile), 0)
    bidx = lax.broadcasted_iota(jnp.int32, (nblk, tile), 0)
    own = 2 * n2 + (lax.broadcasted_iota(jnp.int32, (nblk, tile), 1) >= blk).astype(jnp.int32)
    past = bidx < own

    gates = {}
    for pp in range(MOBA_STEP_PAIRS):
        qpair = qt_ref[0, pp * HEAD_PAIR:(pp + 1) * HEAD_PAIR, :]
        kmean = ksum_ref[0, :, pp * HEAD_PAIR:(pp + 1) * HEAD_PAIR] * (1.0 / blk)
        kmean_hi = kmean.astype(bf16)
        kmean_lo = (kmean - kmean_hi.astype(f32)).astype(bf16)
        for h in range(2):
            mine = (rowi >= HEAD_DIM) if h else (rowi < HEAD_DIM)
            qpad = jnp.where(mine, qpair, jnp.zeros_like(qpair))
            gates[pp, h] = _dot(kmean_hi, qpad) + _dot(kmean_lo, qpad)
            for qb in range(2):
                c = slot[pp, qb, h]
                rhs_ref[c, 0:HEAD_PAIR, :] = qpad[:, qb * blk:(qb + 1) * blk]
                rhs_ref[c, HEAD_PAIR:HEAD_PAIR + nblk, :] = jnp.zeros((nblk, blk), bf16)
                rhs_ref[c, HEAD_PAIR + nblk:HEAD_PAIR + K_FEAT, :] = arow_ref[pp, h]
                acc_ref[c] = jnp.zeros((V_ROWS, blk), f32)
                m_ref[c] = jnp.full((1, blk), M_INIT, f32)

    def key_max(s):
        slabs = s.shape[0] // (tile // MAX_SLABS)
        smax = jnp.max(s.reshape(slabs, tile // MAX_SLABS, blk), axis=0) if slabs > 1 else s
        return jnp.max(smax, axis=0, keepdims=True)

    buf_a, buf_b = (sa_ref, mta_ref), (sb_ref, mtb_ref)

    def scores(i, buf):
        s_ref, mt_ref = buf
        k0 = pl.multiple_of(i * tile, tile)
        for pp in range(MOBA_STEP_PAIRS):
            kt = kaug_ref[0, pp, pl.ds(k0, tile), :]
            for qb in range(2):
                for h in range(2):
                    c = slot[pp, qb, h]
                    s = _dot(kt, rhs_ref[c])
                    s_ref[c] = s
                    mt_ref[c] = key_max(s)

    scores(0, buf_a)

    for (pp, h), gate in gates.items():
        g = jnp.where(past, gate, NEG)
        sel = bidx == own
        for _ in range(MOBA_TOPK):
            mx = jnp.max(g, axis=0, keepdims=True)
            idx = jnp.min(jnp.where(g == mx, bidx, nblk), axis=0, keepdims=True)
            pick = bidx == idx
            sel = sel | (pick & past)
            g = jnp.where(pick, -jnp.inf, g)
        maskbias = jnp.where(sel, 0.0, NEG).astype(bf16)
        for qb in range(2):
            c = slot[pp, qb, h]
            mb = maskbias[:, qb * blk:(qb + 1) * blk]
            rhs_ref[c, HEAD_PAIR:HEAD_PAIR + nblk, :] = mb
            tile_max = []
            for kb in range(tile // blk):
                s = sa_ref[c, kb * blk:(kb + 1) * blk, :] + mb[kb:kb + 1, :].astype(f32)
                sa_ref[c, kb * blk:(kb + 1) * blk, :] = s
                tile_max.append(key_max(s))
            mta_ref[c] = functools.reduce(jnp.maximum, tile_max)

    def update(i, buf, diagonal):
        s_ref, mt_ref = buf
        k0 = pl.multiple_of(i * tile, tile)
        for c, (pp, qb, h) in enumerate(chains):
            if diagonal:
                keys = blk * (qb + 1)
                s = s_ref[c, 0:keys, :] + cmask_ref[qb, 0:keys, :]
                tile_max = key_max(s)
            else:
                keys = tile
                s = s_ref[c]
                tile_max = mt_ref[c]
            rel = tile * (i - n2) - blk * qb
            shift = slope_ref[2 * (hp0 + pp) + h] * rel.astype(f32)
            m_old = m_ref[c]
            m_new = jnp.maximum(m_old, tile_max + shift)
            p = jnp.exp2(s - (m_new - shift)).astype(bf16)
            pv = _dot(vaug_ref[0, 2 * pp + h, :, pl.ds(k0, keys)], p)
            acc_ref[c] = jnp.exp2(m_old - m_new) * acc_ref[c] + pv
            m_ref[c] = m_new

    npair = (ntile - 1) // 2

    def body(i2, carry):
        t = 2 * i2
        scores(t + 1, buf_b)
        update(t, buf_a, False)
        scores(t + 2, buf_a)
        update(t + 1, buf_b, False)
        return carry

    lax.fori_loop(0, npair, body, 0)
    t_end = 2 * npair

    @pl.when(ntile - t_end == 1)
    def _():
        update(t_end, buf_a, True)

    @pl.when(ntile - t_end == 2)
    def _():
        scores(t_end + 1, buf_b)
        update(t_end, buf_a, False)
        update(t_end + 1, buf_b, True)

    for c, (pp, qb, h) in enumerate(chains):
        acc = acc_ref[c]
        row0 = pp * HEAD_PAIR + h * HEAD_DIM
        o_ref[0, row0:row0 + HEAD_DIM, qb * blk:(qb + 1) * blk] = (
            acc[0:HEAD_DIM] / acc[HEAD_DIM:HEAD_DIM + 1]).astype(bf16)


def _moba(slopes, qt, kaug, vaug, ksum, arows, cmask):
    b, _, s = qt.shape
    nblk = s // MOBA_BLOCK
    pairs = MOBA_HEADS // 2
    assert nblk + arows.shape[2] == K_FEAT and nblk % 2 == 0 and pairs % MOBA_STEP_PAIRS == 0
    npp = MOBA_STEP_PAIRS
    nchain = 4 * npp
    return pl.pallas_call(
        functools.partial(_moba_kernel, nblk=nblk),
        out_shape=jax.ShapeDtypeStruct((b, MOBA_WIDTH, s), bf16),
        grid=(b, pairs // npp, nblk // 2),
        in_specs=[
            pl.BlockSpec(memory_space=pltpu.SMEM),
            pl.BlockSpec((1, npp * HEAD_PAIR, MOBA_TILE), lambda bi, hp, n2: (bi, hp, n2)),
            pl.BlockSpec((1, npp, s, HEAD_PAIR + K_FEAT), lambda bi, hp, n2: (bi, hp, 0, 0)),
            pl.BlockSpec((1, 2 * npp, V_ROWS, s), lambda bi, hp, n2: (bi, hp, 0, 0)),
            pl.BlockSpec((1, nblk, npp * HEAD_PAIR), lambda bi, hp, n2: (bi, 0, hp)),
            pl.BlockSpec((npp, 2, K_FEAT - nblk, MOBA_BLOCK), lambda bi, hp, n2: (hp, 0, 0, 0)),
            _resident((2, MOBA_TILE, MOBA_BLOCK)),
        ],
        out_specs=pl.BlockSpec((1, npp * HEAD_PAIR, MOBA_TILE), lambda bi, hp, n2: (bi, hp, n2)),
        scratch_shapes=[
            pltpu.VMEM((nchain, HEAD_PAIR + K_FEAT, MOBA_BLOCK), bf16),
            pltpu.VMEM((nchain, MOBA_TILE, MOBA_BLOCK), f32),
            pltpu.VMEM((nchain, MOBA_TILE, MOBA_BLOCK), f32),
            pltpu.VMEM((nchain, 1, MOBA_BLOCK), f32),
            pltpu.VMEM((nchain, 1, MOBA_BLOCK), f32),
            pltpu.VMEM((nchain, V_ROWS, MOBA_BLOCK), f32),
            pltpu.VMEM((nchain, 1, MOBA_BLOCK), f32),
        ],
        compiler_params=pltpu.CompilerParams(
            dimension_semantics=("parallel", "parallel", "arbitrary"), vmem_limit_bytes=VMEM_LIMIT),
        name="moba",
    )(slopes, qt, kaug, vaug, ksum.reshape(b, nblk, MOBA_WIDTH), arows, cmask)


def _merge_kernel(x_ref, g_ref, d0_ref, d1_ref, d2_ref, ybt_ref, wgate_ref, wua_ref, wub_ref, wo_ref, o_ref,
                  nat_ref, *, tm):
    rows = GROUP_ROWS

    def token_order(g, c, part):
        d_ref, dil = (d0_ref, d1_ref, d2_ref)[g], DSA_GROUPS[g][1]
        per = rows // dil
        lanes = slice(c * LANES, (c + 1) * LANES)
        if dil == 1:
            return d_ref[0, 0, part * rows:(part + 1) * rows, lanes]
        for r in range(dil):
            nat_ref[g, c, pl.ds(part * rows + r, per, stride=dil), :] = (
                d_ref[0, r, part * per:(part + 1) * per, lanes])
        return nat_ref[g, c, part * rows:(part + 1) * rows, :]

    for part in range(tm // rows):
        sl = slice(part * rows, (part + 1) * rows)
        pairs = []
        for hp in range(2):
            outs = [token_order(g, 2 * hp, part) for g in range(len(DSA_GROUPS))]
            lses = [token_order(g, 2 * hp + 1, part) for g in range(len(DSA_GROUPS))]
            mx = jnp.maximum(jnp.maximum(lses[0], lses[1]), lses[2])
            es = [jnp.exp2(l - mx) for l in lses]
            den = es[0] + es[1] + es[2]
            pairs.append((es[0] * outs[0] + es[1] * outs[1] + es[2] * outs[2]) / den)
        ya = jnp.concatenate(pairs, axis=1).astype(bf16)
        x = x_ref[0, sl, :]
        h = _rmsnorm(x, g_ref[...]).astype(bf16)
        gates = jax.nn.sigmoid(_dot(h, wgate_ref[...]))
        ta = _dot(ya, wua_ref[...])
        tb = _dot_tn(ybt_ref[0, :, sl], wub_ref[...])
        merged = gates[:, 0:D_MODEL] * ta + gates[:, D_MODEL:] * tb
        o_ref[0, sl, :] = x + _dot(merged.astype(bf16), wo_ref[...])


def _merge(x3d, gain, dsa_outs, ybt, wgate, wua, wub, wo, *, tm):
    b, s, _ = x3d.shape
    tok = lambda w: pl.BlockSpec((1, tm, w), lambda bi, i: (bi, i, 0))
    grp = lambda dil: pl.BlockSpec((1, dil, tm // dil, 4 * LANES), lambda bi, i: (bi, 0, i, 0))
    return pl.pallas_call(
        functools.partial(_merge_kernel, tm=tm),
        out_shape=jax.ShapeDtypeStruct((b, s, D_MODEL), f32),
        grid=(b, s // tm),
        in_specs=[
            tok(D_MODEL), _resident((1, D_MODEL)), *[grp(dil) for _, dil in DSA_GROUPS],
            pl.BlockSpec((1, MOBA_WIDTH, tm), lambda bi, i: (bi, 0, i)),
            _resident((D_MODEL, 2 * D_MODEL)),
            _resident((DSA_OUT, D_MODEL)),
            _resident((MOBA_WIDTH, D_MODEL)),
            _resident((D_MODEL, D_MODEL)),
        ],
        out_specs=tok(D_MODEL),
        scratch_shapes=[pltpu.VMEM((len(DSA_GROUPS), 4, tm, LANES), f32)],
        compiler_params=pltpu.CompilerParams(
            dimension_semantics=("parallel", "parallel"), vmem_limit_bytes=VMEM_LIMIT),
        name="merge",
    )(x3d, gain, *dsa_outs, ybt, wgate, wua, wub, wo)


def _alibi_slopes(n):
    return 2.0 ** (-8.0 * np.arange(1, n + 1, dtype=np.float64) / n)


def _split3(x):
    hi = x.astype(bf16)
    r1 = x - hi.astype(f32)
    mid = r1.astype(bf16)
    lo = (r1 - mid.astype(f32)).astype(bf16)
    return hi, mid, lo


def _moba_constants(s):
    nblk = s // MOBA_BLOCK
    n_arow = K_FEAT - nblk
    pos = np.arange(s)
    kfeat = np.zeros((s, K_FEAT), np.float32)
    kfeat[pos, pos // MOBA_BLOCK] = 1.0
    kfeat[:, nblk:nblk + 3] = 1.0
    kfeat[:, nblk + 3:nblk + 6] = (pos % MOBA_BLOCK)[:, None]
    kfeat[:, nblk + 6:nblk + 9] = ((pos // MOBA_BLOCK) % 2)[:, None]
    slopes = jnp.asarray(_alibi_slopes(MOBA_HEADS) * LOG2E, f32)
    off = jnp.arange(MOBA_BLOCK, dtype=f32)
    wide = lambda v: jnp.broadcast_to(v[:, None], (MOBA_HEADS, MOBA_BLOCK))
    qterm = _split3(-slopes[:, None] * off[None, :])
    sterm = _split3(wide(slopes))
    bterm = _split3(wide(slopes * MOBA_BLOCK))
    rows = jnp.stack(list(qterm) + list(sterm) + list(bterm), axis=1)
    rows = jnp.concatenate(
        [rows, jnp.zeros((MOBA_HEADS, n_arow - 9, MOBA_BLOCK), bf16)], axis=1)
    arows = rows.reshape(MOBA_HEADS // 2, 2, n_arow, MOBA_BLOCK)
    r = jnp.arange(MOBA_TILE)[None, :, None]
    c = jnp.arange(MOBA_BLOCK)[None, None, :] + MOBA_BLOCK * jnp.arange(2)[:, None, None]
    cmask = jnp.where(r <= c, 0.0, NEG).astype(f32)
    return jnp.asarray(kfeat, bf16), arows, slopes, cmask


def _dsa_bias(group, dilation):
    slopes = _alibi_slopes(DSA_HEADS)[group * DSA_HEADS_PER_GROUP:(group + 1) * DSA_HEADS_PER_GROUP]
    slopes = jnp.asarray(slopes * LOG2E * dilation, f32).reshape(2, 2, 1, 1, 1)
    qi = jnp.arange(DSA_TQ)[:, None]
    kj = jnp.arange(DSA_BAND)[None, :]
    delta = jnp.stack([qi - kj, qi + DSA_BLK - kj])[None, None]
    valid = (delta >= 0) & (delta <= DSA_BLK)
    return jnp.where(valid, -slopes * delta.astype(f32), NEG)


def _layer(x, norm_ffn1, ffn1_gate, ffn1_up, ffn1_down, norm_mix, w_in, w_up_a, w_up_b, w_out,
           norm_ffn2, ffn2_gate, ffn2_up, ffn2_down, gain_final, *, final_norm):
    b, s, _ = x.shape
    tm = 512
    tm_merge = 1024
    tm_ffn = 1024
    row = lambda g: g.reshape(1, D_MODEL).astype(f32)
    qscale = HEAD_DIM ** -0.5 * LOG2E
    qa_scale = jnp.ones((1, 3 * DSA_WIDTH), f32).at[:, :DSA_WIDTH].set(qscale)
    x1, wa, wqb, wkb, wvb, wgate = _ffn(
        x.reshape(b * s, D_MODEL), row(norm_ffn1), ffn1_gate.astype(bf16), ffn1_up.astype(bf16),
        ffn1_down.astype(bf16), row(gain_final), final_norm=False, tm=tm_ffn, w_in=w_in,
        qa_scale=qa_scale, qb_scale=float(qscale))
    x1 = x1.reshape(b, s, D_MODEL)

    kfeat, arows, slopes_b, cmask = _moba_constants(s)
    vpad = jnp.zeros((V_ROWS - HEAD_DIM, tm), bf16).at[0].set(1.0)
    a0, a1, a2, kaug, qt, vaug, ksum, w2_gate, w2_up, w2_down, wua, wub, wo = _proj(
        x1, row(norm_mix), wa, wqb, wkb, wvb, kfeat, vpad,
        (ffn2_gate, ffn2_up, ffn2_down, w_up_a, w_up_b, w_out), tm=tm)

    dsa_outs = [
        _dsa_group(a, _dsa_bias(gi, dil), group=gi)
        for gi, (a, (_, dil)) in enumerate(zip((a0, a1, a2), DSA_GROUPS))
    ]
    ybt = _moba(slopes_b, qt, kaug, vaug, ksum, arows, cmask)
    x2 = _merge(x1, row(norm_mix), dsa_outs, ybt, wgate, wua, wub, wo, tm=tm_merge)
    x3 = _ffn(x2.reshape(b * s, D_MODEL), row(norm_ffn2), w2_gate, w2_up, w2_down, row(gain_final),
              final_norm=final_norm, tm=tm_ffn)
    return x3.reshape(b, s, D_MODEL)


def kernel(x, norm_ffn1, ffn1_gate, ffn1_up, ffn1_down, norm_mix, w_in, w_up_a, w_up_b, w_out,
           norm_ffn2, ffn2_gate, ffn2_up, ffn2_down, norm_final):
    depth = norm_ffn1.shape[0]
    for layer in range(depth):
        last = layer == depth - 1
        x = _layer(x, norm_ffn1[layer], ffn1_gate[layer], ffn1_up[layer], ffn1_down[layer],
                   norm_mix[layer], w_in[layer], w_up_a[layer], w_up_b[layer], w_out[layer],
                   norm_ffn2[layer], ffn2_gate[layer], ffn2_up[layer], ffn2_down[layer],
                   norm_final, final_norm=last)
    return x
```

```python
import functools

import numpy as np
import jax
import jax.numpy as jnp
from jax import lax
from jax.experimental import pallas as pl
from jax.experimental.pallas import tpu as pltpu

D_MODEL = 1024
HEAD_DIM = 64
DSA_GROUPS = ((128, 1), (512, 4), (2048, 16))
DSA_HEADS_PER_GROUP = 4
DSA_HEADS = DSA_HEADS_PER_GROUP * len(DSA_GROUPS)
MOBA_HEADS = 8
MOBA_BLOCK = 256
MOBA_TOPK = 3
D_FF = ((8 * D_MODEL // 3 + 127) // 128) * 128
DSA_WIDTH = DSA_HEADS * HEAD_DIM
MOBA_WIDTH = MOBA_HEADS * HEAD_DIM
DSA_OUT = DSA_HEADS_PER_GROUP * HEAD_DIM
EPS = 1e-6
NEG = -1e30

LANES = 128
HEAD_PAIR = 2 * HEAD_DIM
DSA_BLK = 128
DSA_TQ = DSA_BLK
DSA_BAND = DSA_TQ + DSA_BLK
DSA_STAGE_TILES = 2
DSA_GROUP_COLS = 3 * DSA_OUT
MOBA_TILE = 2 * MOBA_BLOCK
MAX_SLABS = 8
MOBA_STEP_PAIRS = 2
LOG2E = 1.4426950408889634
M_INIT = -3.0e38
V_ROWS = 80
K_FEAT = 128
VMEM_LIMIT = 56 * 1024 * 1024
GROUP_ROWS = 256
BF16_ROWS = 16

f32 = jnp.float32
bf16 = jnp.bfloat16


def _rmsnorm(x, g):
    return x * lax.rsqrt(jnp.mean(x * x, axis=-1, keepdims=True) + EPS) * g


def _dot(a, b):
    return jnp.dot(a, b, preferred_element_type=f32)


def _dot_nt(a, b):
    return lax.dot_general(a, b, (((1,), (1,)), ((), ())), preferred_element_type=f32)


def _dot_tn(a, b):
    return lax.dot_general(a, b, (((0,), (0,)), ((), ())), preferred_element_type=f32)


def _fold_lanes(x, op):
    tiles = [x[:, c * LANES:(c + 1) * LANES] for c in range(x.shape[-1] // LANES)]
    return functools.reduce(op, tiles)


def _resident(shape):
    nd = len(shape)
    return pl.BlockSpec(shape, lambda *_: (0,) * nd, pipeline_mode=pl.Buffered(1))


def _cast_rows(side_in, side_out, plan):
    for o_ref, (src, lo, hi, scale) in zip(side_out, plan):
        w = side_in[src][:, lo:hi]
        if isinstance(scale, int):
            w = w * side_in[scale][...]
        elif scale is not None:
            w = w * scale
        o_ref[...] = w.astype(bf16)


def _ffn_kernel(x_ref, g_ref, wg_ref, wu_ref, wd_ref, gf_ref, *rest, final_norm, n_side, plan):
    side_in, (o_ref, *side_out) = rest[:n_side], rest[n_side:]
    _cast_rows(side_in, side_out, plan)
    rows = GROUP_ROWS
    for part in range(x_ref.shape[0] // rows):
        sl = slice(part * rows, (part + 1) * rows)
        x = x_ref[sl, :]
        h = _rmsnorm(x, g_ref[...]).astype(bf16)
        gate = _dot(h, wg_ref[...])
        up = _dot(h, wu_ref[...])
        act = (gate * jax.nn.sigmoid(gate) * up).astype(bf16)
        y = x + 0.5 * _dot(act, wd_ref[...])
        if final_norm:
            y = _rmsnorm(y, gf_ref[...])
        o_ref[sl, :] = y


def _ffn(x2d, gain, w_gate, w_up, w_down, gain_final, *, final_norm, tm, w_in=None, qa_scale=None,
         qb_scale=None):
    n = x2d.shape[0]
    steps = n // tm
    side_args, side_specs, side_shapes, side_out_specs, plan = [], [], [], [], ()
    if w_in is not None:
        rows = w_in.shape[0] // steps
        assert rows * steps == w_in.shape[0] and rows % BF16_ROWS == 0
        block = lambda width: pl.BlockSpec((rows, width), lambda i: (i, 0))
        side_args = [w_in, qa_scale]
        side_specs = [block(w_in.shape[1]), _resident(qa_scale.shape)]
        qb0 = 3 * DSA_WIDTH
        plan = ((0, 0, qb0, 1),
                (0, qb0, qb0 + MOBA_WIDTH, qb_scale),
                (0, qb0 + MOBA_WIDTH, qb0 + 2 * MOBA_WIDTH, None),
                (0, qb0 + 2 * MOBA_WIDTH, qb0 + 3 * MOBA_WIDTH, None),
                (0, qb0 + 3 * MOBA_WIDTH, w_in.shape[1], None))
        for _, lo, hi, _ in plan:
            side_shapes.append(jax.ShapeDtypeStruct((w_in.shape[0], hi - lo), bf16))
            side_out_specs.append(block(hi - lo))
    out = pl.pallas_call(
        functools.partial(_ffn_kernel, final_norm=final_norm, n_side=len(side_args), plan=plan),
        out_shape=(jax.ShapeDtypeStruct((n, D_MODEL), f32), *side_shapes),
        grid=(steps,),
        in_specs=[
            pl.BlockSpec((tm, D_MODEL), lambda i: (i, 0)),
            _resident((1, D_MODEL)),
            _resident((D_MODEL, D_FF)),
            _resident((D_MODEL, D_FF)),
            _resident((D_FF, D_MODEL)),
            _resident((1, D_MODEL)),
            *side_specs,
        ],
        out_specs=(pl.BlockSpec((tm, D_MODEL), lambda i: (i, 0)), *side_out_specs),
        compiler_params=pltpu.CompilerParams(
            dimension_semantics=("parallel",), vmem_limit_bytes=VMEM_LIMIT),
        name="ffn_final" if final_norm else "ffn",
    )(x2d, gain, w_gate, w_up, w_down, gain_final, *side_args)
    return out if side_args else out[0]


def _proj_kernel(x_ref, g_ref, wa_ref, wqb_ref, wkb_ref, wvb_ref, kfeat_ref, vpad_ref, *rest,
                 tm, n_side, plan):
    side_in, rest = rest[:n_side], rest[n_side:]
    a0_ref, a1_ref, a2_ref, kaug_ref, qt_ref, vaug_ref, ksum_ref = rest[:7]
    side_out, scr_ref = rest[7:-1], rest[-1]

    @pl.when(pl.program_id(0) == 0)
    def _():
        _cast_rows(side_in, side_out, plan)

    rows = GROUP_ROWS
    tiles_per_group = DSA_GROUP_COLS // LANES
    for part in range(tm // rows):
        sl = slice(part * rows, (part + 1) * rows)
        h = _rmsnorm(x_ref[0, sl, :], g_ref[...]).astype(bf16)
        qkv = _dot(h, wa_ref[...])
        for g, (a_ref, (_, dil)) in enumerate(zip((a0_ref, a1_ref, a2_ref), DSA_GROUPS)):
            per = rows // dil
            for c in range(tiles_per_group):
                lo = (c // 2) * DSA_WIDTH + g * DSA_OUT + (c % 2) * LANES
                lanes = slice(c * LANES, (c + 1) * LANES)
                if dil == 1:
                    a_ref[0, 0, sl, lanes] = qkv[:, lo:lo + LANES].astype(bf16)
                    continue
                scr_ref[part, g, c] = qkv[:, lo:lo + LANES]
                for r in range(dil):
                    a_ref[0, r, part * per:(part + 1) * per, lanes] = (
                        scr_ref[part, g, c, pl.ds(r, per, stride=dil), :].astype(bf16))
        kb = _dot(h, wkb_ref[...])
        nb = rows // MOBA_BLOCK
        ksum_ref[part * nb:(part + 1) * nb] = (
            jnp.sum(kb.reshape(nb, MOBA_BLOCK, MOBA_WIDTH), axis=1)[:, None, :])
        kb = kb.astype(bf16)
        for p in range(MOBA_HEADS // 2):
            kaug_ref[0, p, sl, 0:HEAD_PAIR] = kb[:, p * HEAD_PAIR:(p + 1) * HEAD_PAIR]
            kaug_ref[0, p, sl, HEAD_PAIR:HEAD_PAIR + K_FEAT] = kfeat_ref[sl, :]
        qt_ref[0, :, sl] = _dot(h, wqb_ref[...]).T.astype(bf16)
        vt = _dot(h, wvb_ref[...]).T
        for hd in range(MOBA_HEADS):
            vaug_ref[0, hd, 0:HEAD_DIM, sl] = vt[hd * HEAD_DIM:(hd + 1) * HEAD_DIM].astype(bf16)
            vaug_ref[0, hd, HEAD_DIM:V_ROWS, sl] = vpad_ref[:, sl]


def _proj(x3d, gain, wa, wqb, wkb, wvb, kfeat, vpad, later_weights, *, tm):
    b, s, _ = x3d.shape
    nblk = s // MOBA_BLOCK
    steps = s // tm
    side_specs, side_shapes, side_out_specs, plan = [], [], [], []
    parked = lambda bi, i: (jnp.where(bi == 0, i, steps - 1), 0)
    for k, w in enumerate(later_weights):
        rows = w.shape[0] // steps
        assert rows * steps == w.shape[0] and rows % BF16_ROWS == 0
        spec = pl.BlockSpec((rows, w.shape[1]), parked)
        side_specs.append(spec)
        side_out_specs.append(spec)
        side_shapes.append(jax.ShapeDtypeStruct(w.shape, bf16))
        plan.append((k, 0, w.shape[1], None))
    grp_shapes = tuple(
        jax.ShapeDtypeStruct((b, dil, s // dil, DSA_GROUP_COLS), bf16) for _, dil in DSA_GROUPS)
    grp_specs = tuple(
        pl.BlockSpec((1, dil, tm // dil, DSA_GROUP_COLS), lambda bi, i: (bi, 0, i, 0))
        for _, dil in DSA_GROUPS)
    out_shape = grp_shapes + (
        jax.ShapeDtypeStruct((b, MOBA_HEADS // 2, s, HEAD_PAIR + K_FEAT), bf16),
        jax.ShapeDtypeStruct((b, MOBA_WIDTH, s), bf16),
        jax.ShapeDtypeStruct((b, MOBA_HEADS, V_ROWS, s), bf16),
        jax.ShapeDtypeStruct((b * nblk, 1, MOBA_WIDTH), f32),
    )
    per = tm // MOBA_BLOCK
    return pl.pallas_call(
        functools.partial(_proj_kernel, tm=tm, n_side=len(later_weights), plan=tuple(plan)),
        out_shape=out_shape + tuple(side_shapes),
        grid=(b, steps),
        in_specs=[
            pl.BlockSpec((1, tm, D_MODEL), lambda bi, i: (bi, i, 0)),
            _resident((1, D_MODEL)),
            _resident((D_MODEL, 3 * DSA_WIDTH)),
            _resident((D_MODEL, MOBA_WIDTH)),
            _resident((D_MODEL, MOBA_WIDTH)),
            _resident((D_MODEL, MOBA_WIDTH)),
            pl.BlockSpec((tm, K_FEAT), lambda bi, i: (i, 0)),
            _resident((V_ROWS - HEAD_DIM, tm)),
            *side_specs,
        ],
        out_specs=grp_specs + (
            pl.BlockSpec((1, MOBA_HEADS // 2, tm, HEAD_PAIR + K_FEAT), lambda bi, i: (bi, 0, i, 0)),
            pl.BlockSpec((1, MOBA_WIDTH, tm), lambda bi, i: (bi, 0, i)),
            pl.BlockSpec((1, MOBA_HEADS, V_ROWS, tm), lambda bi, i: (bi, 0, 0, i)),
            pl.BlockSpec((per, 1, MOBA_WIDTH), lambda bi, i: (bi * (s // tm) + i, 0, 0)),
        ) + tuple(side_out_specs),
        scratch_shapes=[pltpu.VMEM(
            (tm // GROUP_ROWS, len(DSA_GROUPS), DSA_GROUP_COLS // LANES, GROUP_ROWS, LANES), f32)],
        compiler_params=pltpu.CompilerParams(
            dimension_semantics=("arbitrary", "arbitrary"), vmem_limit_bytes=VMEM_LIMIT),
        name="proj",
    )(x3d, gain, wa, wqb, wkb, wvb, kfeat, vpad, *later_weights)


def _dsa_kernel(q_ref, k_ref, v_ref, bias_ref, o_ref, sa_ref, sb_ref, *, nstage, seg_tiles):
    tq, band = DSA_TQ, DSA_BAND
    first = lax.broadcasted_iota(jnp.int32, (tq, LANES), 1) < HEAD_DIM
    chains = [(u, h) for u in range(DSA_STAGE_TILES) for h in range(2)]

    def rows_of(t):
        return pl.ds(pl.multiple_of(t * tq, tq), tq)

    def seg_start(t):
        return (t % seg_tiles) == 0

    def band_of(t):
        start = jnp.where(seg_start(t), t * tq, t * tq - DSA_BLK)
        return pl.ds(pl.multiple_of(start, DSA_BLK), band)

    def scores(stage, s_ref):
        for u in range(DSA_STAGE_TILES):
            t = stage * DSA_STAGE_TILES + u
            q = q_ref[0, rows_of(t), :]
            kband = k_ref[0, band_of(t), :]
            for h in range(2):
                qm = jnp.where(first if h == 0 else jnp.logical_not(first), q, jnp.zeros_like(q))
                s_ref[2 * u + h] = _dot_nt(qm, kband)

    def finish(stage, s_ref):
        for u in range(DSA_STAGE_TILES):
            t = stage * DSA_STAGE_TILES + u
            variant = jnp.where(seg_start(t), 0, 1)
            vband = v_ref[0, band_of(t), :]
            outs, lses = [], []
            for h in range(2):
                s = s_ref[2 * u + h] + bias_ref[0, h, variant]
                m = jnp.max(_fold_lanes(s, jnp.maximum), axis=-1, keepdims=True)
                p = jnp.exp2(s - m)
                l = jnp.sum(_fold_lanes(p, jnp.add), axis=-1, keepdims=True)
                o = _dot(p.astype(bf16), vband)
                outs.append(o / l)
                lses.append(jnp.broadcast_to(m + jnp.log(l) * LOG2E, (tq, LANES)))
            o_ref[0, rows_of(t), 0:LANES] = jnp.where(first, outs[0], outs[1])
            o_ref[0, rows_of(t), LANES:2 * LANES] = jnp.where(first, lses[0], lses[1])

    zero = jnp.int32(0)
    scores(zero, sa_ref)

    def body(i2, carry):
        st = 2 * i2
        scores(st + 1, sb_ref)
        finish(st, sa_ref)
        scores(st + 2, sa_ref)
        finish(st + 1, sb_ref)
        return carry

    lax.fori_loop(0, nstage // 2 - 1, body, 0)
    st_end = zero + (nstage - 2)
    scores(st_end + 1, sb_ref)
    finish(st_end, sa_ref)
    finish(st_end + 1, sb_ref)


def _dsa_group(a, bias, *, group):
    b, dil, length, _ = a.shape
    s = dil * length
    pairs = DSA_HEADS_PER_GROUP // 2
    seg_tiles = length // DSA_TQ
    nstage = s // (DSA_TQ * DSA_STAGE_TILES)
    assert seg_tiles >= 2 and length % DSA_TQ == 0 and nstage % 2 == 0
    flat = a.reshape(b, s, DSA_GROUP_COLS)
    nchain = 2 * DSA_STAGE_TILES
    col = lambda part: (lambda bi, hp: (bi, 0, part * pairs + hp))
    out = pl.pallas_call(
        functools.partial(_dsa_kernel, nstage=nstage, seg_tiles=seg_tiles),
        out_shape=jax.ShapeDtypeStruct((b, s, 2 * pairs * LANES), f32),
        grid=(b, pairs),
        in_specs=[
            pl.BlockSpec((1, s, LANES), col(0)),
            pl.BlockSpec((1, s, LANES), col(1)),
            pl.BlockSpec((1, s, LANES), col(2)),
            pl.BlockSpec((1, 2, 2, DSA_TQ, DSA_BAND), lambda bi, hp: (hp, 0, 0, 0, 0)),
        ],
        out_specs=pl.BlockSpec((1, s, 2 * LANES), lambda bi, hp: (bi, 0, hp)),
        scratch_shapes=[pltpu.VMEM((nchain, DSA_TQ, DSA_BAND), f32),
                        pltpu.VMEM((nchain, DSA_TQ, DSA_BAND), f32)],
        compiler_params=pltpu.CompilerParams(
            dimension_semantics=("parallel", "parallel"), vmem_limit_bytes=VMEM_LIMIT),
        name=f"dsa_g{group}",
    )(flat, flat, flat, bias)
    return out.reshape(b, dil, length, 2 * pairs * LANES)


def _moba_kernel(slope_ref, qt_ref, kaug_ref, vaug_ref, ksum_ref, arow_ref, cmask_ref, o_ref,
                 rhs_ref, sa_ref, sb_ref, mta_ref, mtb_ref, acc_ref, m_ref, *, nblk):
    hp0 = pl.program_id(1) * MOBA_STEP_PAIRS
    n2 = pl.program_id(2)
    blk = MOBA_BLOCK
    tile = MOBA_TILE
    ntile = n2 + 1
    chains = [(pp, qb, h) for pp in range(MOBA_STEP_PAIRS) for qb in range(2) for h in range(2)]
    slot = {chain: c for c, chain in enumerate(chains)}
    rowi = lax.broadcasted_iota(jnp.int32, (HEAD_PAIR, tile), 0)
    bidx = lax.broadcasted_iota(jnp.int32, (nblk, tile), 0)
    own = 2 * n2 + (lax.broadcasted_iota(jnp.int32, (nblk, tile), 1) >= blk).astype(jnp.int32)
    past = bidx < own

    gates = {}
    for pp in range(MOBA_STEP_PAIRS):
        qpair = qt_ref[0, pp * HEAD_PAIR:(pp + 1) * HEAD_PAIR, :]
        kmean = ksum_ref[0, :, pp * HEAD_PAIR:(pp + 1) * HEAD_PAIR] * (1.0 / blk)
        kmean_hi = kmean.astype(bf16)
        kmean_lo = (kmean - kmean_hi.astype(f32)).astype(bf16)
        for h in range(2):
            mine = (rowi >= HEAD_DIM) if h else (rowi < HEAD_DIM)
            qpad = jnp.where(mine, qpair, jnp.zeros_like(qpair))
            gates[pp, h] = _dot(kmean_hi, qpad) + _dot(kmean_lo, qpad)
            for qb in range(2):
                c = slot[pp, qb, h]
                rhs_ref[c, 0:HEAD_PAIR, :] = qpad[:, qb * blk:(qb + 1) * blk]
                rhs_ref[c, HEAD_PAIR:HEAD_PAIR + nblk, :] = jnp.zeros((nblk, blk), bf16)
                rhs_ref[c, HEAD_PAIR + nblk:HEAD_PAIR + K_FEAT, :] = arow_ref[pp, h]
                acc_ref[c] = jnp.zeros((V_ROWS, blk), f32)
                m_ref[c] = jnp.full((1, blk), M_INIT, f32)

    def key_max(s):
        slabs = s.shape[0] // (tile // MAX_SLABS)
        smax = jnp.max(s.reshape(slabs, tile // MAX_SLABS, blk), axis=0) if slabs > 1 else s
        return jnp.max(smax, axis=0, keepdims=True)

    buf_a, buf_b = (sa_ref, mta_ref), (sb_ref, mtb_ref)

    def scores(i, buf):
        s_ref, mt_ref = buf
        k0 = pl.multiple_of(i * tile, tile)
        for pp in range(MOBA_STEP_PAIRS):
            kt = kaug_ref[0, pp, pl.ds(k0, tile), :]
            for qb in range(2):
                for h in range(2):
                    c = slot[pp, qb, h]
                    s = _dot(kt, rhs_ref[c])
                    s_ref[c] = s
                    mt_ref[c] = key_max(s)

    scores(0, buf_a)

    for (pp, h), gate in gates.items():
        g = jnp.where(past, gate, NEG)
        sel = bidx == own
        for _ in range(MOBA_TOPK):
            mx = jnp.max(g, axis=0, keepdims=True)
            idx = jnp.min(jnp.where(g == mx, bidx, nblk), axis=0, keepdims=True)
            pick = bidx == idx
            sel = sel | (pick & past)
            g = jnp.where(pick, -jnp.inf, g)
        maskbias = jnp.where(sel, 0.0, NEG).astype(bf16)
        for qb in range(2):
            c = slot[pp, qb, h]
            mb = maskbias[:, qb * blk:(qb + 1) * blk]
            rhs_ref[c, HEAD_PAIR:HEAD_PAIR + nblk, :] = mb
            tile_max = []
            for kb in range(tile // blk):
                s = sa_ref[c, kb * blk:(kb + 1) * blk, :] + mb[kb:kb + 1, :].astype(f32)
                sa_ref[c, kb * blk:(kb + 1) * blk, :] = s
                tile_max.append(key_max(s))
            mta_ref[c] = functools.reduce(jnp.maximum, tile_max)

    def update(i, buf, diagonal):
        s_ref, mt_ref = buf
        k0 = pl.multiple_of(i * tile, tile)
        for c, (pp, qb, h) in enumerate(chains):
            if diagonal:
                keys = blk * (qb + 1)
                s = s_ref[c, 0:keys, :] + cmask_ref[qb, 0:keys, :]
                tile_max = key_max(s)
            else:
                keys = tile
                s = s_ref[c]
                tile_max = mt_ref[c]
            rel = tile * (i - n2) - blk * qb
            shift = slope_ref[2 * (hp0 + pp) + h] * rel.astype(f32)
            m_old = m_ref[c]
            m_new = jnp.maximum(m_old, tile_max + shift)
            p = jnp.exp2(s - (m_new - shift)).astype(bf16)
            pv = _dot(vaug_ref[0, 2 * pp + h, :, pl.ds(k0, keys)], p)
            acc_ref[c] = jnp.exp2(m_old - m_new) * acc_ref[c] + pv
            m_ref[c] = m_new

    npair = (ntile - 1) // 2

    def body(i2, carry):
        t = 2 * i2
        scores(t + 1, buf_b)
        update(t, buf_a, False)
        scores(t + 2, buf_a)
        update(t + 1, buf_b, False)
        return carry

    lax.fori_loop(0, npair, body, 0)
    t_end = 2 * npair

    @pl.when(ntile - t_end == 1)
    def _():
        update(t_end, buf_a, True)

    @pl.when(ntile - t_end == 2)
    def _():
        scores(t_end + 1, buf_b)
        update(t_end, buf_a, False)
        update(t_end + 1, buf_b, True)

    for c, (pp, qb, h) in enumerate(chains):
        acc = acc_ref[c]
        row0 = pp * HEAD_PAIR + h * HEAD_DIM
        o_ref[0, row0:row0 + HEAD_DIM, qb * blk:(qb + 1) * blk] = (
            acc[0:HEAD_DIM] / acc[HEAD_DIM:HEAD_DIM + 1]).astype(bf16)


def _moba(slopes, qt, kaug, vaug, ksum, arows, cmask):
    b, _, s = qt.shape
    nblk = s // MOBA_BLOCK
    pairs = MOBA_HEADS // 2
    assert nblk + arows.shape[2] == K_FEAT and nblk % 2 == 0 and pairs % MOBA_STEP_PAIRS == 0
    npp = MOBA_STEP_PAIRS
    nchain = 4 * npp
    return pl.pallas_call(
        functools.partial(_moba_kernel, nblk=nblk),
        out_shape=jax.ShapeDtypeStruct((b, MOBA_WIDTH, s), bf16),
        grid=(b, pairs // npp, nblk // 2),
        in_specs=[
            pl.BlockSpec(memory_space=pltpu.SMEM),
            pl.BlockSpec((1, npp * HEAD_PAIR, MOBA_TILE), lambda bi, hp, n2: (bi, hp, n2)),
            pl.BlockSpec((1, npp, s, HEAD_PAIR + K_FEAT), lambda bi, hp, n2: (bi, hp, 0, 0)),
            pl.BlockSpec((1, 2 * npp, V_ROWS, s), lambda bi, hp, n2: (bi, hp, 0, 0)),
            pl.BlockSpec((1, nblk, npp * HEAD_PAIR), lambda bi, hp, n2: (bi, 0, hp)),
            pl.BlockSpec((npp, 2, K_FEAT - nblk, MOBA_BLOCK), lambda bi, hp, n2: (hp, 0, 0, 0)),
            _resident((2, MOBA_TILE, MOBA_BLOCK)),
        ],
        out_specs=pl.BlockSpec((1, npp * HEAD_PAIR, MOBA_TILE), lambda bi, hp, n2: (bi, hp, n2)),
        scratch_shapes=[
            pltpu.VMEM((nchain, HEAD_PAIR + K_FEAT, MOBA_BLOCK), bf16),
            pltpu.VMEM((nchain, MOBA_TILE, MOBA_BLOCK), f32),
            pltpu.VMEM((nchain, MOBA_TILE, MOBA_BLOCK), f32),
            pltpu.VMEM((nchain, 1, MOBA_BLOCK), f32),
            pltpu.VMEM((nchain, 1, MOBA_BLOCK), f32),
            pltpu.VMEM((nchain, V_ROWS, MOBA_BLOCK), f32),
            pltpu.VMEM((nchain, 1, MOBA_BLOCK), f32),
        ],
        compiler_params=pltpu.CompilerParams(
            dimension_semantics=("parallel", "parallel", "arbitrary"), vmem_limit_bytes=VMEM_LIMIT),
        name="moba",
    )(slopes, qt, kaug, vaug, ksum.reshape(b, nblk, MOBA_WIDTH), arows, cmask)


def _merge_kernel(x_ref, g_ref, d0_ref, d1_ref, d2_ref, ybt_ref, wgate_ref, wua_ref, wub_ref, wo_ref, o_ref,
                  nat_ref, *, tm):
    rows = GROUP_ROWS

    def token_order(g, c, part):
        d_ref, dil = (d0_ref, d1_ref, d2_ref)[g], DSA_GROUPS[g][1]
        per = rows // dil
        lanes = slice(c * LANES, (c + 1) * LANES)
        if dil == 1:
            return d_ref[0, 0, part * rows:(part + 1) * rows, lanes]
        for r in range(dil):
            nat_ref[g, c, pl.ds(part * rows + r, per, stride=dil), :] = (
                d_ref[0, r, part * per:(part + 1) * per, lanes])
        return nat_ref[g, c, part * rows:(part + 1) * rows, :]

    for part in range(tm // rows):
        sl = slice(part * rows, (part + 1) * rows)
        pairs = []
        for hp in range(2):
            outs = [token_order(g, 2 * hp, part) for g in range(len(DSA_GROUPS))]
            lses = [token_order(g, 2 * hp + 1, part) for g in range(len(DSA_GROUPS))]
            mx = jnp.maximum(jnp.maximum(lses[0], lses[1]), lses[2])
            es = [jnp.exp2(l - mx) for l in lses]
            den = es[0] + es[1] + es[2]
            pairs.append((es[0] * outs[0] + es[1] * outs[1] + es[2] * outs[2]) / den)
        ya = jnp.concatenate(pairs, axis=1).astype(bf16)
        x = x_ref[0, sl, :]
        h = _rmsnorm(x, g_ref[...]).astype(bf16)
        gates = jax.nn.sigmoid(_dot(h, wgate_ref[...]))
        ta = _dot(ya, wua_ref[...])
        tb = _dot_tn(ybt_ref[0, :, sl], wub_ref[...])
        merged = gates[:, 0:D_MODEL] * ta + gates[:, D_MODEL:] * tb
        o_ref[0, sl, :] = x + _dot(merged.astype(bf16), wo_ref[...])


def _merge(x3d, gain, dsa_outs, ybt, wgate, wua, wub, wo, *, tm):
    b, s, _ = x3d.shape
    tok = lambda w: pl.BlockSpec((1, tm, w), lambda bi, i: (bi, i, 0))
    grp = lambda dil: pl.BlockSpec((1, dil, tm // dil, 4 * LANES), lambda bi, i: (bi, 0, i, 0))
    return pl.pallas_call(
        functools.partial(_merge_kernel, tm=tm),
        out_shape=jax.ShapeDtypeStruct((b, s, D_MODEL), f32),
        grid=(b, s // tm),
        in_specs=[
            tok(D_MODEL), _resident((1, D_MODEL)), *[grp(dil) for _, dil in DSA_GROUPS],
            pl.BlockSpec((1, MOBA_WIDTH, tm), lambda bi, i: (bi, 0, i)),
            _resident((D_MODEL, 2 * D_MODEL)),
            _resident((DSA_OUT, D_MODEL)),
            _resident((MOBA_WIDTH, D_MODEL)),
            _resident((D_MODEL, D_MODEL)),
        ],
        out_specs=tok(D_MODEL),
        scratch_shapes=[pltpu.VMEM((len(DSA_GROUPS), 4, tm, LANES), f32)],
        compiler_params=pltpu.CompilerParams(
            dimension_semantics=("parallel", "parallel"), vmem_limit_bytes=VMEM_LIMIT),
        name="merge",
    )(x3d, gain, *dsa_outs, ybt, wgate, wua, wub, wo)


def _alibi_slopes(n):
    return 2.0 ** (-8.0 * np.arange(1, n + 1, dtype=np.float64) / n)


def _split3(x):
    x = np.asarray(x, np.float32)
    hi = x.astype(bf16)
    r1 = x - hi.astype(np.float32)
    mid = r1.astype(bf16)
    lo = (r1 - mid.astype(np.float32)).astype(bf16)
    return hi, mid, lo


def _moba_constants(s):
    nblk = s // MOBA_BLOCK
    n_arow = K_FEAT - nblk
    pos = np.arange(s)
    kfeat = np.zeros((s, K_FEAT), np.float32)
    kfeat[pos, pos // MOBA_BLOCK] = 1.0
    kfeat[:, nblk:nblk + 3] = 1.0
    kfeat[:, nblk + 3:nblk + 6] = (pos % MOBA_BLOCK)[:, None]
    kfeat[:, nblk + 6:nblk + 9] = ((pos // MOBA_BLOCK) % 2)[:, None]
    slopes = (_alibi_slopes(MOBA_HEADS) * LOG2E).astype(np.float32)
    off = np.arange(MOBA_BLOCK, dtype=np.float32)
    wide = lambda v: np.broadcast_to(v[:, None], (MOBA_HEADS, MOBA_BLOCK))
    qterm = _split3(-slopes[:, None] * off[None, :])
    sterm = _split3(wide(slopes))
    bterm = _split3(wide(slopes * np.float32(MOBA_BLOCK)))
    rows = np.zeros((MOBA_HEADS, n_arow, MOBA_BLOCK), bf16)
    rows[:, 0:9] = np.stack(list(qterm) + list(sterm) + list(bterm), axis=1)
    arows = rows.reshape(MOBA_HEADS // 2, 2, n_arow, MOBA_BLOCK)
    r = np.arange(MOBA_TILE)[None, :, None]
    c = np.arange(MOBA_BLOCK)[None, None, :] + MOBA_BLOCK * np.arange(2)[:, None, None]
    cmask = np.where(r <= c, 0.0, NEG).astype(np.float32)
    return jnp.asarray(kfeat.astype(bf16)), jnp.asarray(arows), jnp.asarray(slopes), jnp.asarray(cmask)


def _dsa_bias(group, dilation):
    slopes = _alibi_slopes(DSA_HEADS)[group * DSA_HEADS_PER_GROUP:(group + 1) * DSA_HEADS_PER_GROUP]
    slopes = (slopes * LOG2E * dilation).astype(np.float32).reshape(2, 2, 1, 1, 1)
    qi = np.arange(DSA_TQ)[:, None]
    kj = np.arange(DSA_BAND)[None, :]
    delta = np.stack([qi - kj, qi + DSA_BLK - kj])[None, None]
    valid = (delta >= 0) & (delta <= DSA_BLK)
    return jnp.asarray(np.where(valid, -slopes * delta.astype(np.float32), NEG).astype(np.float32))


def _layer(x, norm_ffn1, ffn1_gate, ffn1_up, ffn1_down, norm_mix, w_in, w_up_a, w_up_b, w_out,
           norm_ffn2, ffn2_gate, ffn2_up, ffn2_down, gain_final, *, final_norm):
    b, s, _ = x.shape
    tm = 512
    tm_merge = 1024
    tm_ffn = 1024
    row = lambda g: g.reshape(1, D_MODEL).astype(f32)
    qscale = HEAD_DIM ** -0.5 * LOG2E
    qa_scale = np.ones((1, 3 * DSA_WIDTH), np.float32)
    qa_scale[:, :DSA_WIDTH] = qscale
    qa_scale = jnp.asarray(qa_scale)
    x1, wa, wqb, wkb, wvb, wgate = _ffn(
        x.reshape(b * s, D_MODEL), row(norm_ffn1), ffn1_gate.astype(bf16), ffn1_up.astype(bf16),
        ffn1_down.astype(bf16), row(gain_final), final_norm=False, tm=tm_ffn, w_in=w_in,
        qa_scale=qa_scale, qb_scale=float(qscale))
    x1 = x1.reshape(b, s, D_MODEL)

    kfeat, arows, slopes_b, cmask = _moba_constants(s)
    vpad = np.zeros((V_ROWS - HEAD_DIM, tm), np.float32)
    vpad[0] = 1.0
    vpad = jnp.asarray(vpad.astype(bf16))
    a0, a1, a2, kaug, qt, vaug, ksum, w2_gate, w2_up, w2_down, wua, wub, wo = _proj(
        x1, row(norm_mix), wa, wqb, wkb, wvb, kfeat, vpad,
        (ffn2_gate, ffn2_up, ffn2_down, w_up_a, w_up_b, w_out), tm=tm)

    dsa_outs = [
        _dsa_group(a, _dsa_bias(gi, dil), group=gi)
        for gi, (a, (_, dil)) in enumerate(zip((a0, a1, a2), DSA_GROUPS))
    ]
    ybt = _moba(slopes_b, qt, kaug, vaug, ksum, arows, cmask)
    x2 = _merge(x1, row(norm_mix), dsa_outs, ybt, wgate, wua, wub, wo, tm=tm_merge)
    x3 = _ffn(x2.reshape(b * s, D_MODEL), row(norm_ffn2), w2_gate, w2_up, w2_down, row(gain_final),
              final_norm=final_norm, tm=tm_ffn)
    return x3.reshape(b, s, D_MODEL)


def kernel(x, norm_ffn1, ffn1_gate, ffn1_up, ffn1_down, norm_mix, w_in, w_up_a, w_up_b, w_out,
           norm_ffn2, ffn2_gate, ffn2_up, ffn2_down, norm_final):
    depth = norm_ffn1.shape[0]
    for layer in range(depth):
        last = layer == depth - 1
        x = _layer(x, norm_ffn1[layer], ffn1_gate[layer], ffn1_up[layer], ffn1_down[layer],
                   norm_mix[layer], w_in[layer], w_up_a[layer], w_up_b[layer], w_out[layer],
                   norm_ffn2[layer], ffn2_gate[layer], ffn2_up[layer], ffn2_down[layer],
                   norm_final, final_norm=last)
    return x
```

```python
import functools

import numpy as np
import jax
import jax.numpy as jnp
from jax import lax
from jax.experimental import pallas as pl
from jax.experimental.pallas import tpu as pltpu

D_MODEL = 1024
HEAD_DIM = 64
DSA_GROUPS = ((128, 1), (512, 4), (2048, 16))
DSA_HEADS_PER_GROUP = 4
DSA_HEADS = DSA_HEADS_PER_GROUP * len(DSA_GROUPS)
MOBA_HEADS = 8
MOBA_BLOCK = 256
MOBA_TOPK = 3
D_FF = ((8 * D_MODEL // 3 + 127) // 128) * 128
DSA_WIDTH = DSA_HEADS * HEAD_DIM
MOBA_WIDTH = MOBA_HEADS * HEAD_DIM
DSA_OUT = DSA_HEADS_PER_GROUP * HEAD_DIM
EPS = 1e-6
NEG = -1e30

LANES = 128
HEAD_PAIR = 2 * HEAD_DIM
DSA_BLK = 128
DSA_TQ = DSA_BLK
DSA_BAND = DSA_TQ + DSA_BLK
DSA_STAGE_TILES = 2
DSA_GROUP_COLS = 3 * DSA_OUT
MOBA_TILE = 2 * MOBA_BLOCK
MAX_SLABS = 8
MOBA_STEP_PAIRS = 2
MOBA_UNROLL = 4
LOG2E = 1.4426950408889634
M_INIT = -3.0e38
V_ROWS = 80
K_FEAT = 128
VMEM_LIMIT = 56 * 1024 * 1024
GROUP_ROWS = 256
BF16_ROWS = 16

f32 = jnp.float32
bf16 = jnp.bfloat16


def _rmsnorm(x, g):
    return x * lax.rsqrt(jnp.mean(x * x, axis=-1, keepdims=True) + EPS) * g


def _dot(a, b):
    return jnp.dot(a, b, preferred_element_type=f32)


def _dot_nt(a, b):
    return lax.dot_general(a, b, (((1,), (1,)), ((), ())), preferred_element_type=f32)


def _dot_tn(a, b):
    return lax.dot_general(a, b, (((0,), (0,)), ((), ())), preferred_element_type=f32)


def _fold_lanes(x, op):
    tiles = [x[:, c * LANES:(c + 1) * LANES] for c in range(x.shape[-1] // LANES)]
    return functools.reduce(op, tiles)


def _resident(shape):
    nd = len(shape)
    return pl.BlockSpec(shape, lambda *_: (0,) * nd, pipeline_mode=pl.Buffered(1))


def _cast_rows(side_in, side_out, plan):
    for o_ref, (src, lo, hi, scale) in zip(side_out, plan):
        w = side_in[src][:, lo:hi]
        if isinstance(scale, int):
            w = w * side_in[scale][...]
        elif scale is not None:
            w = w * scale
        o_ref[...] = w.astype(bf16)


def _ffn_kernel(x_ref, g_ref, wg_ref, wu_ref, wd_ref, gf_ref, *rest, final_norm, n_side, plan):
    side_in, (o_ref, *side_out) = rest[:n_side], rest[n_side:]
    _cast_rows(side_in, side_out, plan)
    rows = GROUP_ROWS
    for part in range(x_ref.shape[0] // rows):
        sl = slice(part * rows, (part + 1) * rows)
        x = x_ref[sl, :]
        h = _rmsnorm(x, g_ref[...]).astype(bf16)
        gate = _dot(h, wg_ref[...])
        up = _dot(h, wu_ref[...])
        act = (gate * jax.nn.sigmoid(gate) * up).astype(bf16)
        y = x + 0.5 * _dot(act, wd_ref[...])
        if final_norm:
            y = _rmsnorm(y, gf_ref[...])
        o_ref[sl, :] = y


def _ffn(x2d, gain, w_gate, w_up, w_down, gain_final, *, final_norm, tm, w_in=None, qa_scale=None,
         qb_scale=None):
    n = x2d.shape[0]
    steps = n // tm
    side_args, side_specs, side_shapes, side_out_specs, plan = [], [], [], [], ()
    if w_in is not None:
        rows = w_in.shape[0] // steps
        assert rows * steps == w_in.shape[0] and rows % BF16_ROWS == 0
        block = lambda width: pl.BlockSpec((rows, width), lambda i: (i, 0))
        side_args = [w_in, qa_scale]
        side_specs = [block(w_in.shape[1]), _resident(qa_scale.shape)]
        qb0 = 3 * DSA_WIDTH
        plan = ((0, 0, qb0, 1),
                (0, qb0, qb0 + MOBA_WIDTH, qb_scale),
                (0, qb0 + MOBA_WIDTH, qb0 + 2 * MOBA_WIDTH, None),
                (0, qb0 + 2 * MOBA_WIDTH, qb0 + 3 * MOBA_WIDTH, None),
                (0, qb0 + 3 * MOBA_WIDTH, w_in.shape[1], None))
        for _, lo, hi, _ in plan:
            side_shapes.append(jax.ShapeDtypeStruct((w_in.shape[0], hi - lo), bf16))
            side_out_specs.append(block(hi - lo))
    out = pl.pallas_call(
        functools.partial(_ffn_kernel, final_norm=final_norm, n_side=len(side_args), plan=plan),
        out_shape=(jax.ShapeDtypeStruct((n, D_MODEL), f32), *side_shapes),
        grid=(steps,),
        in_specs=[
            pl.BlockSpec((tm, D_MODEL), lambda i: (i, 0)),
            _resident((1, D_MODEL)),
            _resident((D_MODEL, D_FF)),
            _resident((D_MODEL, D_FF)),
            _resident((D_FF, D_MODEL)),
            _resident((1, D_MODEL)),
            *side_specs,
        ],
        out_specs=(pl.BlockSpec((tm, D_MODEL), lambda i: (i, 0)), *side_out_specs),
        compiler_params=pltpu.CompilerParams(
            dimension_semantics=("parallel",), vmem_limit_bytes=VMEM_LIMIT),
        name="ffn_final" if final_norm else "ffn",
    )(x2d, gain, w_gate, w_up, w_down, gain_final, *side_args)
    return out if side_args else out[0]


def _proj_kernel(x_ref, g_ref, wa_ref, wqb_ref, wkb_ref, wvb_ref, kfeat_ref, vpad_ref, *rest,
                 tm, n_side, plan):
    side_in, rest = rest[:n_side], rest[n_side:]
    a0_ref, a1_ref, a2_ref, kaug_ref, qt_ref, vaug_ref, ksum_ref = rest[:7]
    side_out, scr_ref = rest[7:-1], rest[-1]

    @pl.when(pl.program_id(0) == 0)
    def _():
        _cast_rows(side_in, side_out, plan)

    rows = GROUP_ROWS
    tiles_per_group = DSA_GROUP_COLS // LANES
    for part in range(tm // rows):
        sl = slice(part * rows, (part + 1) * rows)
        h = _rmsnorm(x_ref[0, sl, :], g_ref[...]).astype(bf16)
        qkv = _dot(h, wa_ref[...])
        for g, (a_ref, (_, dil)) in enumerate(zip((a0_ref, a1_ref, a2_ref), DSA_GROUPS)):
            per = rows // dil
            for c in range(tiles_per_group):
                lo = (c // 2) * DSA_WIDTH + g * DSA_OUT + (c % 2) * LANES
                lanes = slice(c * LANES, (c + 1) * LANES)
                if dil == 1:
                    a_ref[0, 0, sl, lanes] = qkv[:, lo:lo + LANES].astype(bf16)
                    continue
                scr_ref[part, g, c] = qkv[:, lo:lo + LANES]
                for r in range(dil):
                    a_ref[0, r, part * per:(part + 1) * per, lanes] = (
                        scr_ref[part, g, c, pl.ds(r, per, stride=dil), :].astype(bf16))
        kb = _dot(h, wkb_ref[...])
        nb = rows // MOBA_BLOCK
        ksum_ref[part * nb:(part + 1) * nb] = (
            jnp.sum(kb.reshape(nb, MOBA_BLOCK, MOBA_WIDTH), axis=1)[:, None, :])
        kb = kb.astype(bf16)
        for p in range(MOBA_HEADS // 2):
            kaug_ref[0, p, sl, 0:HEAD_PAIR] = kb[:, p * HEAD_PAIR:(p + 1) * HEAD_PAIR]
            kaug_ref[0, p, sl, HEAD_PAIR:HEAD_PAIR + K_FEAT] = kfeat_ref[sl, :]
        qt_ref[0, :, sl] = _dot(h, wqb_ref[...]).T.astype(bf16)
        vt = _dot(h, wvb_ref[...]).T
        for hd in range(MOBA_HEADS):
            vaug_ref[0, hd, 0:HEAD_DIM, sl] = vt[hd * HEAD_DIM:(hd + 1) * HEAD_DIM].astype(bf16)
            vaug_ref[0, hd, HEAD_DIM:V_ROWS, sl] = vpad_ref[:, sl]


def _proj(x3d, gain, wa, wqb, wkb, wvb, kfeat, vpad, later_weights, *, tm):
    b, s, _ = x3d.shape
    nblk = s // MOBA_BLOCK
    steps = s // tm
    side_specs, side_shapes, side_out_specs, plan = [], [], [], []
    parked = lambda bi, i: (jnp.where(bi == 0, i, steps - 1), 0)
    for k, w in enumerate(later_weights):
        rows = w.shape[0] // steps
        assert rows * steps == w.shape[0] and rows % BF16_ROWS == 0
        spec = pl.BlockSpec((rows, w.shape[1]), parked)
        side_specs.append(spec)
        side_out_specs.append(spec)
        side_shapes.append(jax.ShapeDtypeStruct(w.shape, bf16))
        plan.append((k, 0, w.shape[1], None))
    grp_shapes = tuple(
        jax.ShapeDtypeStruct((b, dil, s // dil, DSA_GROUP_COLS), bf16) for _, dil in DSA_GROUPS)
    grp_specs = tuple(
        pl.BlockSpec((1, dil, tm // dil, DSA_GROUP_COLS), lambda bi, i: (bi, 0, i, 0))
        for _, dil in DSA_GROUPS)
    out_shape = grp_shapes + (
        jax.ShapeDtypeStruct((b, MOBA_HEADS // 2, s, HEAD_PAIR + K_FEAT), bf16),
        jax.ShapeDtypeStruct((b, MOBA_WIDTH, s), bf16),
        jax.ShapeDtypeStruct((b, MOBA_HEADS, V_ROWS, s), bf16),
        jax.ShapeDtypeStruct((b * nblk, 1, MOBA_WIDTH), f32),
    )
    per = tm // MOBA_BLOCK
    return pl.pallas_call(
        functools.partial(_proj_kernel, tm=tm, n_side=len(later_weights), plan=tuple(plan)),
        out_shape=out_shape + tuple(side_shapes),
        grid=(b, steps),
        in_specs=[
            pl.BlockSpec((1, tm, D_MODEL), lambda bi, i: (bi, i, 0)),
            _resident((1, D_MODEL)),
            _resident((D_MODEL, 3 * DSA_WIDTH)),
            _resident((D_MODEL, MOBA_WIDTH)),
            _resident((D_MODEL, MOBA_WIDTH)),
            _resident((D_MODEL, MOBA_WIDTH)),
            pl.BlockSpec((tm, K_FEAT), lambda bi, i: (i, 0)),
            _resident((V_ROWS - HEAD_DIM, tm)),
            *side_specs,
        ],
        out_specs=grp_specs + (
            pl.BlockSpec((1, MOBA_HEADS // 2, tm, HEAD_PAIR + K_FEAT), lambda bi, i: (bi, 0, i, 0)),
            pl.BlockSpec((1, MOBA_WIDTH, tm), lambda bi, i: (bi, 0, i)),
            pl.BlockSpec((1, MOBA_HEADS, V_ROWS, tm), lambda bi, i: (bi, 0, 0, i)),
            pl.BlockSpec((per, 1, MOBA_WIDTH), lambda bi, i: (bi * (s // tm) + i, 0, 0)),
        ) + tuple(side_out_specs),
        scratch_shapes=[pltpu.VMEM(
            (tm // GROUP_ROWS, len(DSA_GROUPS), DSA_GROUP_COLS // LANES, GROUP_ROWS, LANES), f32)],
        compiler_params=pltpu.CompilerParams(
            dimension_semantics=("arbitrary", "arbitrary"), vmem_limit_bytes=VMEM_LIMIT),
        name="proj",
    )(x3d, gain, wa, wqb, wkb, wvb, kfeat, vpad, *later_weights)


def _dsa_kernel(q_ref, k_ref, v_ref, bias_ref, o_ref, sa_ref, sb_ref, *, nstage, seg_tiles):
    tq, band = DSA_TQ, DSA_BAND
    first = lax.broadcasted_iota(jnp.int32, (tq, LANES), 1) < HEAD_DIM
    chains = [(u, h) for u in range(DSA_STAGE_TILES) for h in range(2)]

    def rows_of(t):
        return pl.ds(pl.multiple_of(t * tq, tq), tq)

    def seg_start(t):
        return (t % seg_tiles) == 0

    def band_of(t):
        start = jnp.where(seg_start(t), t * tq, t * tq - DSA_BLK)
        return pl.ds(pl.multiple_of(start, DSA_BLK), band)

    def scores(stage, s_ref):
        for u in range(DSA_STAGE_TILES):
            t = stage * DSA_STAGE_TILES + u
            q = q_ref[0, rows_of(t), :]
            kband = k_ref[0, band_of(t), :]
            for h in range(2):
                qm = jnp.where(first if h == 0 else jnp.logical_not(first), q, jnp.zeros_like(q))
                s_ref[2 * u + h] = _dot_nt(qm, kband)

    def finish(stage, s_ref):
        for u in range(DSA_STAGE_TILES):
            t = stage * DSA_STAGE_TILES + u
            variant = jnp.where(seg_start(t), 0, 1)
            vband = v_ref[0, band_of(t), :]
            outs, lses = [], []
            for h in range(2):
                s = s_ref[2 * u + h] + bias_ref[0, h, variant]
                m = jnp.max(_fold_lanes(s, jnp.maximum), axis=-1, keepdims=True)
                p = jnp.exp2(s - m)
                l = jnp.sum(_fold_lanes(p, jnp.add), axis=-1, keepdims=True)
                o = _dot(p.astype(bf16), vband)
                outs.append(o / l)
                lses.append(jnp.broadcast_to(m + jnp.log(l) * LOG2E, (tq, LANES)))
            o_ref[0, rows_of(t), 0:LANES] = jnp.where(first, outs[0], outs[1])
            o_ref[0, rows_of(t), LANES:2 * LANES] = jnp.where(first, lses[0], lses[1])

    zero = jnp.int32(0)
    scores(zero, sa_ref)

    def body(i2, carry):
        st = 2 * i2
        scores(st + 1, sb_ref)
        finish(st, sa_ref)
        scores(st + 2, sa_ref)
        finish(st + 1, sb_ref)
        return carry

    lax.fori_loop(0, nstage // 2 - 1, body, 0)
    st_end = zero + (nstage - 2)
    scores(st_end + 1, sb_ref)
    finish(st_end, sa_ref)
    finish(st_end + 1, sb_ref)


def _dsa_group(a, bias, *, group):
    b, dil, length, _ = a.shape
    s = dil * length
    pairs = DSA_HEADS_PER_GROUP // 2
    seg_tiles = length // DSA_TQ
    nstage = s // (DSA_TQ * DSA_STAGE_TILES)
    assert seg_tiles >= 2 and length % DSA_TQ == 0 and nstage % 2 == 0
    flat = a.reshape(b, s, DSA_GROUP_COLS)
    nchain = 2 * DSA_STAGE_TILES
    col = lambda part: (lambda bi, hp: (bi, 0, part * pairs + hp))
    out = pl.pallas_call(
        functools.partial(_dsa_kernel, nstage=nstage, seg_tiles=seg_tiles),
        out_shape=jax.ShapeDtypeStruct((b, s, 2 * pairs * LANES), f32),
        grid=(b, pairs),
        in_specs=[
            pl.BlockSpec((1, s, LANES), col(0)),
            pl.BlockSpec((1, s, LANES), col(1)),
            pl.BlockSpec((1, s, LANES), col(2)),
            pl.BlockSpec((1, 2, 2, DSA_TQ, DSA_BAND), lambda bi, hp: (hp, 0, 0, 0, 0)),
        ],
        out_specs=pl.BlockSpec((1, s, 2 * LANES), lambda bi, hp: (bi, 0, hp)),
        scratch_shapes=[pltpu.VMEM((nchain, DSA_TQ, DSA_BAND), f32),
                        pltpu.VMEM((nchain, DSA_TQ, DSA_BAND), f32)],
        compiler_params=pltpu.CompilerParams(
            dimension_semantics=("parallel", "parallel"), vmem_limit_bytes=VMEM_LIMIT),
        name=f"dsa_g{group}",
    )(flat, flat, flat, bias)
    return out.reshape(b, dil, length, 2 * pairs * LANES)


def _moba_kernel(slope_ref, qt_ref, kaug_ref, vaug_ref, ksum_ref, arow_ref, cmask_ref, o_ref,
                 rhs_ref, sa_ref, sb_ref, mta_ref, mtb_ref, acc_ref, m_ref, *, nblk):
    hp0 = pl.program_id(1) * MOBA_STEP_PAIRS
    n2 = pl.program_id(2)
    blk = MOBA_BLOCK
    tile = MOBA_TILE
    ntile = n2 + 1
    chains = [(pp, qb, h) for pp in range(MOBA_STEP_PAIRS) for qb in range(2) for h in range(2)]
    slot = {chain: c for c, chain in enumerate(chains)}
    rowi = lax.broadcasted_iota(jnp.int32, (HEAD_PAIR, tile), 0)
    bidx = lax.broadcasted_iota(jnp.int32, (nblk, tile), 0)
    own = 2 * n2 + (lax.broadcasted_iota(jnp.int32, (nblk, tile), 1) >= blk).astype(jnp.int32)
    past = bidx < own

    gates = {}
    for pp in range(MOBA_STEP_PAIRS):
        qpair = qt_ref[0, pp * HEAD_PAIR:(pp + 1) * HEAD_PAIR, :]
        kmean = ksum_ref[0, :, pp * HEAD_PAIR:(pp + 1) * HEAD_PAIR] * (1.0 / blk)
        kmean_hi = kmean.astype(bf16)
        kmean_lo = (kmean - kmean_hi.astype(f32)).astype(bf16)
        for h in range(2):
            mine = (rowi >= HEAD_DIM) if h else (rowi < HEAD_DIM)
            qpad = jnp.where(mine, qpair, jnp.zeros_like(qpair))
            gates[pp, h] = _dot(kmean_hi, qpad) + _dot(kmean_lo, qpad)
            for qb in range(2):
                c = slot[pp, qb, h]
                rhs_ref[c, 0:HEAD_PAIR, :] = qpad[:, qb * blk:(qb + 1) * blk]
                rhs_ref[c, HEAD_PAIR:HEAD_PAIR + nblk, :] = jnp.zeros((nblk, blk), bf16)
                rhs_ref[c, HEAD_PAIR + nblk:HEAD_PAIR + K_FEAT, :] = arow_ref[pp, h]
                acc_ref[c] = jnp.zeros((V_ROWS, blk), f32)
                m_ref[c] = jnp.full((1, blk), M_INIT, f32)

    def key_max(s):
        slabs = s.shape[0] // (tile // MAX_SLABS)
        smax = jnp.max(s.reshape(slabs, tile // MAX_SLABS, blk), axis=0) if slabs > 1 else s
        return jnp.max(smax, axis=0, keepdims=True)

    buf_a, buf_b = (sa_ref, mta_ref), (sb_ref, mtb_ref)

    def scores(i, buf):
        s_ref, mt_ref = buf
        k0 = pl.multiple_of(i * tile, tile)
        for pp in range(MOBA_STEP_PAIRS):
            kt = kaug_ref[0, pp, pl.ds(k0, tile), :]
            for qb in range(2):
                for h in range(2):
                    c = slot[pp, qb, h]
                    s = _dot(kt, rhs_ref[c])
                    s_ref[c] = s
                    mt_ref[c] = key_max(s)

    scores(0, buf_a)

    for (pp, h), gate in gates.items():
        g = jnp.where(past, gate, NEG)
        sel = bidx == own
        for _ in range(MOBA_TOPK):
            mx = jnp.max(g, axis=0, keepdims=True)
            idx = jnp.min(jnp.where(g == mx, bidx, nblk), axis=0, keepdims=True)
            pick = bidx == idx
            sel = sel | (pick & past)
            g = jnp.where(pick, -jnp.inf, g)
        maskbias = jnp.where(sel, 0.0, NEG).astype(bf16)
        for qb in range(2):
            c = slot[pp, qb, h]
            mb = maskbias[:, qb * blk:(qb + 1) * blk]
            rhs_ref[c, HEAD_PAIR:HEAD_PAIR + nblk, :] = mb
            tile_max = []
            for kb in range(tile // blk):
                s = sa_ref[c, kb * blk:(kb + 1) * blk, :] + mb[kb:kb + 1, :].astype(f32)
                sa_ref[c, kb * blk:(kb + 1) * blk, :] = s
                tile_max.append(key_max(s))
            mta_ref[c] = functools.reduce(jnp.maximum, tile_max)

    def update(i, buf, diagonal):
        s_ref, mt_ref = buf
        k0 = pl.multiple_of(i * tile, tile)
        for c, (pp, qb, h) in enumerate(chains):
            if diagonal:
                keys = blk * (qb + 1)
                s = s_ref[c, 0:keys, :] + cmask_ref[qb, 0:keys, :]
                tile_max = key_max(s)
            else:
                keys = tile
                s = s_ref[c]
                tile_max = mt_ref[c]
            rel = tile * (i - n2) - blk * qb
            shift = slope_ref[2 * (hp0 + pp) + h] * rel.astype(f32)
            m_old = m_ref[c]
            m_new = jnp.maximum(m_old, tile_max + shift)
            p = jnp.exp2(s - (m_new - shift)).astype(bf16)
            pv = _dot(vaug_ref[0, 2 * pp + h, :, pl.ds(k0, keys)], p)
            acc_ref[c] = jnp.exp2(m_old - m_new) * acc_ref[c] + pv
            m_ref[c] = m_new

    bufs = (buf_a, buf_b)

    def run(t0, count, last):
        for j in range(count):
            if j + 1 < count or not last:
                scores(t0 + j + 1, bufs[(j + 1) % 2])
            update(t0 + j, bufs[j % 2], last and j == count - 1)

    ntrip = (ntile - 1) // MOBA_UNROLL

    def body(i, carry):
        run(MOBA_UNROLL * i, MOBA_UNROLL, False)
        return carry

    lax.fori_loop(0, ntrip, body, 0)
    t_end = MOBA_UNROLL * ntrip
    for left in range(1, MOBA_UNROLL + 1):
        pl.when(ntile - t_end == left)(functools.partial(run, t_end, left, True))

    for c, (pp, qb, h) in enumerate(chains):
        acc = acc_ref[c]
        row0 = pp * HEAD_PAIR + h * HEAD_DIM
        o_ref[0, row0:row0 + HEAD_DIM, qb * blk:(qb + 1) * blk] = (
            acc[0:HEAD_DIM] / acc[HEAD_DIM:HEAD_DIM + 1]).astype(bf16)


def _moba(slopes, qt, kaug, vaug, ksum, arows, cmask):
    b, _, s = qt.shape
    nblk = s // MOBA_BLOCK
    pairs = MOBA_HEADS // 2
    assert nblk + arows.shape[2] == K_FEAT and nblk % 2 == 0 and pairs % MOBA_STEP_PAIRS == 0
    npp = MOBA_STEP_PAIRS
    nchain = 4 * npp
    return pl.pallas_call(
        functools.partial(_moba_kernel, nblk=nblk),
        out_shape=jax.ShapeDtypeStruct((b, MOBA_WIDTH, s), bf16),
        grid=(b, pairs // npp, nblk // 2),
        in_specs=[
            pl.BlockSpec(memory_space=pltpu.SMEM),
            pl.BlockSpec((1, npp * HEAD_PAIR, MOBA_TILE), lambda bi, hp, n2: (bi, hp, n2)),
            pl.BlockSpec((1, npp, s, HEAD_PAIR + K_FEAT), lambda bi, hp, n2: (bi, hp, 0, 0)),
            pl.BlockSpec((1, 2 * npp, V_ROWS, s), lambda bi, hp, n2: (bi, hp, 0, 0)),
            pl.BlockSpec((1, nblk, npp * HEAD_PAIR), lambda bi, hp, n2: (bi, 0, hp)),
            pl.BlockSpec((npp, 2, K_FEAT - nblk, MOBA_BLOCK), lambda bi, hp, n2: (hp, 0, 0, 0)),
            _resident((2, MOBA_TILE, MOBA_BLOCK)),
        ],
        out_specs=pl.BlockSpec((1, npp * HEAD_PAIR, MOBA_TILE), lambda bi, hp, n2: (bi, hp, n2)),
        scratch_shapes=[
            pltpu.VMEM((nchain, HEAD_PAIR + K_FEAT, MOBA_BLOCK), bf16),
            pltpu.VMEM((nchain, MOBA_TILE, MOBA_BLOCK), f32),
            pltpu.VMEM((nchain, MOBA_TILE, MOBA_BLOCK), f32),
            pltpu.VMEM((nchain, 1, MOBA_BLOCK), f32),
            pltpu.VMEM((nchain, 1, MOBA_BLOCK), f32),
            pltpu.VMEM((nchain, V_ROWS, MOBA_BLOCK), f32),
            pltpu.VMEM((nchain, 1, MOBA_BLOCK), f32),
        ],
        compiler_params=pltpu.CompilerParams(
            dimension_semantics=("parallel", "parallel", "arbitrary"), vmem_limit_bytes=VMEM_LIMIT),
        name="moba",
    )(slopes, qt, kaug, vaug, ksum.reshape(b, nblk, MOBA_WIDTH), arows, cmask)


def _merge_kernel(x_ref, g_ref, d0_ref, d1_ref, d2_ref, ybt_ref, wgate_ref, wua_ref, wub_ref, wo_ref, o_ref,
                  nat_ref, *, tm):
    rows = GROUP_ROWS

    def token_order(g, c, part):
        d_ref, dil = (d0_ref, d1_ref, d2_ref)[g], DSA_GROUPS[g][1]
        per = rows // dil
        lanes = slice(c * LANES, (c + 1) * LANES)
        if dil == 1:
            return d_ref[0, 0, part * rows:(part + 1) * rows, lanes]
        for r in range(dil):
            nat_ref[g, c, pl.ds(part * rows + r, per, stride=dil), :] = (
                d_ref[0, r, part * per:(part + 1) * per, lanes])
        return nat_ref[g, c, part * rows:(part + 1) * rows, :]

    for part in range(tm // rows):
        sl = slice(part * rows, (part + 1) * rows)
        pairs = []
        for hp in range(2):
            outs = [token_order(g, 2 * hp, part) for g in range(len(DSA_GROUPS))]
            lses = [token_order(g, 2 * hp + 1, part) for g in range(len(DSA_GROUPS))]
            mx = jnp.maximum(jnp.maximum(lses[0], lses[1]), lses[2])
            es = [jnp.exp2(l - mx) for l in lses]
            den = es[0] + es[1] + es[2]
            pairs.append((es[0] * outs[0] + es[1] * outs[1] + es[2] * outs[2]) / den)
        ya = jnp.concatenate(pairs, axis=1).astype(bf16)
        x = x_ref[0, sl, :]
        h = _rmsnorm(x, g_ref[...]).astype(bf16)
        gates = jax.nn.sigmoid(_dot(h, wgate_ref[...]))
        ta = _dot(ya, wua_ref[...])
        tb = _dot_tn(ybt_ref[0, :, sl], wub_ref[...])
        merged = gates[:, 0:D_MODEL] * ta + gates[:, D_MODEL:] * tb
        o_ref[0, sl, :] = x + _dot(merged.astype(bf16), wo_ref[...])


def _merge(x3d, gain, dsa_outs, ybt, wgate, wua, wub, wo, *, tm):
    b, s, _ = x3d.shape
    tok = lambda w: pl.BlockSpec((1, tm, w), lambda bi, i: (bi, i, 0))
    grp = lambda dil: pl.BlockSpec((1, dil, tm // dil, 4 * LANES), lambda bi, i: (bi, 0, i, 0))
    return pl.pallas_call(
        functools.partial(_merge_kernel, tm=tm),
        out_shape=jax.ShapeDtypeStruct((b, s, D_MODEL), f32),
        grid=(b, s // tm),
        in_specs=[
            tok(D_MODEL), _resident((1, D_MODEL)), *[grp(dil) for _, dil in DSA_GROUPS],
            pl.BlockSpec((1, MOBA_WIDTH, tm), lambda bi, i: (bi, 0, i)),
            _resident((D_MODEL, 2 * D_MODEL)),
            _resident((DSA_OUT, D_MODEL)),
            _resident((MOBA_WIDTH, D_MODEL)),
            _resident((D_MODEL, D_MODEL)),
        ],
        out_specs=tok(D_MODEL),
        scratch_shapes=[pltpu.VMEM((len(DSA_GROUPS), 4, tm, LANES), f32)],
        compiler_params=pltpu.CompilerParams(
            dimension_semantics=("parallel", "parallel"), vmem_limit_bytes=VMEM_LIMIT),
        name="merge",
    )(x3d, gain, *dsa_outs, ybt, wgate, wua, wub, wo)


def _alibi_slopes(n):
    return 2.0 ** (-8.0 * np.arange(1, n + 1, dtype=np.float64) / n)


def _split3(x):
    x = np.asarray(x, np.float32)
    hi = x.astype(bf16)
    r1 = x - hi.astype(np.float32)
    mid = r1.astype(bf16)
    lo = (r1 - mid.astype(np.float32)).astype(bf16)
    return hi, mid, lo


def _moba_constants(s):
    nblk = s // MOBA_BLOCK
    n_arow = K_FEAT - nblk
    pos = np.arange(s)
    kfeat = np.zeros((s, K_FEAT), np.float32)
    kfeat[pos, pos // MOBA_BLOCK] = 1.0
    kfeat[:, nblk:nblk + 3] = 1.0
    kfeat[:, nblk + 3:nblk + 6] = (pos % MOBA_BLOCK)[:, None]
    kfeat[:, nblk + 6:nblk + 9] = ((pos // MOBA_BLOCK) % 2)[:, None]
    slopes = (_alibi_slopes(MOBA_HEADS) * LOG2E).astype(np.float32)
    off = np.arange(MOBA_BLOCK, dtype=np.float32)
    wide = lambda v: np.broadcast_to(v[:, None], (MOBA_HEADS, MOBA_BLOCK))
    qterm = _split3(-slopes[:, None] * off[None, :])
    sterm = _split3(wide(slopes))
    bterm = _split3(wide(slopes * np.float32(MOBA_BLOCK)))
    rows = np.zeros((MOBA_HEADS, n_arow, MOBA_BLOCK), bf16)
    rows[:, 0:9] = np.stack(list(qterm) + list(sterm) + list(bterm), axis=1)
    arows = rows.reshape(MOBA_HEADS // 2, 2, n_arow, MOBA_BLOCK)
    r = np.arange(MOBA_TILE)[None, :, None]
    c = np.arange(MOBA_BLOCK)[None, None, :] + MOBA_BLOCK * np.arange(2)[:, None, None]
    cmask = np.where(r <= c, 0.0, NEG).astype(np.float32)
    return jnp.asarray(kfeat.astype(bf16)), jnp.asarray(arows), jnp.asarray(slopes), jnp.asarray(cmask)


def _dsa_bias(group, dilation):
    slopes = _alibi_slopes(DSA_HEADS)[group * DSA_HEADS_PER_GROUP:(group + 1) * DSA_HEADS_PER_GROUP]
    slopes = (slopes * LOG2E * dilation).astype(np.float32).reshape(2, 2, 1, 1, 1)
    qi = np.arange(DSA_TQ)[:, None]
    kj = np.arange(DSA_BAND)[None, :]
    delta = np.stack([qi - kj, qi + DSA_BLK - kj])[None, None]
    valid = (delta >= 0) & (delta <= DSA_BLK)
    return jnp.asarray(np.where(valid, -slopes * delta.astype(np.float32), NEG).astype(np.float32))


def _layer(x, norm_ffn1, ffn1_gate, ffn1_up, ffn1_down, norm_mix, w_in, w_up_a, w_up_b, w_out,
           norm_ffn2, ffn2_gate, ffn2_up, ffn2_down, gain_final, *, final_norm):
    b, s, _ = x.shape
    tm = 512
    tm_merge = 1024
    tm_ffn = 1024
    row = lambda g: g.reshape(1, D_MODEL).astype(f32)
    qscale = HEAD_DIM ** -0.5 * LOG2E
    qa_scale = np.ones((1, 3 * DSA_WIDTH), np.float32)
    qa_scale[:, :DSA_WIDTH] = qscale
    qa_scale = jnp.asarray(qa_scale)
    x1, wa, wqb, wkb, wvb, wgate = _ffn(
        x.reshape(b * s, D_MODEL), row(norm_ffn1), ffn1_gate.astype(bf16), ffn1_up.astype(bf16),
        ffn1_down.astype(bf16), row(gain_final), final_norm=False, tm=tm_ffn, w_in=w_in,
        qa_scale=qa_scale, qb_scale=float(qscale))
    x1 = x1.reshape(b, s, D_MODEL)

    kfeat, arows, slopes_b, cmask = _moba_constants(s)
    vpad = np.zeros((V_ROWS - HEAD_DIM, tm), np.float32)
    vpad[0] = 1.0
    vpad = jnp.asarray(vpad.astype(bf16))
    a0, a1, a2, kaug, qt, vaug, ksum, w2_gate, w2_up, w2_down, wua, wub, wo = _proj(
        x1, row(norm_mix), wa, wqb, wkb, wvb, kfeat, vpad,
        (ffn2_gate, ffn2_up, ffn2_down, w_up_a, w_up_b, w_out), tm=tm)

    dsa_outs = [
        _dsa_group(a, _dsa_bias(gi, dil), group=gi)
        for gi, (a, (_, dil)) in enumerate(zip((a0, a1, a2), DSA_GROUPS))
    ]
    ybt = _moba(slopes_b, qt, kaug, vaug, ksum, arows, cmask)
    x2 = _merge(x1, row(norm_mix), dsa_outs, ybt, wgate, wua, wub, wo, tm=tm_merge)
    x3 = _ffn(x2.reshape(b * s, D_MODEL), row(norm_ffn2), w2_gate, w2_up, w2_down, row(gain_final),
              final_norm=final_norm, tm=tm_ffn)
    return x3.reshape(b, s, D_MODEL)


def kernel(x, norm_ffn1, ffn1_gate, ffn1_up, ffn1_down, norm_mix, w_in, w_up_a, w_up_b, w_out,
           norm_ffn2, ffn2_gate, ffn2_up, ffn2_down, norm_final):
    depth = norm_ffn1.shape[0]
    for layer in range(depth):
        last = layer == depth - 1
        x = _layer(x, norm_ffn1[layer], ffn1_gate[layer], ffn1_up[layer], ffn1_down[layer],
                   norm_mix[layer], w_in[layer], w_up_a[layer], w_up_b[layer], w_out[layer],
                   norm_ffn2[layer], ffn2_gate[layer], ffn2_up[layer], ffn2_down[layer],
                   norm_final, final_norm=last)
    return x
```

```python
import functools

import numpy as np
import jax
import jax.numpy as jnp
from jax import lax
from jax.experimental import pallas as pl
from jax.experimental.pallas import tpu as pltpu

D_MODEL = 1024
HEAD_DIM = 64
DSA_GROUPS = ((128, 1), (512, 4), (2048, 16))
DSA_HEADS_PER_GROUP = 4
DSA_HEADS = DSA_HEADS_PER_GROUP * len(DSA_GROUPS)
MOBA_HEADS = 8
MOBA_BLOCK = 256
MOBA_TOPK = 3
D_FF = ((8 * D_MODEL // 3 + 127) // 128) * 128
DSA_WIDTH = DSA_HEADS * HEAD_DIM
MOBA_WIDTH = MOBA_HEADS * HEAD_DIM
DSA_OUT = DSA_HEADS_PER_GROUP * HEAD_DIM
EPS = 1e-6
NEG = -1e30

LANES = 128
HEAD_PAIR = 2 * HEAD_DIM
DSA_BLK = 128
DSA_TQ = DSA_BLK
DSA_BAND = DSA_TQ + DSA_BLK
DSA_STAGE_TILES = 2
DSA_GROUP_COLS = 3 * DSA_OUT
MOBA_TILE = 2 * MOBA_BLOCK
MAX_SLABS = 8
MOBA_STEP_PAIRS = 2
MOBA_UNROLL = 4
LOG2E = 1.4426950408889634
M_INIT = -3.0e38
V_ROWS = 80
K_FEAT = 128
VMEM_LIMIT = 56 * 1024 * 1024
GROUP_ROWS = 256
BF16_ROWS = 16

f32 = jnp.float32
bf16 = jnp.bfloat16


def _rmsnorm(x, g):
    return x * lax.rsqrt(jnp.mean(x * x, axis=-1, keepdims=True) + EPS) * g


def _dot(a, b):
    return jnp.dot(a, b, preferred_element_type=f32)


def _dot_nt(a, b):
    return lax.dot_general(a, b, (((1,), (1,)), ((), ())), preferred_element_type=f32)


def _dot_tn(a, b):
    return lax.dot_general(a, b, (((0,), (0,)), ((), ())), preferred_element_type=f32)


def _fold_lanes(x, op):
    tiles = [x[:, c * LANES:(c + 1) * LANES] for c in range(x.shape[-1] // LANES)]
    return functools.reduce(op, tiles)


def _resident(shape):
    nd = len(shape)
    return pl.BlockSpec(shape, lambda *_: (0,) * nd, pipeline_mode=pl.Buffered(1))


def _cast_rows(side_in, side_out, plan):
    for o_ref, (src, lo, hi, scale) in zip(side_out, plan):
        w = side_in[src][:, lo:hi]
        if isinstance(scale, int):
            w = w * side_in[scale][...]
        elif scale is not None:
            w = w * scale
        o_ref[...] = w.astype(bf16)


def _ffn_kernel(x_ref, g_ref, wg_ref, wu_ref, wd_ref, gf_ref, *rest, final_norm, n_side, plan):
    side_in, (o_ref, *side_out) = rest[:n_side], rest[n_side:]
    _cast_rows(side_in, side_out, plan)
    rows = GROUP_ROWS
    for part in range(x_ref.shape[0] // rows):
        sl = slice(part * rows, (part + 1) * rows)
        x = x_ref[sl, :]
        h = _rmsnorm(x, g_ref[...]).astype(bf16)
        gate = _dot(h, wg_ref[...])
        up = _dot(h, wu_ref[...])
        act = (gate * jax.nn.sigmoid(gate) * up).astype(bf16)
        y = x + 0.5 * _dot(act, wd_ref[...])
        if final_norm:
            y = _rmsnorm(y, gf_ref[...])
        o_ref[sl, :] = y


def _ffn(x2d, gain, w_gate, w_up, w_down, gain_final, *, final_norm, tm, w_in=None, qa_scale=None,
         qb_scale=None):
    n = x2d.shape[0]
    steps = n // tm
    side_args, side_specs, side_shapes, side_out_specs, plan = [], [], [], [], ()
    if w_in is not None:
        rows = w_in.shape[0] // steps
        assert rows * steps == w_in.shape[0] and rows % BF16_ROWS == 0
        block = lambda width: pl.BlockSpec((rows, width), lambda i: (i, 0))
        side_args = [w_in, qa_scale]
        side_specs = [block(w_in.shape[1]), _resident(qa_scale.shape)]
        qb0 = 3 * DSA_WIDTH
        plan = ((0, 0, qb0, 1),
                (0, qb0, qb0 + MOBA_WIDTH, qb_scale),
                (0, qb0 + MOBA_WIDTH, qb0 + 2 * MOBA_WIDTH, None),
                (0, qb0 + 2 * MOBA_WIDTH, qb0 + 3 * MOBA_WIDTH, None),
                (0, qb0 + 3 * MOBA_WIDTH, w_in.shape[1], None))
        for _, lo, hi, _ in plan:
            side_shapes.append(jax.ShapeDtypeStruct((w_in.shape[0], hi - lo), bf16))
            side_out_specs.append(block(hi - lo))
    out = pl.pallas_call(
        functools.partial(_ffn_kernel, final_norm=final_norm, n_side=len(side_args), plan=plan),
        out_shape=(jax.ShapeDtypeStruct((n, D_MODEL), f32), *side_shapes),
        grid=(steps,),
        in_specs=[
            pl.BlockSpec((tm, D_MODEL), lambda i: (i, 0)),
            _resident((1, D_MODEL)),
            _resident((D_MODEL, D_FF)),
            _resident((D_MODEL, D_FF)),
            _resident((D_FF, D_MODEL)),
            _resident((1, D_MODEL)),
            *side_specs,
        ],
        out_specs=(pl.BlockSpec((tm, D_MODEL), lambda i: (i, 0)), *side_out_specs),
        compiler_params=pltpu.CompilerParams(
            dimension_semantics=("parallel",), vmem_limit_bytes=VMEM_LIMIT),
        name="ffn_final" if final_norm else "ffn",
    )(x2d, gain, w_gate, w_up, w_down, gain_final, *side_args)
    return out if side_args else out[0]


def _proj_kernel(x_ref, g_ref, wa_ref, wqb_ref, wkb_ref, wvb_ref, kfeat_ref, vpad_ref, *rest,
                 tm, n_side, plan):
    side_in, rest = rest[:n_side], rest[n_side:]
    a0_ref, a1_ref, a2_ref, kaug_ref, qt_ref, vaug_ref, ksum_ref = rest[:7]
    side_out, scr_ref = rest[7:-1], rest[-1]

    @pl.when(pl.program_id(0) == 0)
    def _():
        _cast_rows(side_in, side_out, plan)

    rows = GROUP_ROWS
    tiles_per_group = DSA_GROUP_COLS // LANES
    for part in range(tm // rows):
        sl = slice(part * rows, (part + 1) * rows)
        h = _rmsnorm(x_ref[0, sl, :], g_ref[...]).astype(bf16)
        qkv = _dot(h, wa_ref[...])
        for g, (a_ref, (_, dil)) in enumerate(zip((a0_ref, a1_ref, a2_ref), DSA_GROUPS)):
            per = rows // dil
            for c in range(tiles_per_group):
                lo = (c // 2) * DSA_WIDTH + g * DSA_OUT + (c % 2) * LANES
                lanes = slice(c * LANES, (c + 1) * LANES)
                if dil == 1:
                    a_ref[0, 0, sl, lanes] = qkv[:, lo:lo + LANES].astype(bf16)
                    continue
                scr_ref[part, g, c] = qkv[:, lo:lo + LANES]
                for r in range(dil):
                    a_ref[0, r, part * per:(part + 1) * per, lanes] = (
                        scr_ref[part, g, c, pl.ds(r, per, stride=dil), :].astype(bf16))
        kb = _dot(h, wkb_ref[...])
        nb = rows // MOBA_BLOCK
        ksum_ref[part * nb:(part + 1) * nb] = (
            jnp.sum(kb.reshape(nb, MOBA_BLOCK, MOBA_WIDTH), axis=1)[:, None, :])
        kb = kb.astype(bf16)
        for p in range(MOBA_HEADS // 2):
            kaug_ref[0, p, sl, 0:HEAD_PAIR] = kb[:, p * HEAD_PAIR:(p + 1) * HEAD_PAIR]
            kaug_ref[0, p, sl, HEAD_PAIR:HEAD_PAIR + K_FEAT] = kfeat_ref[sl, :]
        qt_ref[0, :, sl] = _dot(h, wqb_ref[...]).T.astype(bf16)
        vt = _dot(h, wvb_ref[...]).T
        for hd in range(MOBA_HEADS):
            vaug_ref[0, hd, 0:HEAD_DIM, sl] = vt[hd * HEAD_DIM:(hd + 1) * HEAD_DIM].astype(bf16)
            vaug_ref[0, hd, HEAD_DIM:V_ROWS, sl] = vpad_ref[:, sl]


def _proj(x3d, gain, wa, wqb, wkb, wvb, kfeat, vpad, later_weights, *, tm):
    b, s, _ = x3d.shape
    nblk = s // MOBA_BLOCK
    steps = s // tm
    side_specs, side_shapes, side_out_specs, plan = [], [], [], []
    parked = lambda bi, i: (jnp.where(bi == 0, i, steps - 1), 0)
    for k, w in enumerate(later_weights):
        rows = w.shape[0] // steps
        assert rows * steps == w.shape[0] and rows % BF16_ROWS == 0
        spec = pl.BlockSpec((rows, w.shape[1]), parked)
        side_specs.append(spec)
        side_out_specs.append(spec)
        side_shapes.append(jax.ShapeDtypeStruct(w.shape, bf16))
        plan.append((k, 0, w.shape[1], None))
    grp_shapes = tuple(
        jax.ShapeDtypeStruct((b, dil, s // dil, DSA_GROUP_COLS), bf16) for _, dil in DSA_GROUPS)
    grp_specs = tuple(
        pl.BlockSpec((1, dil, tm // dil, DSA_GROUP_COLS), lambda bi, i: (bi, 0, i, 0))
        for _, dil in DSA_GROUPS)
    out_shape = grp_shapes + (
        jax.ShapeDtypeStruct((b, MOBA_HEADS // 2, s, HEAD_PAIR + K_FEAT), bf16),
        jax.ShapeDtypeStruct((b, MOBA_WIDTH, s), bf16),
        jax.ShapeDtypeStruct((b, MOBA_HEADS, V_ROWS, s), bf16),
        jax.ShapeDtypeStruct((b * nblk, 1, MOBA_WIDTH), f32),
    )
    per = tm // MOBA_BLOCK
    return pl.pallas_call(
        functools.partial(_proj_kernel, tm=tm, n_side=len(later_weights), plan=tuple(plan)),
        out_shape=out_shape + tuple(side_shapes),
        grid=(b, steps),
        in_specs=[
            pl.BlockSpec((1, tm, D_MODEL), lambda bi, i: (bi, i, 0)),
            _resident((1, D_MODEL)),
            _resident((D_MODEL, 3 * DSA_WIDTH)),
            _resident((D_MODEL, MOBA_WIDTH)),
            _resident((D_MODEL, MOBA_WIDTH)),
            _resident((D_MODEL, MOBA_WIDTH)),
            pl.BlockSpec((tm, K_FEAT), lambda bi, i: (i, 0)),
            _resident((V_ROWS - HEAD_DIM, tm)),
            *side_specs,
        ],
        out_specs=grp_specs + (
            pl.BlockSpec((1, MOBA_HEADS // 2, tm, HEAD_PAIR + K_FEAT), lambda bi, i: (bi, 0, i, 0)),
            pl.BlockSpec((1, MOBA_WIDTH, tm), lambda bi, i: (bi, 0, i)),
            pl.BlockSpec((1, MOBA_HEADS, V_ROWS, tm), lambda bi, i: (bi, 0, 0, i)),
            pl.BlockSpec((per, 1, MOBA_WIDTH), lambda bi, i: (bi * (s // tm) + i, 0, 0)),
        ) + tuple(side_out_specs),
        scratch_shapes=[pltpu.VMEM(
            (tm // GROUP_ROWS, len(DSA_GROUPS), DSA_GROUP_COLS // LANES, GROUP_ROWS, LANES), f32)],
        compiler_params=pltpu.CompilerParams(
            dimension_semantics=("arbitrary", "arbitrary"), vmem_limit_bytes=VMEM_LIMIT),
        name="proj",
    )(x3d, gain, wa, wqb, wkb, wvb, kfeat, vpad, *later_weights)


def _dsa_kernel(q_ref, k_ref, v_ref, bias_ref, o_ref, sa_ref, sb_ref, *, nstage, seg_tiles):
    tq, band = DSA_TQ, DSA_BAND
    first = lax.broadcasted_iota(jnp.int32, (tq, LANES), 1) < HEAD_DIM
    chains = [(u, h) for u in range(DSA_STAGE_TILES) for h in range(2)]

    def rows_of(t):
        return pl.ds(pl.multiple_of(t * tq, tq), tq)

    def seg_start(t):
        return (t % seg_tiles) == 0

    def band_of(t):
        start = jnp.where(seg_start(t), t * tq, t * tq - DSA_BLK)
        return pl.ds(pl.multiple_of(start, DSA_BLK), band)

    def scores(stage, s_ref):
        for u in range(DSA_STAGE_TILES):
            t = stage * DSA_STAGE_TILES + u
            q = q_ref[0, rows_of(t), :]
            kband = k_ref[0, band_of(t), :]
            for h in range(2):
                qm = jnp.where(first if h == 0 else jnp.logical_not(first), q, jnp.zeros_like(q))
                s_ref[2 * u + h] = _dot_nt(qm, kband)

    def finish(stage, s_ref):
        for u in range(DSA_STAGE_TILES):
            t = stage * DSA_STAGE_TILES + u
            variant = jnp.where(seg_start(t), 0, 1)
            vband = v_ref[0, band_of(t), :]
            os, ms, ls = [], [], []
            for h in range(2):
                s = s_ref[2 * u + h] + bias_ref[0, h, variant]
                m = jnp.max(_fold_lanes(s, jnp.maximum), axis=-1, keepdims=True)
                p = jnp.exp2(s - m)
                l = jnp.sum(_fold_lanes(p, jnp.add), axis=-1, keepdims=True)
                os.append(_dot(p.astype(bf16), vband))
                ms.append(jnp.broadcast_to(m, (tq, LANES)))
                ls.append(jnp.broadcast_to(l, (tq, LANES)))
            o, m, l = (jnp.where(first, a, b) for a, b in (os, ms, ls))
            o_ref[0, rows_of(t), 0:LANES] = o / l
            o_ref[0, rows_of(t), LANES:2 * LANES] = m + jnp.log(l) * LOG2E

    zero = jnp.int32(0)
    scores(zero, sa_ref)

    def body(i2, carry):
        st = 2 * i2
        scores(st + 1, sb_ref)
        finish(st, sa_ref)
        scores(st + 2, sa_ref)
        finish(st + 1, sb_ref)
        return carry

    lax.fori_loop(0, nstage // 2 - 1, body, 0)
    st_end = zero + (nstage - 2)
    scores(st_end + 1, sb_ref)
    finish(st_end, sa_ref)
    finish(st_end + 1, sb_ref)


def _dsa_group(a, bias, *, group):
    b, dil, length, _ = a.shape
    s = dil * length
    pairs = DSA_HEADS_PER_GROUP // 2
    seg_tiles = length // DSA_TQ
    nstage = s // (DSA_TQ * DSA_STAGE_TILES)
    assert seg_tiles >= 2 and length % DSA_TQ == 0 and nstage % 2 == 0
    flat = a.reshape(b, s, DSA_GROUP_COLS)
    nchain = 2 * DSA_STAGE_TILES
    col = lambda part: (lambda bi, hp: (bi, 0, part * pairs + hp))
    out = pl.pallas_call(
        functools.partial(_dsa_kernel, nstage=nstage, seg_tiles=seg_tiles),
        out_shape=jax.ShapeDtypeStruct((b, s, 2 * pairs * LANES), f32),
        grid=(b, pairs),
        in_specs=[
            pl.BlockSpec((1, s, LANES), col(0)),
            pl.BlockSpec((1, s, LANES), col(1)),
            pl.BlockSpec((1, s, LANES), col(2)),
            pl.BlockSpec((1, 2, 2, DSA_TQ, DSA_BAND), lambda bi, hp: (hp, 0, 0, 0, 0)),
        ],
        out_specs=pl.BlockSpec((1, s, 2 * LANES), lambda bi, hp: (bi, 0, hp)),
        scratch_shapes=[pltpu.VMEM((nchain, DSA_TQ, DSA_BAND), f32),
                        pltpu.VMEM((nchain, DSA_TQ, DSA_BAND), f32)],
        compiler_params=pltpu.CompilerParams(
            dimension_semantics=("parallel", "parallel"), vmem_limit_bytes=VMEM_LIMIT),
        name=f"dsa_g{group}",
    )(flat, flat, flat, bias)
    return out.reshape(b, dil, length, 2 * pairs * LANES)


def _moba_kernel(slope_ref, qt_ref, kaug_ref, vaug_ref, ksum_ref, arow_ref, cmask_ref, o_ref,
                 rhs_ref, sa_ref, sb_ref, mta_ref, mtb_ref, acc_ref, m_ref, *, nblk):
    hp0 = pl.program_id(1) * MOBA_STEP_PAIRS
    n2 = pl.program_id(2)
    blk = MOBA_BLOCK
    tile = MOBA_TILE
    ntile = n2 + 1
    chains = [(pp, qb, h) for pp in range(MOBA_STEP_PAIRS) for qb in range(2) for h in range(2)]
    slot = {chain: c for c, chain in enumerate(chains)}
    rowi = lax.broadcasted_iota(jnp.int32, (HEAD_PAIR, tile), 0)
    bidx = lax.broadcasted_iota(jnp.int32, (nblk, tile), 0)
    own = 2 * n2 + (lax.broadcasted_iota(jnp.int32, (nblk, tile), 1) >= blk).astype(jnp.int32)
    past = bidx < own

    gates = {}
    for pp in range(MOBA_STEP_PAIRS):
        qpair = qt_ref[0, pp * HEAD_PAIR:(pp + 1) * HEAD_PAIR, :]
        kmean = ksum_ref[0, :, pp * HEAD_PAIR:(pp + 1) * HEAD_PAIR] * (1.0 / blk)
        kmean_hi = kmean.astype(bf16)
        kmean_lo = (kmean - kmean_hi.astype(f32)).astype(bf16)
        for h in range(2):
            mine = (rowi >= HEAD_DIM) if h else (rowi < HEAD_DIM)
            qpad = jnp.where(mine, qpair, jnp.zeros_like(qpair))
            gates[pp, h] = _dot(kmean_hi, qpad) + _dot(kmean_lo, qpad)
            for qb in range(2):
                c = slot[pp, qb, h]
                rhs_ref[c, 0:HEAD_PAIR, :] = qpad[:, qb * blk:(qb + 1) * blk]
                rhs_ref[c, HEAD_PAIR:HEAD_PAIR + nblk, :] = jnp.zeros((nblk, blk), bf16)
                rhs_ref[c, HEAD_PAIR + nblk:HEAD_PAIR + K_FEAT, :] = arow_ref[pp, h]
                acc_ref[c] = jnp.zeros((V_ROWS, blk), f32)
                m_ref[c] = jnp.full((1, blk), M_INIT, f32)

    def key_max(s):
        slabs = s.shape[0] // (tile // MAX_SLABS)
        smax = jnp.max(s.reshape(slabs, tile // MAX_SLABS, blk), axis=0) if slabs > 1 else s
        return jnp.max(smax, axis=0, keepdims=True)

    buf_a, buf_b = (sa_ref, mta_ref), (sb_ref, mtb_ref)

    def scores(i, buf):
        s_ref, mt_ref = buf
        k0 = pl.multiple_of(i * tile, tile)
        for pp in range(MOBA_STEP_PAIRS):
            kt = kaug_ref[0, pp, pl.ds(k0, tile), :]
            for qb in range(2):
                for h in range(2):
                    c = slot[pp, qb, h]
                    s = _dot(kt, rhs_ref[c])
                    s_ref[c] = s
                    mt_ref[c] = key_max(s)

    scores(0, buf_a)

    for (pp, h), gate in gates.items():
        g = jnp.where(past, gate, NEG)
        sel = bidx == own
        for _ in range(MOBA_TOPK):
            mx = jnp.max(g, axis=0, keepdims=True)
            idx = jnp.min(jnp.where(g == mx, bidx, nblk), axis=0, keepdims=True)
            pick = bidx == idx
            sel = sel | (pick & past)
            g = jnp.where(pick, -jnp.inf, g)
        maskbias = jnp.where(sel, 0.0, NEG).astype(bf16)
        for qb in range(2):
            c = slot[pp, qb, h]
            mb = maskbias[:, qb * blk:(qb + 1) * blk]
            rhs_ref[c, HEAD_PAIR:HEAD_PAIR + nblk, :] = mb
            tile_max = []
            for kb in range(tile // blk):
                s = sa_ref[c, kb * blk:(kb + 1) * blk, :] + mb[kb:kb + 1, :].astype(f32)
                sa_ref[c, kb * blk:(kb + 1) * blk, :] = s
                tile_max.append(key_max(s))
            mta_ref[c] = functools.reduce(jnp.maximum, tile_max)

    def update(i, buf, diagonal):
        s_ref, mt_ref = buf
        k0 = pl.multiple_of(i * tile, tile)
        for c, (pp, qb, h) in enumerate(chains):
            if diagonal:
                keys = blk * (qb + 1)
                s = s_ref[c, 0:keys, :] + cmask_ref[qb, 0:keys, :]
                tile_max = key_max(s)
            else:
                keys = tile
                s = s_ref[c]
                tile_max = mt_ref[c]
            rel = tile * (i - n2) - blk * qb
            shift = slope_ref[2 * (hp0 + pp) + h] * rel.astype(f32)
            m_old = m_ref[c]
            m_new = jnp.maximum(m_old, tile_max + shift)
            p = jnp.exp2(s - (m_new - shift)).astype(bf16)
            pv = _dot(vaug_ref[0, 2 * pp + h, :, pl.ds(k0, keys)], p)
            acc_ref[c] = jnp.exp2(m_old - m_new) * acc_ref[c] + pv
            m_ref[c] = m_new

    bufs = (buf_a, buf_b)

    def run(t0, count, last):
        for j in range(count):
            if j + 1 < count or not last:
                scores(t0 + j + 1, bufs[(j + 1) % 2])
            update(t0 + j, bufs[j % 2], last and j == count - 1)

    ntrip = (ntile - 1) // MOBA_UNROLL

    def body(i, carry):
        run(MOBA_UNROLL * i, MOBA_UNROLL, False)
        return carry

    lax.fori_loop(0, ntrip, body, 0)
    t_end = MOBA_UNROLL * ntrip
    for left in range(1, MOBA_UNROLL + 1):
        pl.when(ntile - t_end == left)(functools.partial(run, t_end, left, True))

    for c, (pp, qb, h) in enumerate(chains):
        acc = acc_ref[c]
        row0 = pp * HEAD_PAIR + h * HEAD_DIM
        o_ref[0, row0:row0 + HEAD_DIM, qb * blk:(qb + 1) * blk] = (
            acc[0:HEAD_DIM] / acc[HEAD_DIM:HEAD_DIM + 1]).astype(bf16)


def _moba(slopes, qt, kaug, vaug, ksum, arows, cmask):
    b, _, s = qt.shape
    nblk = s // MOBA_BLOCK
    pairs = MOBA_HEADS // 2
    assert nblk + arows.shape[2] == K_FEAT and nblk % 2 == 0 and pairs % MOBA_STEP_PAIRS == 0
    npp = MOBA_STEP_PAIRS
    nchain = 4 * npp
    return pl.pallas_call(
        functools.partial(_moba_kernel, nblk=nblk),
        out_shape=jax.ShapeDtypeStruct((b, MOBA_WIDTH, s), bf16),
        grid=(b, pairs // npp, nblk // 2),
        in_specs=[
            pl.BlockSpec(memory_space=pltpu.SMEM),
            pl.BlockSpec((1, npp * HEAD_PAIR, MOBA_TILE), lambda bi, hp, n2: (bi, hp, n2)),
            pl.BlockSpec((1, npp, s, HEAD_PAIR + K_FEAT), lambda bi, hp, n2: (bi, hp, 0, 0)),
            pl.BlockSpec((1, 2 * npp, V_ROWS, s), lambda bi, hp, n2: (bi, hp, 0, 0)),
            pl.BlockSpec((1, nblk, npp * HEAD_PAIR), lambda bi, hp, n2: (bi, 0, hp)),
            pl.BlockSpec((npp, 2, K_FEAT - nblk, MOBA_BLOCK), lambda bi, hp, n2: (hp, 0, 0, 0)),
            _resident((2, MOBA_TILE, MOBA_BLOCK)),
        ],
        out_specs=pl.BlockSpec((1, npp * HEAD_PAIR, MOBA_TILE), lambda bi, hp, n2: (bi, hp, n2)),
        scratch_shapes=[
            pltpu.VMEM((nchain, HEAD_PAIR + K_FEAT, MOBA_BLOCK), bf16),
            pltpu.VMEM((nchain, MOBA_TILE, MOBA_BLOCK), f32),
            pltpu.VMEM((nchain, MOBA_TILE, MOBA_BLOCK), f32),
            pltpu.VMEM((nchain, 1, MOBA_BLOCK), f32),
            pltpu.VMEM((nchain, 1, MOBA_BLOCK), f32),
            pltpu.VMEM((nchain, V_ROWS, MOBA_BLOCK), f32),
            pltpu.VMEM((nchain, 1, MOBA_BLOCK), f32),
        ],
        compiler_params=pltpu.CompilerParams(
            dimension_semantics=("parallel", "parallel", "arbitrary"), vmem_limit_bytes=VMEM_LIMIT),
        name="moba",
    )(slopes, qt, kaug, vaug, ksum.reshape(b, nblk, MOBA_WIDTH), arows, cmask)


def _merge_kernel(x_ref, g_ref, d0_ref, d1_ref, d2_ref, ybt_ref, wgate_ref, wua_ref, wub_ref, wo_ref, o_ref,
                  nat_ref, *, tm):
    rows = GROUP_ROWS

    def token_order(g, c, part):
        d_ref, dil = (d0_ref, d1_ref, d2_ref)[g], DSA_GROUPS[g][1]
        per = rows // dil
        lanes = slice(c * LANES, (c + 1) * LANES)
        if dil == 1:
            return d_ref[0, 0, part * rows:(part + 1) * rows, lanes]
        for r in range(dil):
            nat_ref[g, c, pl.ds(part * rows + r, per, stride=dil), :] = (
                d_ref[0, r, part * per:(part + 1) * per, lanes])
        return nat_ref[g, c, part * rows:(part + 1) * rows, :]

    for part in range(tm // rows):
        sl = slice(part * rows, (part + 1) * rows)
        pairs = []
        for hp in range(2):
            outs = [token_order(g, 2 * hp, part) for g in range(len(DSA_GROUPS))]
            lses = [token_order(g, 2 * hp + 1, part) for g in range(len(DSA_GROUPS))]
            mx = jnp.maximum(jnp.maximum(lses[0], lses[1]), lses[2])
            es = [jnp.exp2(l - mx) for l in lses]
            den = es[0] + es[1] + es[2]
            pairs.append((es[0] * outs[0] + es[1] * outs[1] + es[2] * outs[2]) / den)
        ya = jnp.concatenate(pairs, axis=1).astype(bf16)
        x = x_ref[0, sl, :]
        h = _rmsnorm(x, g_ref[...]).astype(bf16)
        gates = jax.nn.sigmoid(_dot(h, wgate_ref[...]))
        ta = _dot(ya, wua_ref[...])
        tb = _dot_tn(ybt_ref[0, :, sl], wub_ref[...])
        merged = gates[:, 0:D_MODEL] * ta + gates[:, D_MODEL:] * tb
        o_ref[0, sl, :] = x + _dot(merged.astype(bf16), wo_ref[...])


def _merge(x3d, gain, dsa_outs, ybt, wgate, wua, wub, wo, *, tm):
    b, s, _ = x3d.shape
    tok = lambda w: pl.BlockSpec((1, tm, w), lambda bi, i: (bi, i, 0))
    grp = lambda dil: pl.BlockSpec((1, dil, tm // dil, 4 * LANES), lambda bi, i: (bi, 0, i, 0))
    return pl.pallas_call(
        functools.partial(_merge_kernel, tm=tm),
        out_shape=jax.ShapeDtypeStruct((b, s, D_MODEL), f32),
        grid=(b, s // tm),
        in_specs=[
            tok(D_MODEL), _resident((1, D_MODEL)), *[grp(dil) for _, dil in DSA_GROUPS],
            pl.BlockSpec((1, MOBA_WIDTH, tm), lambda bi, i: (bi, 0, i)),
            _resident((D_MODEL, 2 * D_MODEL)),
            _resident((DSA_OUT, D_MODEL)),
            _resident((MOBA_WIDTH, D_MODEL)),
            _resident((D_MODEL, D_MODEL)),
        ],
        out_specs=tok(D_MODEL),
        scratch_shapes=[pltpu.VMEM((len(DSA_GROUPS), 4, tm, LANES), f32)],
        compiler_params=pltpu.CompilerParams(
            dimension_semantics=("parallel", "parallel"), vmem_limit_bytes=VMEM_LIMIT),
        name="merge",
    )(x3d, gain, *dsa_outs, ybt, wgate, wua, wub, wo)


def _alibi_slopes(n):
    return 2.0 ** (-8.0 * np.arange(1, n + 1, dtype=np.float64) / n)


def _split3(x):
    x = np.asarray(x, np.float32)
    hi = x.astype(bf16)
    r1 = x - hi.astype(np.float32)
    mid = r1.astype(bf16)
    lo = (r1 - mid.astype(np.float32)).astype(bf16)
    return hi, mid, lo


def _moba_constants(s):
    nblk = s // MOBA_BLOCK
    n_arow = K_FEAT - nblk
    pos = np.arange(s)
    kfeat = np.zeros((s, K_FEAT), np.float32)
    kfeat[pos, pos // MOBA_BLOCK] = 1.0
    kfeat[:, nblk:nblk + 3] = 1.0
    kfeat[:, nblk + 3:nblk + 6] = (pos % MOBA_BLOCK)[:, None]
    kfeat[:, nblk + 6:nblk + 9] = ((pos // MOBA_BLOCK) % 2)[:, None]
    slopes = (_alibi_slopes(MOBA_HEADS) * LOG2E).astype(np.float32)
    off = np.arange(MOBA_BLOCK, dtype=np.float32)
    wide = lambda v: np.broadcast_to(v[:, None], (MOBA_HEADS, MOBA_BLOCK))
    qterm = _split3(-slopes[:, None] * off[None, :])
    sterm = _split3(wide(slopes))
    bterm = _split3(wide(slopes * np.float32(MOBA_BLOCK)))
    rows = np.zeros((MOBA_HEADS, n_arow, MOBA_BLOCK), bf16)
    rows[:, 0:9] = np.stack(list(qterm) + list(sterm) + list(bterm), axis=1)
    arows = rows.reshape(MOBA_HEADS // 2, 2, n_arow, MOBA_BLOCK)
    r = np.arange(MOBA_TILE)[None, :, None]
    c = np.arange(MOBA_BLOCK)[None, None, :] + MOBA_BLOCK * np.arange(2)[:, None, None]
    cmask = np.where(r <= c, 0.0, NEG).astype(np.float32)
    return jnp.asarray(kfeat.astype(bf16)), jnp.asarray(arows), jnp.asarray(slopes), jnp.asarray(cmask)


def _dsa_bias(group, dilation):
    slopes = _alibi_slopes(DSA_HEADS)[group * DSA_HEADS_PER_GROUP:(group + 1) * DSA_HEADS_PER_GROUP]
    slopes = (slopes * LOG2E * dilation).astype(np.float32).reshape(2, 2, 1, 1, 1)
    qi = np.arange(DSA_TQ)[:, None]
    kj = np.arange(DSA_BAND)[None, :]
    delta = np.stack([qi - kj, qi + DSA_BLK - kj])[None, None]
    valid = (delta >= 0) & (delta <= DSA_BLK)
    return jnp.asarray(np.where(valid, -slopes * delta.astype(np.float32), NEG).astype(np.float32))


def _layer(x, norm_ffn1, ffn1_gate, ffn1_up, ffn1_down, norm_mix, w_in, w_up_a, w_up_b, w_out,
           norm_ffn2, ffn2_gate, ffn2_up, ffn2_down, gain_final, *, final_norm):
    b, s, _ = x.shape
    tm = 512
    tm_merge = 1024
    tm_ffn = 1024
    row = lambda g: g.reshape(1, D_MODEL).astype(f32)
    qscale = HEAD_DIM ** -0.5 * LOG2E
    qa_scale = np.ones((1, 3 * DSA_WIDTH), np.float32)
    qa_scale[:, :DSA_WIDTH] = qscale
    qa_scale = jnp.asarray(qa_scale)
    x1, wa, wqb, wkb, wvb, wgate = _ffn(
        x.reshape(b * s, D_MODEL), row(norm_ffn1), ffn1_gate.astype(bf16), ffn1_up.astype(bf16),
        ffn1_down.astype(bf16), row(gain_final), final_norm=False, tm=tm_ffn, w_in=w_in,
        qa_scale=qa_scale, qb_scale=float(qscale))
    x1 = x1.reshape(b, s, D_MODEL)

    kfeat, arows, slopes_b, cmask = _moba_constants(s)
    vpad = np.zeros((V_ROWS - HEAD_DIM, tm), np.float32)
    vpad[0] = 1.0
    vpad = jnp.asarray(vpad.astype(bf16))
    a0, a1, a2, kaug, qt, vaug, ksum, w2_gate, w2_up, w2_down, wua, wub, wo = _proj(
        x1, row(norm_mix), wa, wqb, wkb, wvb, kfeat, vpad,
        (ffn2_gate, ffn2_up, ffn2_down, w_up_a, w_up_b, w_out), tm=tm)

    dsa_outs = [
        _dsa_group(a, _dsa_bias(gi, dil), group=gi)
        for gi, (a, (_, dil)) in enumerate(zip((a0, a1, a2), DSA_GROUPS))
    ]
    ybt = _moba(slopes_b, qt, kaug, vaug, ksum, arows, cmask)
    x2 = _merge(x1, row(norm_mix), dsa_outs, ybt, wgate, wua, wub, wo, tm=tm_merge)
    x3 = _ffn(x2.reshape(b * s, D_MODEL), row(norm_ffn2), w2_gate, w2_up, w2_down, row(gain_final),
              final_norm=final_norm, tm=tm_ffn)
    return x3.reshape(b, s, D_MODEL)


def kernel(x, norm_ffn1, ffn1_gate, ffn1_up, ffn1_down, norm_mix, w_in, w_up_a, w_up_b, w_out,
           norm_ffn2, ffn2_gate, ffn2_up, ffn2_down, norm_final):
    depth = norm_ffn1.shape[0]
    for layer in range(depth):
        last = layer == depth - 1
        x = _layer(x, norm_ffn1[layer], ffn1_gate[layer], ffn1_up[layer], ffn1_down[layer],
                   norm_mix[layer], w_in[layer], w_up_a[layer], w_up_b[layer], w_out[layer],
                   norm_ffn2[layer], ffn2_gate[layer], ffn2_up[layer], ffn2_down[layer],
                   norm_final, final_norm=last)
    return x
```
